```python
import math
import jax, jax.numpy as jnp
from jax import lax
import numpy as np

D_MODEL = 1024
BATCH = 4
SEQ = 4096
DEPTH = 2

GRID_W = 64
CTX_LEN = 256
D_FF = 2816
N_MOD = 9
EPS = 1e-6

HY_C = 256
HY_GROUPS = 4
HY_ORDER = 2
HY_BANDS = 16
HY_EMB = 1 + 2 * HY_BANDS
HY_FFN = 64
HY_DECAY_TARGET = 1e-2
HY_FAST_PCT = 0.3
HY_SLOW_PCT = 1.5
HY_WINDOW_SHIFT = 0.05

SG_C = 256
SG_GROUPS = 4
SG_CHUNK = 128

MLA_HEADS = 8
MLA_NOPE = 64
MLA_ROPE = 32
MLA_V = 64
MLA_Q_RANK = 384
MLA_KV_RANK = 256
MLA_SCALE = (MLA_NOPE + MLA_ROPE) ** -0.5
ROPE_BASE = 10000.0
Q_BLOCK = 128

MIX_WIDTH = HY_C + SG_C + MLA_HEADS * MLA_V
OFF_SG = 3 * HY_C
OFF_Q = OFF_SG + 2 * SG_C
OFF_KV = OFF_Q + MLA_Q_RANK
OFF_KR = OFF_KV + MLA_KV_RANK
N_IN = OFF_KR + MLA_ROPE

kernel_name = "hybrid_hyena_sgu_mla_dit_trunk"


def rms(x):
    xf = x.astype(jnp.float32)
    return (xf * lax.rsqrt(jnp.mean(xf * xf, axis=-1, keepdims=True) + EPS)).astype(x.dtype)


def layer_norm(x, g, b):
    xf = x.astype(jnp.float32)
    mu = jnp.mean(xf, axis=-1, keepdims=True)
    var = jnp.mean(jnp.square(xf - mu), axis=-1, keepdims=True)
    return ((xf - mu) * lax.rsqrt(var + EPS)).astype(x.dtype) * g + b


def modulate(x, shift, scale):
    return rms(x) * (1 + scale) + shift


def swiglu(h, wg, wu, wd):
    return (jax.nn.silu(h @ wg) * (h @ wu)) @ wd


def short_conv(u, w, b):
    L = u.shape[1]
    up = jnp.pad(u, ((0, 0), (1, 1), (0, 0)))
    return up[:, 0:L] * w[0] + up[:, 1:L + 1] * w[1] + up[:, 2:L + 2] * w[2] + b


def hyena_filters(L, w1, b1, w2, b2, w3, freq):
    pos = jnp.arange(L, dtype=jnp.float32)
    t = pos / max(L - 1, 1)
    w = 2.0 * math.pi * pos / L
    bands = jnp.linspace(1e-4, HY_BANDS - 1, HY_BANDS, dtype=jnp.float32)
    ang = w[:, None] * bands[None, :]
    z = jnp.concatenate([t[:, None], jnp.cos(ang), -jnp.sin(ang)], axis=-1).astype(w1.dtype)
    h = jnp.sin(freq[0] * (z @ w1 + b1))
    h = jnp.sin(freq[1] * (h @ w2 + b2))
    h = (h @ w3).reshape(L, HY_ORDER, 2, HY_C)
    min_decay = abs(math.log(HY_DECAY_TARGET) / HY_SLOW_PCT)
    max_decay = abs(math.log(HY_DECAY_TARGET) / HY_FAST_PCT)
    deltas = jnp.linspace(min_decay, max_decay, HY_C, dtype=jnp.float32)
    window = jnp.exp(-t[:, None, None, None] * deltas) + HY_WINDOW_SHIFT
    return h * window.astype(h.dtype)


def bidir_fftconv(u, h_fwd, h_bwd):
    L = u.shape[1]
    k = jnp.concatenate([h_fwd[:1] + h_bwd[:1], h_fwd[1:], jnp.zeros_like(h_fwd[:1]), h_bwd[:0:-1]], axis=0)
    uf = jnp.fft.rfft(u.astype(jnp.float32), n=2 * L, axis=1)
    kf = jnp.fft.rfft(k.astype(jnp.float32), n=2 * L, axis=0)
    y = jnp.fft.irfft(uf * kf[None], n=2 * L, axis=1)[:, :L]
    return y.astype(u.dtype)


def hyena_mixer(p3, conv_w, conv_b, filt, bias):
    z = short_conv(p3, conv_w, conv_b)
    x1, x2, v = jnp.split(z, 3, axis=-1)
    gates = (x1, x2)
    for o in range(HY_ORDER):
        v = gates[o] * (bidir_fftconv(v, filt[:, o, 0], filt[:, o, 1]) + v * bias[o])
    return v


def chunk_sgu(p2, ln_g, ln_b, w_s, b_s):
    z = jax.nn.gelu(p2, approximate=False)
    u, v = jnp.split(z, 2, axis=-1)
    v = layer_norm(v, ln_g, ln_b)
    B, L, _ = v.shape
    v = v.reshape(B, L // SG_CHUNK, SG_CHUNK, SG_GROUPS, SG_C // SG_GROUPS)
    s = jnp.einsum('gpq,bnqgc->bnpgc', w_s, v) + b_s.T[None, None, :, :, None]
    return u * s.reshape(B, L, SG_C)


def axial_rope_tables(n):
    n_rows = n // GRID_W
    rows = jnp.repeat(jnp.arange(n_rows, dtype=jnp.float32), GRID_W)
    cols = jnp.tile(jnp.arange(GRID_W, dtype=jnp.float32), n_rows)
    quarter = MLA_ROPE // 4
    inv = 1.0 / (ROPE_BASE ** (jnp.arange(quarter, dtype=jnp.float32) / quarter))
    ang = jnp.stack([rows[:, None] * inv, cols[:, None] * inv], axis=1)
    return jnp.cos(ang), jnp.sin(ang)


def apply_axial_rope(x, cos, sin):
    shp = x.shape
    quarter = MLA_ROPE // 4
    xr = x.reshape(shp[:-1] + (2, 2, quarter))
    x1, x2 = xr[..., 0, :], xr[..., 1, :]
    bshape = (shp[1],) + (1,) * (x.ndim - 3) + (2, quarter)
    c = cos.reshape(bshape).astype(x.dtype)
    s = sin.reshape(bshape).astype(x.dtype)
    return jnp.stack([x1 * c - x2 * s, x2 * c + x1 * s], axis=-2).reshape(shp)


def mla_q(cq, g, w_uq):
    B, n, _ = cq.shape
    q = ((rms(cq) * g) @ w_uq).reshape(B, n, MLA_HEADS, MLA_NOPE + MLA_ROPE)
    return q[..., :MLA_NOPE], q[..., MLA_NOPE:]


def mla_kv(ckv, g, w_ukv):
    B, n, _ = ckv.shape
    kv = ((rms(ckv) * g) @ w_ukv).reshape(B, n, MLA_HEADS, MLA_NOPE + MLA_V)
    return kv[..., :MLA_NOPE], kv[..., MLA_NOPE:]


def attend(q_nope, q_rope, k_nope, k_rope, v):
    s = jnp.einsum('bqhd,bkhd->bhqk', q_nope, k_nope) + jnp.einsum('bqhr,bkr->bhqk', q_rope, k_rope)
    p = jax.nn.softmax(s.astype(jnp.float32) * MLA_SCALE, axis=-1).astype(v.dtype)
    return jnp.einsum('bhqk,bkhd->bqhd', p, v)


def attend_blocks(q_nope, q_rope, k_nope, k_rope, v):
    B, n = q_nope.shape[:2]
    nb = n // Q_BLOCK

    def to_blocks(t):
        return t.reshape((B, nb, Q_BLOCK) + t.shape[2:]).swapaxes(0, 1)

    out = lax.map(lambda qs: attend(qs[0], qs[1], k_nope, k_rope, v), (to_blocks(q_nope), to_blocks(q_rope)))
    return out.swapaxes(0, 1).reshape(B, n, MLA_HEADS * MLA_V)


def setup_inputs(seed: int = 0) -> dict:
    key = jax.random.key(seed)
    ks = iter(jax.random.split(key, 40))

    def nrm(shape, scale):
        return scale * jax.random.normal(next(ks), shape, jnp.float32)

    D = D_MODEL
    return {
        "x": nrm((BATCH, SEQ, D), 1.0),
        "c": nrm((BATCH, D), 1.0),
        "ctx": nrm((BATCH, CTX_LEN, D), 1.0),
        "c_ctx": nrm((D,), 1.0),
        "ada_w": nrm((DEPTH, D, N_MOD * D), 0.5 * D ** -0.5),
        "ada_b": nrm((DEPTH, N_MOD * D), 0.02),
        "ffn1_wg": nrm((DEPTH, D, D_FF), D ** -0.5),
        "ffn1_wu": nrm((DEPTH, D, D_FF), D ** -0.5),
        "ffn1_wd": nrm((DEPTH, D_FF, D), D_FF ** -0.5),
        "w_in": nrm((DEPTH, D, N_IN), D ** -0.5),
        "w_out": nrm((DEPTH, MIX_WIDTH, D), MIX_WIDTH ** -0.5),
        "hy_conv_w": nrm((DEPTH, 3, 3 * HY_C), 3 ** -0.5),
        "hy_conv_b": nrm((DEPTH, 3 * HY_C), 0.02),
        "hy_w1": nrm((DEPTH, HY_EMB, HY_FFN), 2.0 * HY_EMB ** -0.5),
        "hy_b1": nrm((DEPTH, HY_FFN), 0.1),
        "hy_w2": nrm((DEPTH, HY_FFN, HY_FFN), 2.0 * HY_FFN ** -0.5),
        "hy_b2": nrm((DEPTH, HY_FFN), 0.1),
        "hy_w3": nrm((DEPTH, HY_FFN, HY_ORDER * 2 * HY_C), 0.005),
        "hy_freq": 1.0 + nrm((DEPTH, 2, HY_FFN), 0.1),
        "hy_bias": nrm((DEPTH, HY_ORDER, HY_C), 0.2),
        "sg_ln_g": 1.0 + nrm((DEPTH, SG_C), 0.05),
        "sg_ln_b": nrm((DEPTH, SG_C), 0.02),
        "sg_w": nrm((DEPTH, SG_GROUPS, SG_CHUNK, SG_CHUNK), 0.5 * SG_CHUNK ** -0.5),
        "sg_b": 1.0 + nrm((DEPTH, SG_GROUPS, SG_CHUNK), 0.1),
        "mla_q_norm": 1.0 + nrm((DEPTH, MLA_Q_RANK), 0.05),
        "mla_w_uq": nrm((DEPTH, MLA_Q_RANK, MLA_HEADS * (MLA_NOPE + MLA_ROPE)), MLA_Q_RANK ** -0.5),
        "mla_kv_norm": 1.0 + nrm((DEPTH, MLA_KV_RANK), 0.05),
        "mla_w_ukv": nrm((DEPTH, MLA_KV_RANK, MLA_HEADS * (MLA_NOPE + MLA_V)), MLA_KV_RANK ** -0.5),
        "ffn2_wg": nrm((DEPTH, D, D_FF), D ** -0.5),
        "ffn2_wu": nrm((DEPTH, D, D_FF), D ** -0.5),
        "ffn2_wd": nrm((DEPTH, D_FF, D), D_FF ** -0.5),
        "final_norm": 1.0 + nrm((D,), 0.05),
    }


def reference(x, c, ctx, c_ctx, ada_w, ada_b, ffn1_wg, ffn1_wu, ffn1_wd, w_in, w_out,
              hy_conv_w, hy_conv_b, hy_w1, hy_b1, hy_w2, hy_b2, hy_w3, hy_freq, hy_bias,
              sg_ln_g, sg_ln_b, sg_w, sg_b, mla_q_norm, mla_w_uq, mla_kv_norm, mla_w_ukv,
              ffn2_wg, ffn2_wu, ffn2_wd, final_norm):
    B, n_lat, D = x.shape
    n_ctx = ctx.shape[1]
    cos, sin = axial_rope_tables(n_lat)
    xc = ctx
    for l in range(DEPTH):
        last = l == DEPTH - 1
        mod = (jax.nn.silu(c) @ ada_w[l] + ada_b[l]).reshape(B, N_MOD, 1, D)
        modc = (jax.nn.silu(c_ctx) @ ada_w[l] + ada_b[l]).reshape(N_MOD, D)

        x = x + 0.5 * mod[:, 2] * swiglu(modulate(x, mod[:, 0], mod[:, 1]), ffn1_wg[l], ffn1_wu[l], ffn1_wd[l])
        xc = xc + 0.5 * modc[2] * swiglu(modulate(xc, modc[0], modc[1]), ffn1_wg[l], ffn1_wu[l], ffn1_wd[l])

        h = modulate(x, mod[:, 3], mod[:, 4])
        hc = modulate(xc, modc[3], modc[4])
        proj = h @ w_in[l]
        projc = hc @ (w_in[l][:, OFF_KV:] if last else w_in[l])

        kc_nope, vc = mla_kv(projc[..., -(MLA_KV_RANK + MLA_ROPE):-MLA_ROPE], mla_kv_norm[l], mla_w_ukv[l])
        kc_rope = projc[..., -MLA_ROPE:]

        k_nope, v = mla_kv(proj[..., OFF_KV:OFF_KR], mla_kv_norm[l], mla_w_ukv[l])
        k_rope = apply_axial_rope(proj[..., OFF_KR:N_IN], cos, sin)
        q_nope, q_rope = mla_q(proj[..., OFF_Q:OFF_KV], mla_q_norm[l], mla_w_uq[l])
        q_rope = apply_axial_rope(q_rope, cos, sin)
        y_attn = attend_blocks(q_nope, q_rope,
                               jnp.concatenate([kc_nope, k_nope], axis=1),
                               jnp.concatenate([kc_rope, k_rope], axis=1),
                               jnp.concatenate([vc, v], axis=1))
        filt = hyena_filters(n_lat, hy_w1[l], hy_b1[l], hy_w2[l], hy_b2[l], hy_w3[l], hy_freq[l])
        y_hy = hyena_mixer(proj[..., :OFF_SG], hy_conv_w[l], hy_conv_b[l], filt, hy_bias[l])
        y_sg = chunk_sgu(proj[..., OFF_SG:OFF_Q], sg_ln_g[l], sg_ln_b[l], sg_w[l], sg_b[l])
        x = x + mod[:, 5] * (jnp.concatenate([y_hy, y_sg, y_attn], axis=-1) @ w_out[l])

        if not last:
            filt_c = hyena_filters(n_ctx, hy_w1[l], hy_b1[l], hy_w2[l], hy_b2[l], hy_w3[l], hy_freq[l])
            yc_hy = hyena_mixer(projc[..., :OFF_SG], hy_conv_w[l], hy_conv_b[l], filt_c, hy_bias[l])
            yc_sg = chunk_sgu(projc[..., OFF_SG:OFF_Q], sg_ln_g[l], sg_ln_b[l], sg_w[l], sg_b[l])
            qc_nope, qc_rope = mla_q(projc[..., OFF_Q:OFF_KV], mla_q_norm[l], mla_w_uq[l])
            yc_attn = attend(qc_nope, qc_rope, kc_nope, kc_rope, vc).reshape(B, n_ctx, MLA_HEADS * MLA_V)
            xc = xc + modc[5] * (jnp.concatenate([yc_hy, yc_sg, yc_attn], axis=-1) @ w_out[l])

        x = x + 0.5 * mod[:, 8] * swiglu(modulate(x, mod[:, 6], mod[:, 7]), ffn2_wg[l], ffn2_wu[l], ffn2_wd[l])
        if not last:
            xc = xc + 0.5 * modc[8] * swiglu(modulate(xc, modc[6], modc[7]), ffn2_wg[l], ffn2_wu[l], ffn2_wd[l])
    return rms(x) * final_norm
```

```python
import functools
import math

import numpy as np
import jax
import jax.numpy as jnp
from jax import lax
from jax.experimental import pallas as pl
from jax.experimental.pallas import tpu as pltpu

F32 = jnp.float32
BF = jnp.bfloat16

D_MODEL = 1024
D_FF = 2816
N_MOD = 9
EPS = 1e-6
GRID_W = 64

HY_C = 256
HY_BANDS = 16
HY_EMB = 1 + 2 * HY_BANDS
HY_FFN = 64
HY_DECAY_TARGET = 1e-2
HY_FAST_PCT = 0.3
HY_SLOW_PCT = 1.5
HY_WINDOW_SHIFT = 0.05

SG_C = 256
SG_GROUPS = 4
SG_CHUNK = 128

MLA_HEADS = 8
MLA_NOPE = 64
MLA_ROPE = 32
MLA_V = 64
MLA_Q_RANK = 384
MLA_KV_RANK = 256
MLA_SCALE = (MLA_NOPE + MLA_ROPE) ** -0.5
ROPE_BASE = 10000.0

OFF_SG = 3 * HY_C
OFF_Q = OFF_SG + 2 * SG_C
OFF_KV = OFF_Q + MLA_Q_RANK
OFF_KR = OFF_KV + MLA_KV_RANK
N_IN = OFF_KR + MLA_ROPE
N_IN_PAD = 2048

LANE = 128
HEAD_PAD = 128
FFN_CHUNK = 256
VMEM_LIMIT = 56 * 1024 * 1024

DFT_N2 = LANE


def _cparams(n_axes, vmem=VMEM_LIMIT):
    return pltpu.CompilerParams(dimension_semantics=("arbitrary",) * n_axes, vmem_limit_bytes=vmem)


def _mod_spec(d, seq, tm, cond_row):
    if cond_row is None:
        tiles_per_seq = seq // tm
        return pl.BlockSpec((None, N_MOD, d), lambda i: (i // tiles_per_seq, 0, 0))
    return pl.BlockSpec((None, N_MOD, d), lambda i: (cond_row, 0, 0))


def _rms(x):
    return x * lax.rsqrt(jnp.mean(x * x, axis=-1, keepdims=True) + EPS)


def _dot(a, b):
    return jnp.dot(a, b, preferred_element_type=F32)


def _mod_kernel(c_ref, w_ref, b_ref, o_ref):
    c = c_ref[...]
    s = c * jax.nn.sigmoid(c)
    o_ref[0] = _dot(s.astype(BF), w_ref[0].astype(BF)) + b_ref[0]


def _modulation(cc, ada_w, ada_b):
    depth, d, n = ada_w.shape
    tn = 1152
    return pl.pallas_call(
        _mod_kernel,
        grid=(depth, n // tn),
        in_specs=[
            pl.BlockSpec((8, d), lambda l, j: (0, 0)),
            pl.BlockSpec((1, d, tn), lambda l, j: (l, 0, j)),
            pl.BlockSpec((1, 1, tn), lambda l, j: (l, 0, j)),
        ],
        out_specs=pl.BlockSpec((1, 8, tn), lambda l, j: (l, 0, j)),
        out_shape=jax.ShapeDtypeStruct((depth, 8, n), F32),
        compiler_params=_cparams(2),
        name="adaln_mod",
    )(cc, ada_w, ada_b.reshape(depth, 1, n))


def _ffn_kernel(*refs, k0, final):
    if final:
        x_ref, mod_ref, wg_ref, wu_ref, wd_ref, fn_ref, o_ref, a_ref = refs
    else:
        x_ref, mod_ref, wg_ref, wu_ref, wd_ref, o_ref, a_ref = refs
    x = x_ref[...]
    shift = mod_ref[k0:k0 + 1, :]
    scale = mod_ref[k0 + 1:k0 + 2, :]
    gate = mod_ref[k0 + 2:k0 + 3, :]
    hb = (_rms(x) * (1.0 + scale) + shift).astype(BF)
    for j in range(D_FF // FFN_CHUNK):
        cols = slice(j * FFN_CHUNK, (j + 1) * FFN_CHUNK)
        g = _dot(hb, wg_ref[:, cols])
        u = _dot(hb, wu_ref[:, cols])
        a_ref[:, cols] = (g * jax.nn.sigmoid(g) * u).astype(BF)
    y = x + (0.5 * gate) * _dot(a_ref[...], wd_ref[...])
    if final:
        y = _rms(y) * fn_ref[...]
    o_ref[...] = y


def _ffn(x, mod_l, k0, wg, wu, wd, *, tm, seq, cond_row=None, final_norm=None):
    t, d = x.shape
    final = final_norm is not None
    const = lambda i: (0, 0)
    in_specs = [
        pl.BlockSpec((tm, d), lambda i: (i, 0)),
        _mod_spec(d, seq, tm, cond_row),
        pl.BlockSpec((d, D_FF), const, pipeline_mode=pl.Buffered(1)),
        pl.BlockSpec((d, D_FF), const, pipeline_mode=pl.Buffered(1)),
        pl.BlockSpec((D_FF, d), const, pipeline_mode=pl.Buffered(1)),
    ]
    args = [x, mod_l, wg, wu, wd]
    if final:
        in_specs.append(pl.BlockSpec((1, d), const))
        args.append(final_norm.reshape(1, d))
    return pl.pallas_call(
        functools.partial(_ffn_kernel, k0=k0, final=final),
        grid=(t // tm,),
        in_specs=in_specs,
        out_specs=pl.BlockSpec((tm, d), lambda i: (i, 0)),
        out_shape=jax.ShapeDtypeStruct((t, d), F32),
        scratch_shapes=[pltpu.VMEM((tm, D_FF), BF)],
        compiler_params=_cparams(1),
        name="ffn",
    )(*args)


def _gelu(x):
    return 0.5 * x * (1.0 + lax.erf(x * (1.0 / math.sqrt(2.0))))


def _mix_in_kernel(x_ref, mod_ref, wall_ref, ca_ref, sb_ref, ck_ref, gq_ref, wq_ref, gkv_ref, wkv_ref,
                   lng_ref, lnb_ref, wsg_ref, bsg_ref,
                   p3_ref, ysg_ref, q_ref, k_ref, v_ref, *, tm):
    x = x_ref[...]
    shift = mod_ref[3:4, :]
    scale = mod_ref[4:5, :]
    hb = (_rms(x) * (1.0 + scale) + shift).astype(BF)
    proj = _dot(hb, wall_ref[...])
    p3_ref[...] = proj[:, 0:OFF_SG]

    z = _gelu(proj[:, OFF_SG:OFF_Q])
    u = z[:, 0:SG_C]
    v = z[:, SG_C:2 * SG_C]
    mu = jnp.mean(v, axis=-1, keepdims=True)
    vc = v - mu
    var = jnp.mean(vc * vc, axis=-1, keepdims=True)
    vb = ((vc * lax.rsqrt(var + EPS)) * lng_ref[...] + lnb_ref[...]).astype(BF)
    lane_g = lax.broadcasted_iota(jnp.int32, (SG_CHUNK, SG_C), 1) // (SG_C // SG_GROUPS)
    for j in range(tm // SG_CHUNK):
        rows = slice(j * SG_CHUNK, (j + 1) * SG_CHUNK)
        o = _dot(wsg_ref[...], vb[rows, :])
        s = o[(SG_GROUPS - 1) * SG_CHUNK:, :]
        for g in range(SG_GROUPS - 2, -1, -1):
            s = jnp.where(lane_g == g, o[g * SG_CHUNK:(g + 1) * SG_CHUNK, :], s)
        ysg_ref[rows, :] = (u[rows, :] * (s + bsg_ref[...])).astype(BF)

    cqn = (_rms(proj[:, OFF_Q:OFF_KV]) * gq_ref[...]).astype(BF)
    qq = _dot(cqn, wq_ref[...])
    ca = ca_ref[...]
    sb = sb_ref[...]
    nq = MLA_HEADS * HEAD_PAD
    for h in range(MLA_HEADS):
        cols = slice(h * HEAD_PAD, (h + 1) * HEAD_PAD)
        cols_sw = slice(nq + h * HEAD_PAD, nq + (h + 1) * HEAD_PAD)
        q_ref[:, cols] = (qq[:, cols] * ca + qq[:, cols_sw] * sb).astype(BF)

    kvn = (_rms(proj[:, OFF_KV:OFF_KR]) * gkv_ref[...]).astype(BF)
    kv = _dot(kvn, wkv_ref[...])
    v_ref[...] = kv[:, nq:].astype(BF)
    t = proj[:, OFF_KR:OFF_KR + LANE] * ck_ref[...]
    krr = t + pltpu.roll(t, LANE - MLA_ROPE, 1)
    lane = lax.broadcasted_iota(jnp.int32, (tm, LANE), 1)
    in_rope = (lane >= MLA_NOPE) & (lane < MLA_NOPE + MLA_ROPE)
    krp = jnp.where(in_rope, pltpu.roll(krr, MLA_NOPE, 1), 0.0)
    for h in range(MLA_HEADS):
        cols = slice(h * HEAD_PAD, (h + 1) * HEAD_PAD)
        k_ref[:, cols] = (kv[:, cols] + krp).astype(BF)


def _mix_in(x, mod_l, cond_row, seq, w, tabs, *, tm):
    t, d = x.shape
    tiles_per_seq = seq // tm
    const = lambda i: (0, 0)
    tab = lambda i: (i % tiles_per_seq, 0)
    nq = MLA_HEADS * HEAD_PAD
    nv = MLA_HEADS * MLA_V
    row = lambda i: (i, 0)
    return pl.pallas_call(
        functools.partial(_mix_in_kernel, tm=tm),
        grid=(t // tm,),
        in_specs=[
            pl.BlockSpec((tm, d), row),
            _mod_spec(d, seq, tm, cond_row),
            pl.BlockSpec((d, N_IN_PAD), const),
            pl.BlockSpec((tm, LANE), tab),
            pl.BlockSpec((tm, LANE), tab),
            pl.BlockSpec((tm, LANE), tab),
            pl.BlockSpec((1, MLA_Q_RANK), const),
            pl.BlockSpec((MLA_Q_RANK, 2 * nq), const),
            pl.BlockSpec((1, MLA_KV_RANK), const),
            pl.BlockSpec((MLA_KV_RANK, nq + nv), const),
            pl.BlockSpec((1, SG_C), const),
            pl.BlockSpec((1, SG_C), const),
            pl.BlockSpec((SG_GROUPS * SG_CHUNK, SG_CHUNK), const),
            pl.BlockSpec((SG_CHUNK, SG_C), const),
        ],
        out_specs=[
            pl.BlockSpec((tm, OFF_SG), row),
            pl.BlockSpec((tm, SG_C), row),
            pl.BlockSpec((tm, nq), row),
            pl.BlockSpec((tm, nq), row),
            pl.BlockSpec((tm, nv), row),
        ],
        out_shape=[
            jax.ShapeDtypeStruct((t, OFF_SG), F32),
            jax.ShapeDtypeStruct((t, SG_C), BF),
            jax.ShapeDtypeStruct((t, nq), BF),
            jax.ShapeDtypeStruct((t, nq), BF),
            jax.ShapeDtypeStruct((t, nv), BF),
        ],
        compiler_params=_cparams(1),
        name="mix_in",
    )(x, mod_l, w["w_all"], tabs["ca"], tabs["sb"], tabs["ck"], w["gq"], w["wq"], w["gkv"], w["wkv"],
      w["lng"], w["lnb"], w["wsg"], w["bsg"])


def _attn_kernel(*refs, with_lat):
    if with_lat:
        q_ref, kc_ref, vc_ref, kl_ref, vl_ref, o_ref = refs
    else:
        q_ref, kc_ref, vc_ref, o_ref = refs
    nt = (((1,), (1,)), ((), ()))
    tq = q_ref.shape[0]
    outs = []
    for h in range(2):
        cols = slice(h * HEAD_PAD, (h + 1) * HEAD_PAD)
        q = q_ref[:, cols]
        sc = lax.dot_general(q, kc_ref[:, cols], nt, preferred_element_type=F32)
        m = jnp.max(sc, axis=-1, keepdims=True)
        if with_lat:
            sl = lax.dot_general(q, kl_ref[:, cols], nt, preferred_element_type=F32)
            m = jnp.maximum(m, jnp.max(sl, axis=-1, keepdims=True))
        pc = jnp.exp(sc - m)
        den = jnp.sum(pc, axis=-1, keepdims=True)
        o = _dot(pc.astype(BF), vc_ref[...])
        if with_lat:
            pl_ = jnp.exp(sl - m)
            den = den + jnp.sum(pl_, axis=-1, keepdims=True)
            o = o + _dot(pl_.astype(BF), vl_ref[...])
        outs.append(o / den)
    lane = lax.broadcasted_iota(jnp.int32, (tq, 2 * MLA_V), 1)
    o_ref[...] = jnp.where(lane < MLA_V, outs[0], outs[1]).astype(BF)


def _attention(q, kc, vc, kl, vl, *, batch, nq_seq, n_ctx, tq):
    with_lat = kl is not None
    t = q.shape[0]
    nqt = nq_seq // tq
    hp = MLA_HEADS // 2
    in_specs = [
        pl.BlockSpec((tq, 2 * HEAD_PAD), lambda b, p, i: (b * nqt + i, p)),
        pl.BlockSpec((n_ctx, 2 * HEAD_PAD), lambda b, p, i: (b, p)),
        pl.BlockSpec((n_ctx, 2 * MLA_V), lambda b, p, i: (b, p)),
    ]
    args = [q, kc, vc]
    if with_lat:
        in_specs += [
            pl.BlockSpec((nq_seq, 2 * HEAD_PAD), lambda b, p, i: (b, p)),
            pl.BlockSpec((nq_seq, 2 * MLA_V), lambda b, p, i: (b, p)),
        ]
        args += [kl, vl]
    return pl.pallas_call(
        functools.partial(_attn_kernel, with_lat=with_lat),
        grid=(batch, hp, nqt),
        in_specs=in_specs,
        out_specs=pl.BlockSpec((tq, 2 * MLA_V), lambda b, p, i: (b * nqt + i, p)),
        out_shape=jax.ShapeDtypeStruct((t, MLA_HEADS * MLA_V), BF),
        compiler_params=_cparams(3),
        name="attn_lat" if with_lat else "attn_ctx",
    )(*args)


def _wout_kernel(x_ref, mod_ref, yh_ref, ys_ref, ya_ref, w_ref, o_ref):
    acc = _dot(yh_ref[...].astype(BF), w_ref[0:HY_C, :])
    acc = acc + _dot(ys_ref[...], w_ref[HY_C:HY_C + SG_C, :])
    acc = acc + _dot(ya_ref[...], w_ref[HY_C + SG_C:, :])
    o_ref[...] = x_ref[...] + mod_ref[5:6, :] * acc


def _wout(x, mod_l, cond_row, seq, yh, ys, ya, w, *, tm):
    t, d = x.shape
    row = lambda i: (i, 0)
    return pl.pallas_call(
        _wout_kernel,
        grid=(t // tm,),
        in_specs=[
            pl.BlockSpec((tm, d), row),
            _mod_spec(d, seq, tm, cond_row),
            pl.BlockSpec((tm, HY_C), row),
            pl.BlockSpec((tm, SG_C), row),
            pl.BlockSpec((tm, MLA_HEADS * MLA_V), row),
            pl.BlockSpec((HY_C + SG_C + MLA_HEADS * MLA_V, d), lambda i: (0, 0)),
        ],
        out_specs=pl.BlockSpec((tm, d), row),
        out_shape=jax.ShapeDtypeStruct((t, d), F32),
        compiler_params=_cparams(1),
        name="w_out",
    )(x, mod_l, yh, ys, ya, w)


def _filt_kernel(et_ref, w1_ref, b1_ref, f0_ref, w2_ref, b2_ref, f1_ref, w3_ref, win_ref, o_ref):
    e = et_ref[...].astype(BF)
    h = jnp.sin(f0_ref[...] * (_dot(w1_ref[...], e) + b1_ref[...]))
    h = jnp.sin(f1_ref[...] * (_dot(w2_ref[...], h.astype(BF)) + b2_ref[...]))
    o_ref[...] = _dot(w3_ref[0], h.astype(BF)) * win_ref[...]


def _filter_time(tabs, fw, *, pb):
    n2l = tabs["et"].shape[1]
    nb = n2l // pb
    const = lambda i: (0, 0)
    col = lambda i: (0, i)
    rows = 2 * HY_C
    return pl.pallas_call(
        _filt_kernel,
        grid=(nb,),
        in_specs=[
            pl.BlockSpec((HY_FFN, pb), col),
            pl.BlockSpec((HY_FFN, HY_FFN), const),
            pl.BlockSpec((HY_FFN, 1), const),
            pl.BlockSpec((HY_FFN, 1), const),
            pl.BlockSpec((HY_FFN, HY_FFN), const),
            pl.BlockSpec((HY_FFN, 1), const),
            pl.BlockSpec((HY_FFN, 1), const),
            pl.BlockSpec((1, rows, HY_FFN), lambda i: (i // (nb // 2), 0, 0)),
            pl.BlockSpec((rows, pb), col),
        ],
        out_specs=pl.BlockSpec((rows, pb), col),
        out_shape=jax.ShapeDtypeStruct((rows, n2l), F32),
        compiler_params=_cparams(1),
        name="hyena_filter",
    )(tabs["et"], fw["w1t"], fw["b1"], fw["f0"], fw["w2t"], fw["b2"], fw["f1"], fw["w3t"], tabs["win"])


def _dft_stage1(m1, rhs, tw_ref, lhs_ref, c0, n1f):
    r = _dot(m1, rhs)
    br = r[0:n1f, :]
    bi = r[n1f:2 * n1f, :]
    tr = tw_ref[0]
    ti = tw_ref[1]
    pr = (br * tr - bi * ti).astype(BF)
    pi = (br * ti + bi * tr).astype(BF)
    q0 = pl.multiple_of(c0 * n1f, 2 * n1f)
    lhs_ref[pl.ds(q0, n1f), 0:LANE] = pr[:, 0:LANE]
    lhs_ref[pl.ds(q0, n1f), LANE:2 * LANE] = pi[:, 0:LANE]
    lhs_ref[pl.ds(q0 + n1f, n1f), 0:LANE] = pr[:, LANE:2 * LANE]
    lhs_ref[pl.ds(q0 + n1f, n1f), LANE:2 * LANE] = pi[:, LANE:2 * LANE]


def _kf_kernel(k_ref, m1_ref, tw_ref, g2_ref, o_ref, lhs_ref, *, cb, n1f):
    zero_at = n1f // 2

    def body(j, carry):
        c0 = 2 * j
        ka = k_ref[c0]
        kb = k_ref[c0 + 1]
        rowi = lax.broadcasted_iota(jnp.int32, (n1f, 2 * LANE), 0)
        lanei = lax.broadcasted_iota(jnp.int32, (n1f, 2 * LANE), 1)
        pad_tap = (rowi == zero_at) & ((lanei == 0) | (lanei == LANE))
        rhs = jnp.where(pad_tap, 0.0, jnp.concatenate([ka, kb], axis=1)).astype(BF)
        _dft_stage1(m1_ref[...], rhs, tw_ref, lhs_ref, c0, n1f)
        return carry

    lax.fori_loop(0, cb // 2, body, 0)
    o_ref[...] = _dot(lhs_ref[...], g2_ref[...])


def _filter_spectrum(ktime3, consts, *, cb):
    rows, n1f, _ = ktime3.shape
    const2 = lambda i: (0, 0)
    return pl.pallas_call(
        functools.partial(_kf_kernel, cb=cb, n1f=n1f),
        grid=(rows // cb,),
        in_specs=[
            pl.BlockSpec((cb, n1f, LANE), lambda i: (i, 0, 0)),
            pl.BlockSpec((2 * n1f, n1f), const2),
            pl.BlockSpec((2, n1f, 2 * LANE), lambda i: (0, 0, 0)),
            pl.BlockSpec((2 * LANE, 2 * LANE), const2),
        ],
        out_specs=pl.BlockSpec((cb * n1f, 2 * LANE), lambda i: (i, 0)),
        out_shape=jax.ShapeDtypeStruct((rows * n1f, 2 * LANE), F32),
        scratch_shapes=[pltpu.VMEM((cb * n1f, 2 * LANE), BF)],
        compiler_params=_cparams(1),
        name="hyena_filter_spectrum",
    )(ktime3, consts["m1f"], consts["tw"], consts["g2"])


def _hy_lat_kernel(x1_ref, x2_ref, v_ref, cw_ref, hb_ref, kf_ref, m1_ref, tw_ref, g2_ref, g2i_ref, m1i_ref,
                   o_ref, zx1, zx2, zv, vbuf, lhs_ref, c_ref, *, cb, n1h, rb):
    n1f = 2 * n1h
    r = cb * n1h
    rowi = lax.broadcasted_iota(jnp.int32, (r, LANE), 0)
    lanei = lax.broadcasted_iota(jnp.int32, (r, LANE), 1)
    n1i = rowi & (n1h - 1)
    first = (n1i == 0) & (lanei == 0)
    last = (n1i == n1h - 1) & (lanei == LANE - 1)

    def bro(p):
        return jnp.broadcast_to(p, (cb, n1h, LANE)).reshape(r, LANE)

    def sconv(u3, s):
        u = u3.reshape(r, LANE)
        a = pltpu.roll(u, 1, 1)
        prev = jnp.where(lanei == 0, pltpu.roll(a, 1, 0), a)
        prev = jnp.where(first, 0.0, prev)
        f = pltpu.roll(u, LANE - 1, 1)
        nxt = jnp.where(lanei == LANE - 1, pltpu.roll(f, r - 1, 0), f)
        nxt = jnp.where(last, 0.0, nxt)
        return prev * bro(cw_ref[s, 0]) + u * bro(cw_ref[s, 1]) + nxt * bro(cw_ref[s, 2]) + bro(cw_ref[s, 3])

    for b in range(2):
        zx1[b] = sconv(x1_ref[b], 0)
        zx2[b] = sconv(x2_ref[b], 1)
        zv[b] = sconv(v_ref[b], 2)

    def conv_order(o, vin, gate, write):
        def fwd(j, carry):
            c0 = 2 * j
            r0 = pl.multiple_of(c0 * n1h, 2 * n1h)
            zr = vin[0, pl.ds(r0, 2 * n1h), :]
            zi = vin[1, pl.ds(r0, 2 * n1h), :]
            rhs = jnp.concatenate([
                jnp.concatenate([zr[0:n1h], zr[n1h:2 * n1h]], axis=1),
                jnp.concatenate([zi[0:n1h], zi[n1h:2 * n1h]], axis=1)], axis=0).astype(BF)
            _dft_stage1(m1_ref[...], rhs, tw_ref, lhs_ref, c0, n1f)
            return carry

        lax.fori_loop(0, cb // 2, fwd, 0)

        def mid(i, carry):
            q = pl.multiple_of(i * rb, rb)
            x = _dot(lhs_ref[pl.ds(q, rb), :], g2_ref[...])
            kf = kf_ref[o, pl.ds(q, rb), :]
            xr, xi = x[:, 0:LANE], x[:, LANE:]
            kr, ki = kf[:, 0:LANE], kf[:, LANE:]
            y = jnp.concatenate([xr * kr - xi * ki, xr * ki + xi * kr], axis=1).astype(BF)
            c_ref[pl.ds(q, rb), :] = _dot(y, g2i_ref[...])
            return carry

        lax.fori_loop(0, cb * n1f // rb, mid, 0)

        tr = tw_ref[0][:, 0:LANE]
        ti = tw_ref[1][:, 0:LANE]

        def inv(j, carry):
            c0 = 2 * j
            q0 = pl.multiple_of(c0 * n1f, 2 * n1f)
            ca = c_ref[pl.ds(q0, n1f), :]
            cc = c_ref[pl.ds(q0 + n1f, n1f), :]

            def itw(t):
                cr, ci = t[:, 0:LANE], t[:, LANE:]
                return cr * tr + ci * ti, ci * tr - cr * ti

            ar, ai = itw(ca)
            br, bi = itw(cc)
            rhs = jnp.concatenate([jnp.concatenate([ar, br], axis=1),
                                   jnp.concatenate([ai, bi], axis=1)], axis=0).astype(BF)
            out = _dot(m1i_ref[...], rhs)
            r0 = pl.multiple_of(c0 * n1h, 2 * n1h)
            bias = jnp.broadcast_to(hb_ref[o, pl.ds(c0, 2)], (2, n1h, LANE)).reshape(2 * n1h, LANE)
            for b in range(2):
                ob = out[b * n1h:(b + 1) * n1h, :]
                y = jnp.concatenate([ob[:, 0:LANE], ob[:, LANE:]], axis=0)
                vv = vin[b, pl.ds(r0, 2 * n1h), :]
                gg = gate[b, pl.ds(r0, 2 * n1h), :]
                write(b, c0, r0, gg * (y + vv * bias))
            return carry

        lax.fori_loop(0, cb // 2, inv, 0)

    def write_mid(b, c0, r0, val):
        vbuf[b, pl.ds(r0, 2 * n1h), :] = val

    def write_out(b, c0, r0, val):
        o_ref[b, pl.ds(c0, 2)] = val.reshape(2, n1h, LANE)

    conv_order(0, zv, zx1, write_mid)
    conv_order(1, vbuf, zx2, write_out)


def _hyena_lat(p3t, cw, hb, kf, consts, *, cb, rb):
    batch, _, n1h, _ = p3t.shape
    n1f = 2 * n1h
    ncb = HY_C // cb
    r = cb * n1h
    const2 = lambda p, c: (0, 0)
    sec = lambda k: pl.BlockSpec((2, cb, n1h, LANE), lambda p, c, k=k: (p, c + k * ncb, 0, 0))
    return pl.pallas_call(
        functools.partial(_hy_lat_kernel, cb=cb, n1h=n1h, rb=rb),
        grid=(batch // 2, ncb),
        in_specs=[
            sec(0), sec(1), sec(2),
            pl.BlockSpec((3, 4, cb, 1, LANE), lambda p, c: (0, 0, c, 0, 0)),
            pl.BlockSpec((2, cb, 1, LANE), lambda p, c: (0, c, 0, 0)),
            pl.BlockSpec((2, cb * n1f, 2 * LANE), lambda p, c: (0, c, 0)),
            pl.BlockSpec((2 * n1f, n1f), const2),
            pl.BlockSpec((2, n1f, 2 * LANE), lambda p, c: (0, 0, 0)),
            pl.BlockSpec((2 * LANE, 2 * LANE), const2),
            pl.BlockSpec((2 * LANE, 2 * LANE), const2),
            pl.BlockSpec((n1f, 2 * n1f), const2),
        ],
        out_specs=pl.BlockSpec((2, cb, n1h, LANE), lambda p, c: (p, c, 0, 0)),
        out_shape=jax.ShapeDtypeStruct((batch, HY_C, n1h, LANE), F32),
        scratch_shapes=[
            pltpu.VMEM((2, r, LANE), F32), pltpu.VMEM((2, r, LANE), F32),
            pltpu.VMEM((2, r, LANE), F32), pltpu.VMEM((2, r, LANE), F32),
            pltpu.VMEM((cb * n1f, 2 * LANE), BF), pltpu.VMEM((cb * n1f, 2 * LANE), F32),
        ],
        compiler_params=_cparams(2),
        name="hyena_lat",
    )(p3t, p3t, p3t, cw, hb, kf, consts["m1"], consts["tw"], consts["g2"], consts["g2i"], consts["m1i"])


def _hy_ctx_kernel(x1_ref, x2_ref, v_ref, cw_ref, hb_ref, kt_ref, mf_ref, mff_ref, mi_ref, o_ref, *, lc):
    lanei = lax.broadcasted_iota(jnp.int32, (HY_C, lc), 1)

    def sconv(u, s):
        prev = jnp.where(lanei == 0, 0.0, pltpu.roll(u, 1, 1))
        nxt = jnp.where(lanei == lc - 1, 0.0, pltpu.roll(u, lc - 1, 1))
        return prev * cw_ref[s, 0] + u * cw_ref[s, 1] + nxt * cw_ref[s, 2] + cw_ref[s, 3]

    zx1 = [sconv(x1_ref[b], 0) for b in range(2)]
    zx2 = [sconv(x2_ref[b], 1) for b in range(2)]
    v = [sconv(v_ref[b], 2) for b in range(2)]

    kt = kt_ref[...]
    kl = lax.broadcasted_iota(jnp.int32, kt.shape, 1)
    kf = _dot(jnp.where(kl == lc, 0.0, kt).astype(BF), mff_ref[...])
    n = 2 * lc
    for o in range(2):
        z = jnp.concatenate([v[0], v[1]], axis=1).astype(BF)
        x = _dot(z, mf_ref[...])
        xr, xi = x[:, 0:n], x[:, n:]
        kr = kf[o * HY_C:(o + 1) * HY_C, 0:n]
        ki = kf[o * HY_C:(o + 1) * HY_C, n:]
        y = _dot(jnp.concatenate([xr * kr - xi * ki, xr * ki + xi * kr], axis=1).astype(BF), mi_ref[...])
        gate = zx1 if o == 0 else zx2
        v = [gate[b] * (y[:, b * lc:(b + 1) * lc] + v[b] * hb_ref[o]) for b in range(2)]
    for b in range(2):
        o_ref[b] = v[b]


def _hyena_ctx(p3t, cw, hb, kt, consts):
    batch, _, lc = p3t.shape
    n = 2 * lc
    sec = lambda k: pl.BlockSpec((2, HY_C, lc), lambda p, k=k: (p, k, 0))
    return pl.pallas_call(
        functools.partial(_hy_ctx_kernel, lc=lc),
        grid=(batch // 2,),
        in_specs=[
            sec(0), sec(1), sec(2),
            pl.BlockSpec((3, 4, HY_C, 1), lambda p: (0, 0, 0, 0)),
            pl.BlockSpec((2, HY_C, 1), lambda p: (0, 0, 0)),
            pl.BlockSpec((2 * HY_C, n), lambda p: (0, 0)),
            pl.BlockSpec((n, 2 * n), lambda p: (0, 0)),
            pl.BlockSpec((n, 2 * n), lambda p: (0, 0)),
            pl.BlockSpec((2 * n, n), lambda p: (0, 0)),
        ],
        out_specs=pl.BlockSpec((2, HY_C, lc), lambda p: (p, 0, 0)),
        out_shape=jax.ShapeDtypeStruct((batch, HY_C, lc), F32),
        compiler_params=_cparams(1),
        name="hyena_ctx",
    )(p3t, p3t, p3t, cw, hb, kt, consts["mf"], consts["mff"], consts["mi"])


def _mxu_const(a):
    return jnp.asarray(a, F32).astype(BF)


def _dft_consts_two_stage(n1f):
    n2 = DFT_N2
    n = n1f * n2
    n1h = n1f // 2
    k1 = np.arange(n1f)
    ang1 = 2.0 * np.pi * np.outer(k1, k1) / n1f
    f1r, f1i = np.cos(ang1), -np.sin(ang1)
    m1 = np.block([[f1r[:, :n1h], -f1i[:, :n1h]], [f1i[:, :n1h], f1r[:, :n1h]]])
    m1f = np.concatenate([f1r, f1i], axis=0)
    angt = 2.0 * np.pi * np.outer(k1, np.arange(n2)) / n
    tw = np.stack([np.tile(np.cos(angt), (1, 2)), np.tile(-np.sin(angt), (1, 2))])
    a2 = 2.0 * np.pi * np.outer(np.arange(n2), np.arange(n2)) / n2
    f2r, f2i = np.cos(a2), -np.sin(a2)
    g2 = np.block([[f2r, f2i], [-f2i, f2r]])
    g2i = np.block([[f2r, -f2i], [f2i, f2r]])
    m1i = np.block([[f1r[:n1h, :], f1i[:n1h, :]], [-f1i[:n1h, :], f1r[:n1h, :]]]) / n
    return {
        "m1": _mxu_const(m1), "m1f": _mxu_const(m1f), "tw": jnp.asarray(tw, F32),
        "g2": _mxu_const(g2), "g2i": _mxu_const(g2i), "m1i": _mxu_const(m1i),
    }


def _dft_consts_single(lc):
    n = 2 * lc
    ang = 2.0 * np.pi * np.outer(np.arange(n), np.arange(n)) / n
    fr, fi = np.cos(ang), -np.sin(ang)
    mf = np.block([[fr[:lc], fi[:lc]], [-fi[:lc], fr[:lc]]])
    mff = np.concatenate([fr, fi], axis=1)
    gr, gi = fr[:, :lc] / n, -fi[:, :lc] / n
    mi = np.block([[gr, gi], [-gi, gr]])
    return {"mf": _mxu_const(mf), "mff": _mxu_const(mff), "mi": _mxu_const(mi)}


def _filter_tables(seq):
    pos = jnp.arange(seq, dtype=F32)
    t = pos / max(seq - 1, 1)
    w = 2.0 * math.pi * pos / seq
    bands = jnp.linspace(1e-4, HY_BANDS - 1, HY_BANDS, dtype=F32)
    ang = w[:, None] * bands[None, :]
    z = jnp.concatenate([t[:, None], jnp.cos(ang), -jnp.sin(ang)], axis=-1)
    min_decay = abs(math.log(HY_DECAY_TARGET) / HY_SLOW_PCT)
    max_decay = abs(math.log(HY_DECAY_TARGET) / HY_FAST_PCT)
    deltas = jnp.linspace(min_decay, max_decay, HY_C, dtype=F32)
    window = jnp.exp(-t[:, None] * deltas[None, :]) + HY_WINDOW_SHIFT
    idx = np.concatenate([np.arange(seq), np.array([0]), np.arange(seq - 1, 0, -1)])
    et = jnp.pad(z[idx], ((0, 0), (0, HY_FFN - HY_EMB))).T
    win = jnp.tile(window[idx].T, (2, 1))
    return {"et": et, "win": win}


def _rope_tables(n_lat, n_ctx):
    n_rows = n_lat // GRID_W
    rows = jnp.repeat(jnp.arange(n_rows, dtype=F32), GRID_W)
    cols = jnp.tile(jnp.arange(GRID_W, dtype=F32), n_rows)
    quarter = MLA_ROPE // 4
    inv = 1.0 / (ROPE_BASE ** (jnp.arange(quarter, dtype=F32) / quarter))
    ang = jnp.stack([rows[:, None] * inv, cols[:, None] * inv], axis=1)
    r = np.arange(MLA_ROPE)
    axis, half, i = r // (2 * quarter), (r % (2 * quarter)) // quarter, r % quarter
    cos32 = jnp.cos(ang)[:, axis, i]
    ssin32 = jnp.sin(ang)[:, axis, i] * jnp.asarray(np.where(half == 0, -1.0, 1.0), F32)
    ones = jnp.ones((n_lat, MLA_NOPE), F32)
    zpad = jnp.zeros((n_lat, HEAD_PAD - MLA_NOPE - MLA_ROPE), F32)
    z64 = jnp.zeros((n_lat, MLA_NOPE), F32)
    lat = {
        "ca": MLA_SCALE * jnp.concatenate([ones, cos32, zpad], axis=1),
        "sb": MLA_SCALE * jnp.concatenate([z64, ssin32, zpad], axis=1),
        "ck": jnp.concatenate([cos32, ssin32, z64], axis=1),
    }
    one_c = jnp.ones((n_ctx, MLA_NOPE + MLA_ROPE), F32)
    ctx = {
        "ca": MLA_SCALE * jnp.concatenate([one_c, jnp.zeros((n_ctx, HEAD_PAD - MLA_NOPE - MLA_ROPE), F32)], axis=1),
        "sb": jnp.zeros((n_ctx, HEAD_PAD), F32),
        "ck": jnp.concatenate([jnp.ones((n_ctx, MLA_ROPE), F32), jnp.zeros((n_ctx, LANE - MLA_ROPE), F32)], axis=1),
    }
    return lat, ctx


def _rope_swap():
    quarter = MLA_ROPE // 4
    r = np.arange(MLA_ROPE)
    half = (r % (2 * quarter)) // quarter
    return np.where(half == 0, r + quarter, r - quarter)


def _mix_weights(l, w_in, mla_q_norm, mla_w_uq, mla_kv_norm, mla_w_ukv, sg_ln_g, sg_ln_b, sg_w, sg_b):
    sw = _rope_swap()
    wi = w_in[l]
    kr = wi[:, OFF_KR:N_IN]
    w_all = jnp.concatenate(
        [wi, kr[:, sw], jnp.zeros((D_MODEL, N_IN_PAD - N_IN - MLA_ROPE), F32)], axis=1).astype(BF)
    dq = MLA_NOPE + MLA_ROPE
    uq = mla_w_uq[l].reshape(MLA_Q_RANK, MLA_HEADS, dq)
    pad = HEAD_PAD - dq
    wqa = jnp.pad(uq, ((0, 0), (0, 0), (0, pad)))
    rope_sw = uq[:, :, MLA_NOPE:][:, :, sw]
    wqb = jnp.pad(rope_sw, ((0, 0), (0, 0), (MLA_NOPE, pad)))
    wq = jnp.concatenate([wqa.reshape(MLA_Q_RANK, -1), wqb.reshape(MLA_Q_RANK, -1)], axis=1).astype(BF)
    ukv = mla_w_ukv[l].reshape(MLA_KV_RANK, MLA_HEADS, MLA_NOPE + MLA_V)
    wk = jnp.pad(ukv[:, :, :MLA_NOPE], ((0, 0), (0, 0), (0, HEAD_PAD - MLA_NOPE))).reshape(MLA_KV_RANK, -1)
    wv = ukv[:, :, MLA_NOPE:].reshape(MLA_KV_RANK, -1)
    return {
        "w_all": w_all,
        "gq": mla_q_norm[l].reshape(1, -1), "wq": wq,
        "gkv": mla_kv_norm[l].reshape(1, -1), "wkv": jnp.concatenate([wk, wv], axis=1).astype(BF),
        "lng": sg_ln_g[l].reshape(1, -1), "lnb": sg_ln_b[l].reshape(1, -1),
        "wsg": sg_w[l].reshape(SG_GROUPS * SG_CHUNK, SG_CHUNK).astype(BF),
        "bsg": jnp.repeat(sg_b[l].T, SG_C // SG_GROUPS, axis=1),
    }


def _filter_weights(l, hy_w1, hy_b1, hy_w2, hy_b2, hy_w3, hy_freq):
    w1t = jnp.pad(hy_w1[l], ((0, HY_FFN - HY_EMB), (0, 0))).T.astype(BF)
    w3 = hy_w3[l].reshape(HY_FFN, 2, 2, HY_C)
    w3t = jnp.transpose(w3, (2, 1, 3, 0)).reshape(2, 2 * HY_C, HY_FFN).astype(BF)
    col = lambda a: a.reshape(HY_FFN, 1)
    return {"w1t": w1t, "b1": col(hy_b1[l]), "f0": col(hy_freq[l, 0]),
            "w2t": hy_w2[l].T.astype(BF), "b2": col(hy_b2[l]), "f1": col(hy_freq[l, 1]), "w3t": w3t}


def kernel(x, c, ctx, c_ctx, ada_w, ada_b, ffn1_wg, ffn1_wu, ffn1_wd, w_in, w_out, hy_conv_w, hy_conv_b, hy_w1, hy_b1, hy_w2, hy_b2, hy_w3, hy_freq, hy_bias, sg_ln_g, sg_ln_b, sg_w, sg_b, mla_q_norm, mla_w_uq, mla_kv_norm, mla_w_ukv, ffn2_wg, ffn2_wu, ffn2_wd, final_norm):
    batch, n_lat, d = x.shape
    n_ctx = ctx.shape[1]
    depth = ada_w.shape[0]
    assert d == D_MODEL and batch % 2 == 0 and batch <= 4
    assert n_lat % (2 * LANE) == 0 and n_ctx % SG_CHUNK == 0
    n1h = n_lat // LANE
    ctx_row = batch

    cc = jnp.concatenate([c, c_ctx[None, :], jnp.zeros((8 - batch - 1, d), F32)], axis=0)
    mods = _modulation(cc, ada_w, ada_b).reshape(depth, 8, N_MOD, d)

    rope_lat, rope_ctx = _rope_tables(n_lat, n_ctx)
    ftab_lat = _filter_tables(n_lat)
    ftab_ctx = _filter_tables(n_ctx)
    dft_lat = _dft_consts_two_stage(2 * n1h)
    dft_ctx = _dft_consts_single(n_ctx)

    xl = x.reshape(batch * n_lat, d)
    xc = ctx.reshape(batch * n_ctx, d)
    tm = 512
    for l in range(depth):
        last = l == depth - 1
        mod_l = mods[l]
        f1 = (ffn1_wg[l].astype(BF), ffn1_wu[l].astype(BF), ffn1_wd[l].astype(BF))
        f2 = (ffn2_wg[l].astype(BF), ffn2_wu[l].astype(BF), ffn2_wd[l].astype(BF))

        xl = _ffn(xl, mod_l, 0, *f1, tm=tm, seq=n_lat)
        xc = _ffn(xc, mod_l, 0, *f1, tm=tm, seq=n_ctx, cond_row=ctx_row)

        mw = _mix_weights(l, w_in, mla_q_norm, mla_w_uq, mla_kv_norm, mla_w_ukv, sg_ln_g, sg_ln_b, sg_w, sg_b)
        p3, ysg, q, k, v = _mix_in(xl, mod_l, None, n_lat, mw, rope_lat, tm=tm)
        p3c, ysgc, qc, kc, vc = _mix_in(xc, mod_l, ctx_row, n_ctx, mw, rope_ctx, tm=n_ctx)

        y_attn = _attention(q, kc, vc, k, v, batch=batch, nq_seq=n_lat, n_ctx=n_ctx, tq=256)

        fw = _filter_weights(l, hy_w1, hy_b1, hy_w2, hy_b2, hy_w3, hy_freq)
        cw = jnp.concatenate([hy_conv_w[l], hy_conv_b[l][None, :]], axis=0)
        cw = jnp.transpose(cw.reshape(4, 3, HY_C), (1, 0, 2))

        ktime = _filter_time(ftab_lat, fw, pb=1024)
        hb = hy_bias[l] + ktime[:, n_lat].reshape(2, HY_C)
        kf = _filter_spectrum(ktime.reshape(2 * HY_C, 2 * n1h, LANE), dft_lat, cb=32)
        p3t = jnp.transpose(p3.reshape(batch, n_lat, OFF_SG), (0, 2, 1)).reshape(batch, OFF_SG, n1h, LANE)
        y_hy = _hyena_lat(
            p3t,
            jnp.broadcast_to(cw[:, :, :, None, None], (3, 4, HY_C, 1, LANE)),
            jnp.broadcast_to(hb[:, :, None, None], (2, HY_C, 1, LANE)),
            kf.reshape(2, HY_C * 2 * n1h, 2 * LANE), dft_lat, cb=32, rb=512)
        y_hy = jnp.transpose(y_hy.reshape(batch, HY_C, n_lat), (0, 2, 1)).reshape(batch * n_lat, HY_C)

        wo = w_out[l].astype(BF)
        xl = _wout(xl, mod_l, None, n_lat, y_hy, ysg, y_attn, wo, tm=tm)

        if not last:
            ktime_c = _filter_time(ftab_ctx, fw, pb=n_ctx)
            hbc = hy_bias[l] + ktime_c[:, n_ctx].reshape(2, HY_C)
            p3ct = jnp.transpose(p3c.reshape(batch, n_ctx, OFF_SG), (0, 2, 1))
            yc_hy = _hyena_ctx(p3ct, cw[:, :, :, None], hbc[:, :, None], ktime_c, dft_ctx)
            yc_hy = jnp.transpose(yc_hy, (0, 2, 1)).reshape(batch * n_ctx, HY_C)
            yc_attn = _attention(qc, kc, vc, None, None, batch=batch, nq_seq=n_ctx, n_ctx=n_ctx, tq=n_ctx)
            xc = _wout(xc, mod_l, ctx_row, n_ctx, yc_hy, ysgc, yc_attn, wo, tm=n_ctx)

        xl = _ffn(xl, mod_l, 6, *f2, tm=tm, seq=n_lat, final_norm=final_norm if last else None)
        if not last:
            xc = _ffn(xc, mod_l, 6, *f2, tm=tm, seq=n_ctx, cond_row=ctx_row)
    return xl.reshape(batch, n_lat, d)
```

```python
import functools
import math

import numpy as np
import jax
import jax.numpy as jnp
from jax import lax
from jax.experimental import pallas as pl
from jax.experimental.pallas import tpu as pltpu

F32 = jnp.float32
BF = jnp.bfloat16

D_MODEL = 1024
D_FF = 2816
N_MOD = 9
EPS = 1e-6
GRID_W = 64

HY_C = 256
HY_BANDS = 16
HY_EMB = 1 + 2 * HY_BANDS
HY_FFN = 64
HY_DECAY_TARGET = 1e-2
HY_FAST_PCT = 0.3
HY_SLOW_PCT = 1.5
HY_WINDOW_SHIFT = 0.05

SG_C = 256
SG_GROUPS = 4
SG_CHUNK = 128

MLA_HEADS = 8
MLA_NOPE = 64
MLA_ROPE = 32
MLA_V = 64
MLA_Q_RANK = 384
MLA_KV_RANK = 256
MLA_SCALE = (MLA_NOPE + MLA_ROPE) ** -0.5
Q_SCALE = MLA_SCALE * math.log2(math.e)
ROPE_BASE = 10000.0

OFF_SG = 3 * HY_C
OFF_Q = OFF_SG + 2 * SG_C
OFF_KV = OFF_Q + MLA_Q_RANK
OFF_KR = OFF_KV + MLA_KV_RANK
N_IN = OFF_KR + MLA_ROPE
N_IN_PAD = 2048

LANE = 128
HEAD_PAD = 128
FFN_CHUNK = 256
VMEM_LIMIT = 56 * 1024 * 1024

DFT_N2 = LANE


def _cparams(n_axes, vmem=VMEM_LIMIT):
    return pltpu.CompilerParams(dimension_semantics=("arbitrary",) * n_axes, vmem_limit_bytes=vmem)


def _mod_spec(d, seq, tm, cond_row):
    if cond_row is None:
        tiles_per_seq = seq // tm
        return pl.BlockSpec((None, N_MOD, d), lambda i: (i // tiles_per_seq, 0, 0))
    return pl.BlockSpec((None, N_MOD, d), lambda i: (cond_row, 0, 0))


def _rms(x):
    return x * lax.rsqrt(jnp.mean(x * x, axis=-1, keepdims=True) + EPS)


def _dot(a, b):
    return jnp.dot(a, b, preferred_element_type=F32)


_NT = (((1,), (1,)), ((), ()))


def _mod_kernel(c_ref, w_ref, b_ref, o_ref):
    c = c_ref[...]
    s = c * jax.nn.sigmoid(c)
    o_ref[0] = _dot(s.astype(BF), w_ref[0].astype(BF)) + b_ref[0]


def _modulation(cc, ada_w, ada_b):
    depth, d, n = ada_w.shape
    tn = 1152
    return pl.pallas_call(
        _mod_kernel,
        grid=(depth, n // tn),
        in_specs=[
            pl.BlockSpec((8, d), lambda l, j: (0, 0)),
            pl.BlockSpec((1, d, tn), lambda l, j: (l, 0, j)),
            pl.BlockSpec((1, 1, tn), lambda l, j: (l, 0, j)),
        ],
        out_specs=pl.BlockSpec((1, 8, tn), lambda l, j: (l, 0, j)),
        out_shape=jax.ShapeDtypeStruct((depth, 8, n), F32),
        compiler_params=_cparams(2),
        name="adaln_mod",
    )(cc, ada_w, ada_b.reshape(depth, 1, n))


def _ffn_kernel(*refs, k0, final):
    if final:
        x_ref, mod_ref, wg_ref, wu_ref, wd_ref, fn_ref, o_ref, a_ref = refs
    else:
        x_ref, mod_ref, wg_ref, wu_ref, wd_ref, o_ref, a_ref = refs
    x = x_ref[...]
    shift = mod_ref[k0:k0 + 1, :]
    scale = mod_ref[k0 + 1:k0 + 2, :]
    gate = mod_ref[k0 + 2:k0 + 3, :]
    hb = (_rms(x) * (1.0 + scale) + shift).astype(BF)
    for j in range(D_FF // FFN_CHUNK):
        cols = slice(j * FFN_CHUNK, (j + 1) * FFN_CHUNK)
        g = _dot(hb, wg_ref[:, cols])
        u = _dot(hb, wu_ref[:, cols])
        a_ref[:, cols] = (g * jax.nn.sigmoid(g) * u).astype(BF)
    y = x + (0.5 * gate) * _dot(a_ref[...], wd_ref[...])
    if final:
        y = _rms(y) * fn_ref[...]
    o_ref[...] = y


def _ffn(x, mod_l, k0, wg, wu, wd, *, tm, seq, cond_row=None, final_norm=None):
    t, d = x.shape
    final = final_norm is not None
    const = lambda i: (0, 0)
    in_specs = [
        pl.BlockSpec((tm, d), lambda i: (i, 0)),
        _mod_spec(d, seq, tm, cond_row),
        pl.BlockSpec((d, D_FF), const, pipeline_mode=pl.Buffered(1)),
        pl.BlockSpec((d, D_FF), const, pipeline_mode=pl.Buffered(1)),
        pl.BlockSpec((D_FF, d), const, pipeline_mode=pl.Buffered(1)),
    ]
    args = [x, mod_l, wg, wu, wd]
    if final:
        in_specs.append(pl.BlockSpec((1, d), const))
        args.append(final_norm.reshape(1, d))
    return pl.pallas_call(
        functools.partial(_ffn_kernel, k0=k0, final=final),
        grid=(t // tm,),
        in_specs=in_specs,
        out_specs=pl.BlockSpec((tm, d), lambda i: (i, 0)),
        out_shape=jax.ShapeDtypeStruct((t, d), F32),
        scratch_shapes=[pltpu.VMEM((tm, D_FF), BF)],
        compiler_params=_cparams(1),
        name="ffn",
    )(*args)


def _gelu(x):
    return 0.5 * x * (1.0 + lax.erf(x * (1.0 / math.sqrt(2.0))))


def _mix_in_kernel(x_ref, mod_ref, wall_ref, ca_ref, sb_ref, ck_ref, gq_ref, wq_ref, gkv_ref, wkv_ref, wv_ref,
                   lng_ref, lnb_ref, wsg_ref, bsg_ref,
                   p3_ref, ysg_ref, q_ref, k_ref, v_ref, *, tm):
    x = x_ref[...]
    shift = mod_ref[3:4, :]
    scale = mod_ref[4:5, :]
    hb = (_rms(x) * (1.0 + scale) + shift).astype(BF)
    proj = _dot(hb, wall_ref[...])
    p3_ref[...] = proj[:, 0:OFF_SG]

    z = _gelu(proj[:, OFF_SG:OFF_Q])
    u = z[:, 0:SG_C]
    v = z[:, SG_C:2 * SG_C]
    mu = jnp.mean(v, axis=-1, keepdims=True)
    vc = v - mu
    var = jnp.mean(vc * vc, axis=-1, keepdims=True)
    vb = ((vc * lax.rsqrt(var + EPS)) * lng_ref[...] + lnb_ref[...]).astype(BF)
    lane_g = lax.broadcasted_iota(jnp.int32, (SG_CHUNK, SG_C), 1) // (SG_C // SG_GROUPS)
    for j in range(tm // SG_CHUNK):
        rows = slice(j * SG_CHUNK, (j + 1) * SG_CHUNK)
        o = _dot(wsg_ref[...], vb[rows, :])
        s = o[(SG_GROUPS - 1) * SG_CHUNK:, :]
        for g in range(SG_GROUPS - 2, -1, -1):
            s = jnp.where(lane_g == g, o[g * SG_CHUNK:(g + 1) * SG_CHUNK, :], s)
        ysg_ref[rows, :] = (u[rows, :] * (s + bsg_ref[...])).astype(BF)

    cqn = (_rms(proj[:, OFF_Q:OFF_KV]) * gq_ref[...]).astype(BF)
    qq = _dot(cqn, wq_ref[...])
    ca = ca_ref[...]
    sb = sb_ref[...]
    nq = MLA_HEADS * HEAD_PAD
    for h in range(MLA_HEADS):
        cols = slice(h * HEAD_PAD, (h + 1) * HEAD_PAD)
        cols_sw = slice(nq + h * HEAD_PAD, nq + (h + 1) * HEAD_PAD)
        q_ref[:, cols] = (qq[:, cols] * ca + qq[:, cols_sw] * sb).astype(BF)

    kvn = (_rms(proj[:, OFF_KV:OFF_KR]) * gkv_ref[...]).astype(BF)
    kv = _dot(kvn, wkv_ref[...])
    vlane = lax.broadcasted_iota(jnp.int32, (1, nq), 1)
    odd = ((vlane // HEAD_PAD) & 1) == 1
    one_lane = jnp.where(odd, 0, MLA_V)
    vone = jnp.where((vlane & (HEAD_PAD - 1)) == one_lane, 1.0, 0.0)
    v_ref[...] = (_dot(kvn, wv_ref[...]) + vone).astype(BF)
    t = proj[:, OFF_KR:OFF_KR + LANE] * ck_ref[...]
    krr = t + pltpu.roll(t, LANE - MLA_ROPE, 1)
    lane = lax.broadcasted_iota(jnp.int32, (tm, LANE), 1)
    in_rope = (lane >= MLA_NOPE) & (lane < MLA_NOPE + MLA_ROPE)
    krp = jnp.where(in_rope, pltpu.roll(krr, MLA_NOPE, 1), 0.0)
    for h in range(MLA_HEADS):
        cols = slice(h * HEAD_PAD, (h + 1) * HEAD_PAD)
        k_ref[:, cols] = (kv[:, cols] + krp).astype(BF)


def _mix_in(x, mod_l, cond_row, seq, w, tabs, *, tm):
    t, d = x.shape
    tiles_per_seq = seq // tm
    const = lambda i: (0, 0)
    tab = lambda i: (i % tiles_per_seq, 0)
    nq = MLA_HEADS * HEAD_PAD
    nv = MLA_HEADS * MLA_V
    row = lambda i: (i, 0)
    return pl.pallas_call(
        functools.partial(_mix_in_kernel, tm=tm),
        grid=(t // tm,),
        in_specs=[
            pl.BlockSpec((tm, d), row),
            _mod_spec(d, seq, tm, cond_row),
            pl.BlockSpec((d, N_IN_PAD), const),
            pl.BlockSpec((tm, LANE), tab),
            pl.BlockSpec((tm, LANE), tab),
            pl.BlockSpec((tm, LANE), tab),
            pl.BlockSpec((1, MLA_Q_RANK), const),
            pl.BlockSpec((MLA_Q_RANK, 2 * nq), const),
            pl.BlockSpec((1, MLA_KV_RANK), const),
            pl.BlockSpec((MLA_KV_RANK, nq), const),
            pl.BlockSpec((MLA_KV_RANK, nq), const),
            pl.BlockSpec((1, SG_C), const),
            pl.BlockSpec((1, SG_C), const),
            pl.BlockSpec((SG_GROUPS * SG_CHUNK, SG_CHUNK), const),
            pl.BlockSpec((SG_CHUNK, SG_C), const),
        ],
        out_specs=[
            pl.BlockSpec((tm, OFF_SG), row),
            pl.BlockSpec((tm, SG_C), row),
            pl.BlockSpec((tm, nq), row),
            pl.BlockSpec((tm, nq), row),
            pl.BlockSpec((tm, nq), row),
        ],
        out_shape=[
            jax.ShapeDtypeStruct((t, OFF_SG), F32),
            jax.ShapeDtypeStruct((t, SG_C), BF),
            jax.ShapeDtypeStruct((t, nq), BF),
            jax.ShapeDtypeStruct((t, nq), BF),
            jax.ShapeDtypeStruct((t, nq), BF),
        ],
        compiler_params=_cparams(1),
        name="mix_in",
    )(x, mod_l, w["w_all"], tabs["ca"], tabs["sb"], tabs["ck"], w["gq"], w["wq"], w["gkv"], w["wkv"], w["wv"],
      w["lng"], w["lnb"], w["wsg"], w["bsg"])


def _attn_kernel(*refs, n_seg):
    q_ref = refs[0]
    segs = [(refs[1 + 2 * s], refs[2 + 2 * s]) for s in range(n_seg)]
    o_ref, s_buf, p_buf = refs[1 + 2 * n_seg:]
    tq = q_ref.shape[0]
    offs = [0]
    for k_ref, _ in segs:
        offs.append(offs[-1] + k_ref.shape[0])
    lane = lax.broadcasted_iota(jnp.int32, (tq, HEAD_PAD), 1)

    def scores(h):
        cols = slice(h * HEAD_PAD, (h + 1) * HEAD_PAD)
        q = q_ref[:, cols]
        for s, (k_ref, _) in enumerate(segs):
            s_buf[h % 2, :, offs[s]:offs[s + 1]] = lax.dot_general(q, k_ref[:, cols], _NT,
                                                                   preferred_element_type=F32)

    def softmax(h):
        s = s_buf[h % 2]
        m = jnp.max(s, axis=-1, keepdims=True)
        p_buf[h % 2] = jnp.exp2(s - m).astype(BF)

    def values(h, prev):
        cols = slice(h * HEAD_PAD, (h + 1) * HEAD_PAD)
        acc = None
        for s, (_, v_ref) in enumerate(segs):
            part = _dot(p_buf[h % 2, :, offs[s]:offs[s + 1]], v_ref[:, cols])
            acc = part if acc is None else acc + part
        if h % 2 == 0:
            return acc / acc[:, MLA_V:MLA_V + 1]
        o = acc / acc[:, 0:1]
        pair = h // 2
        o_ref[:, pair * HEAD_PAD:(pair + 1) * HEAD_PAD] = jnp.where(lane < MLA_V, prev, o).astype(BF)
        return None

    prev = None
    for step in range(MLA_HEADS + 2):
        if step < MLA_HEADS:
            scores(step)
        if 1 <= step <= MLA_HEADS:
            softmax(step - 1)
        if step >= 2:
            prev = values(step - 2, prev)


def _attention(q, segs, *, batch, nq_seq, tq):
    t = q.shape[0]
    nqt = nq_seq // tq
    width = MLA_HEADS * HEAD_PAD
    in_specs = [pl.BlockSpec((tq, width), lambda b, i: (b * nqt + i, 0))]
    args = [q]
    nk_total = 0
    for k, v, nk in segs:
        in_specs += [pl.BlockSpec((nk, width), lambda b, i: (b, 0)), pl.BlockSpec((nk, width), lambda b, i: (b, 0))]
        args += [k, v]
        nk_total += nk
    return pl.pallas_call(
        functools.partial(_attn_kernel, n_seg=len(segs)),
        grid=(batch, nqt),
        in_specs=in_specs,
        out_specs=pl.BlockSpec((tq, MLA_HEADS * MLA_V), lambda b, i: (b * nqt + i, 0)),
        out_shape=jax.ShapeDtypeStruct((t, MLA_HEADS * MLA_V), BF),
        scratch_shapes=[pltpu.VMEM((2, tq, nk_total), F32), pltpu.VMEM((2, tq, nk_total), BF)],
        compiler_params=_cparams(2),
        name="attn_lat" if len(segs) > 1 else "attn_ctx",
    )(*args)


def _wout_kernel(x_ref, mod_ref, yh_ref, ys_ref, ya_ref, w_ref, o_ref):
    acc = _dot(yh_ref[...].astype(BF), w_ref[0:HY_C, :])
    acc = acc + _dot(ys_ref[...], w_ref[HY_C:HY_C + SG_C, :])
    acc = acc + _dot(ya_ref[...], w_ref[HY_C + SG_C:, :])
    o_ref[...] = x_ref[...] + mod_ref[5:6, :] * acc


def _wout(x, mod_l, cond_row, seq, yh, ys, ya, w, *, tm):
    t, d = x.shape
    row = lambda i: (i, 0)
    return pl.pallas_call(
        _wout_kernel,
        grid=(t // tm,),
        in_specs=[
            pl.BlockSpec((tm, d), row),
            _mod_spec(d, seq, tm, cond_row),
            pl.BlockSpec((tm, HY_C), row),
            pl.BlockSpec((tm, SG_C), row),
            pl.BlockSpec((tm, MLA_HEADS * MLA_V), row),
            pl.BlockSpec((HY_C + SG_C + MLA_HEADS * MLA_V, d), lambda i: (0, 0)),
        ],
        out_specs=pl.BlockSpec((tm, d), row),
        out_shape=jax.ShapeDtypeStruct((t, d), F32),
        compiler_params=_cparams(1),
        name="w_out",
    )(x, mod_l, yh, ys, ya, w)


def _filt_kernel(et_ref, w1_ref, b1_ref, f0_ref, w2_ref, b2_ref, f1_ref, w3_ref, win_ref, o_ref):
    e = et_ref[...].astype(BF)
    h = jnp.sin(f0_ref[...] * (_dot(w1_ref[...], e) + b1_ref[...]))
    h = jnp.sin(f1_ref[...] * (_dot(w2_ref[...], h.astype(BF)) + b2_ref[...]))
    o_ref[...] = _dot(w3_ref[0], h.astype(BF)) * win_ref[...]


def _filter_time(tabs, fw, *, pb):
    n2l = tabs["et"].shape[1]
    nb = n2l // pb
    const = lambda i: (0, 0)
    col = lambda i: (0, i)
    rows = 2 * HY_C
    return pl.pallas_call(
        _filt_kernel,
        grid=(nb,),
        in_specs=[
            pl.BlockSpec((HY_FFN, pb), col),
            pl.BlockSpec((HY_FFN, HY_FFN), const),
            pl.BlockSpec((HY_FFN, 1), const),
            pl.BlockSpec((HY_FFN, 1), const),
            pl.BlockSpec((HY_FFN, HY_FFN), const),
            pl.BlockSpec((HY_FFN, 1), const),
            pl.BlockSpec((HY_FFN, 1), const),
            pl.BlockSpec((1, rows, HY_FFN), lambda i: (i // (nb // 2), 0, 0)),
            pl.BlockSpec((rows, pb), col),
        ],
        out_specs=pl.BlockSpec((rows, pb), col),
        out_shape=jax.ShapeDtypeStruct((rows, n2l), F32),
        compiler_params=_cparams(1),
        name="hyena_filter",
    )(tabs["et"], fw["w1t"], fw["b1"], fw["f0"], fw["w2t"], fw["b2"], fw["f1"], fw["w3t"], tabs["win"])


def _dft_stage1(m1, rhs, tw_ref, lhs_ref, c0, n1f):
    r = _dot(m1, rhs)
    br = r[0:n1f, :]
    bi = r[n1f:2 * n1f, :]
    tr = tw_ref[0]
    ti = tw_ref[1]
    pr = (br * tr - bi * ti).astype(BF)
    pi = (br * ti + bi * tr).astype(BF)
    q0 = pl.multiple_of(c0 * n1f, 2 * n1f)
    lhs_ref[pl.ds(q0, n1f), 0:LANE] = pr[:, 0:LANE]
    lhs_ref[pl.ds(q0, n1f), LANE:2 * LANE] = pi[:, 0:LANE]
    lhs_ref[pl.ds(q0 + n1f, n1f), 0:LANE] = pr[:, LANE:2 * LANE]
    lhs_ref[pl.ds(q0 + n1f, n1f), LANE:2 * LANE] = pi[:, LANE:2 * LANE]


def _kf_kernel(k_ref, m1_ref, tw_ref, g2_ref, o_ref, lhs_ref, *, cb, n1f):
    zero_at = n1f // 2

    def body(j, carry):
        c0 = 2 * j
        ka = k_ref[c0]
        kb = k_ref[c0 + 1]
        rowi = lax.broadcasted_iota(jnp.int32, (n1f, 2 * LANE), 0)
        lanei = lax.broadcasted_iota(jnp.int32, (n1f, 2 * LANE), 1)
        pad_tap = (rowi == zero_at) & ((lanei == 0) | (lanei == LANE))
        rhs = jnp.where(pad_tap, 0.0, jnp.concatenate([ka, kb], axis=1)).astype(BF)
        _dft_stage1(m1_ref[...], rhs, tw_ref, lhs_ref, c0, n1f)
        return carry

    lax.fori_loop(0, cb // 2, body, 0)
    o_ref[...] = _dot(lhs_ref[...], g2_ref[...])


def _filter_spectrum(ktime3, consts, *, cb):
    rows, n1f, _ = ktime3.shape
    const2 = lambda i: (0, 0)
    return pl.pallas_call(
        functools.partial(_kf_kernel, cb=cb, n1f=n1f),
        grid=(rows // cb,),
        in_specs=[
            pl.BlockSpec((cb, n1f, LANE), lambda i: (i, 0, 0)),
            pl.BlockSpec((2 * n1f, n1f), const2),
            pl.BlockSpec((2, n1f, 2 * LANE), lambda i: (0, 0, 0)),
            pl.BlockSpec((2 * LANE, 2 * LANE), const2),
        ],
        out_specs=pl.BlockSpec((cb * n1f, 2 * LANE), lambda i: (i, 0)),
        out_shape=jax.ShapeDtypeStruct((rows * n1f, 2 * LANE), F32),
        scratch_shapes=[pltpu.VMEM((cb * n1f, 2 * LANE), BF)],
        compiler_params=_cparams(1),
        name="hyena_filter_spectrum",
    )(ktime3, consts["m1f"], consts["tw"], consts["g2"])


def _hy_lat_kernel(x1_ref, x2_ref, v_ref, cw_ref, hb_ref, kf_ref, m1_ref, tw_ref, g2_ref, g2i_ref, m1i_ref,
                   o_ref, zx1, zx2, zv, vbuf, lhs_ref, c_ref, *, cb, n1h, rb):
    n1f = 2 * n1h
    r = cb * n1h
    rowi = lax.broadcasted_iota(jnp.int32, (r, LANE), 0)
    lanei = lax.broadcasted_iota(jnp.int32, (r, LANE), 1)
    n1i = rowi & (n1h - 1)
    first = (n1i == 0) & (lanei == 0)
    last = (n1i == n1h - 1) & (lanei == LANE - 1)

    def bro(p):
        return jnp.broadcast_to(p, (cb, n1h, LANE)).reshape(r, LANE)

    def sconv(u3, s):
        u = u3.reshape(r, LANE)
        a = pltpu.roll(u, 1, 1)
        prev = jnp.where(lanei == 0, pltpu.roll(a, 1, 0), a)
        prev = jnp.where(first, 0.0, prev)
        f = pltpu.roll(u, LANE - 1, 1)
        nxt = jnp.where(lanei == LANE - 1, pltpu.roll(f, r - 1, 0), f)
        nxt = jnp.where(last, 0.0, nxt)
        return prev * bro(cw_ref[s, 0]) + u * bro(cw_ref[s, 1]) + nxt * bro(cw_ref[s, 2]) + bro(cw_ref[s, 3])

    for b in range(2):
        zx1[b] = sconv(x1_ref[b], 0)
        zx2[b] = sconv(x2_ref[b], 1)
        zv[b] = sconv(v_ref[b], 2)

    def conv_order(o, vin, gate, write):
        def fwd(j, carry):
            c0 = 2 * j
            r0 = pl.multiple_of(c0 * n1h, 2 * n1h)
            zr = vin[0, pl.ds(r0, 2 * n1h), :]
            zi = vin[1, pl.ds(r0, 2 * n1h), :]
            rhs = jnp.concatenate([
                jnp.concatenate([zr[0:n1h], zr[n1h:2 * n1h]], axis=1),
                jnp.concatenate([zi[0:n1h], zi[n1h:2 * n1h]], axis=1)], axis=0).astype(BF)
            _dft_stage1(m1_ref[...], rhs, tw_ref, lhs_ref, c0, n1f)
            return carry

        lax.fori_loop(0, cb // 2, fwd, 0)

        def mid(i, carry):
            q = pl.multiple_of(i * rb, rb)
            x = _dot(lhs_ref[pl.ds(q, rb), :], g2_ref[...])
            kf = kf_ref[o, pl.ds(q, rb), :]
            xr, xi = x[:, 0:LANE], x[:, LANE:]
            kr, ki = kf[:, 0:LANE], kf[:, LANE:]
            y = jnp.concatenate([xr * kr - xi * ki, xr * ki + xi * kr], axis=1).astype(BF)
            c_ref[pl.ds(q, rb), :] = _dot(y, g2i_ref[...])
            return carry

        lax.fori_loop(0, cb * n1f // rb, mid, 0)

        tr = tw_ref[0][:, 0:LANE]
        ti = tw_ref[1][:, 0:LANE]

        def inv(j, carry):
            c0 = 2 * j
            q0 = pl.multiple_of(c0 * n1f, 2 * n1f)
            ca = c_ref[pl.ds(q0, n1f), :]
            cc = c_ref[pl.ds(q0 + n1f, n1f), :]

            def itw(t):
                cr, ci = t[:, 0:LANE], t[:, LANE:]
                return cr * tr + ci * ti, ci * tr - cr * ti

            ar, ai = itw(ca)
            br, bi = itw(cc)
            rhs = jnp.concatenate([jnp.concatenate([ar, br], axis=1),
                                   jnp.concatenate([ai, bi], axis=1)], axis=0).astype(BF)
            out = _dot(m1i_ref[...], rhs)
            r0 = pl.multiple_of(c0 * n1h, 2 * n1h)
            bias = jnp.broadcast_to(hb_ref[o, pl.ds(c0, 2)], (2, n1h, LANE)).reshape(2 * n1h, LANE)
            for b in range(2):
                ob = out[b * n1h:(b + 1) * n1h, :]
                y = jnp.concatenate([ob[:, 0:LANE], ob[:, LANE:]], axis=0)
                vv = vin[b, pl.ds(r0, 2 * n1h), :]
                gg = gate[b, pl.ds(r0, 2 * n1h), :]
                write(b, c0, r0, gg * (y + vv * bias))
            return carry

        lax.fori_loop(0, cb // 2, inv, 0)

    def write_mid(b, c0, r0, val):
        vbuf[b, pl.ds(r0, 2 * n1h), :] = val

    def write_out(b, c0, r0, val):
        o_ref[b, pl.ds(c0, 2)] = val.reshape(2, n1h, LANE)

    conv_order(0, zv, zx1, write_mid)
    conv_order(1, vbuf, zx2, write_out)


def _hyena_lat(p3t, cw, hb, kf, consts, *, cb, rb):
    batch, _, n1h, _ = p3t.shape
    n1f = 2 * n1h
    ncb = HY_C // cb
    r = cb * n1h
    const2 = lambda p, c: (0, 0)
    sec = lambda k: pl.BlockSpec((2, cb, n1h, LANE), lambda p, c, k=k: (p, c + k * ncb, 0, 0))
    return pl.pallas_call(
        functools.partial(_hy_lat_kernel, cb=cb, n1h=n1h, rb=rb),
        grid=(batch // 2, ncb),
        in_specs=[
            sec(0), sec(1), sec(2),
            pl.BlockSpec((3, 4, cb, 1, LANE), lambda p, c: (0, 0, c, 0, 0)),
            pl.BlockSpec((2, cb, 1, LANE), lambda p, c: (0, c, 0, 0)),
            pl.BlockSpec((2, cb * n1f, 2 * LANE), lambda p, c: (0, c, 0)),
            pl.BlockSpec((2 * n1f, n1f), const2),
            pl.BlockSpec((2, n1f, 2 * LANE), lambda p, c: (0, 0, 0)),
            pl.BlockSpec((2 * LANE, 2 * LANE), const2),
            pl.BlockSpec((2 * LANE, 2 * LANE), const2),
            pl.BlockSpec((n1f, 2 * n1f), const2),
        ],
        out_specs=pl.BlockSpec((2, cb, n1h, LANE), lambda p, c: (p, c, 0, 0)),
        out_shape=jax.ShapeDtypeStruct((batch, HY_C, n1h, LANE), F32),
        scratch_shapes=[
            pltpu.VMEM((2, r, LANE), F32), pltpu.VMEM((2, r, LANE), F32),
            pltpu.VMEM((2, r, LANE), F32), pltpu.VMEM((2, r, LANE), F32),
            pltpu.VMEM((cb * n1f, 2 * LANE), BF), pltpu.VMEM((cb * n1f, 2 * LANE), F32),
        ],
        compiler_params=_cparams(2),
        name="hyena_lat",
    )(p3t, p3t, p3t, cw, hb, kf, consts["m1"], consts["tw"], consts["g2"], consts["g2i"], consts["m1i"])


def _hy_ctx_kernel(x1_ref, x2_ref, v_ref, cw_ref, hb_ref, kt_ref, mf_ref, mff_ref, mi_ref, o_ref, *, lc):
    lanei = lax.broadcasted_iota(jnp.int32, (HY_C, lc), 1)

    def sconv(u, s):
        prev = jnp.where(lanei == 0, 0.0, pltpu.roll(u, 1, 1))
        nxt = jnp.where(lanei == lc - 1, 0.0, pltpu.roll(u, lc - 1, 1))
        return prev * cw_ref[s, 0] + u * cw_ref[s, 1] + nxt * cw_ref[s, 2] + cw_ref[s, 3]

    zx1 = [sconv(x1_ref[b], 0) for b in range(2)]
    zx2 = [sconv(x2_ref[b], 1) for b in range(2)]
    v = [sconv(v_ref[b], 2) for b in range(2)]

    kt = kt_ref[...]
    kl = lax.broadcasted_iota(jnp.int32, kt.shape, 1)
    kf = _dot(jnp.where(kl == lc, 0.0, kt).astype(BF), mff_ref[...])
    n = 2 * lc
    for o in range(2):
        z = jnp.concatenate([v[0], v[1]], axis=1).astype(BF)
        x = _dot(z, mf_ref[...])
        xr, xi = x[:, 0:n], x[:, n:]
        kr = kf[o * HY_C:(o + 1) * HY_C, 0:n]
        ki = kf[o * HY_C:(o + 1) * HY_C, n:]
        y = _dot(jnp.concatenate([xr * kr - xi * ki, xr * ki + xi * kr], axis=1).astype(BF), mi_ref[...])
        gate = zx1 if o == 0 else zx2
        v = [gate[b] * (y[:, b * lc:(b + 1) * lc] + v[b] * hb_ref[o]) for b in range(2)]
    for b in range(2):
        o_ref[b] = v[b]


def _hyena_ctx(p3t, cw, hb, kt, consts):
    batch, _, lc = p3t.shape
    n = 2 * lc
    sec = lambda k: pl.BlockSpec((2, HY_C, lc), lambda p, k=k: (p, k, 0))
    return pl.pallas_call(
        functools.partial(_hy_ctx_kernel, lc=lc),
        grid=(batch // 2,),
        in_specs=[
            sec(0), sec(1), sec(2),
            pl.BlockSpec((3, 4, HY_C, 1), lambda p: (0, 0, 0, 0)),
            pl.BlockSpec((2, HY_C, 1), lambda p: (0, 0, 0)),
            pl.BlockSpec((2 * HY_C, n), lambda p: (0, 0)),
            pl.BlockSpec((n, 2 * n), lambda p: (0, 0)),
            pl.BlockSpec((n, 2 * n), lambda p: (0, 0)),
            pl.BlockSpec((2 * n, n), lambda p: (0, 0)),
        ],
        out_specs=pl.BlockSpec((2, HY_C, lc), lambda p: (p, 0, 0)),
        out_shape=jax.ShapeDtypeStruct((batch, HY_C, lc), F32),
        compiler_params=_cparams(1),
        name="hyena_ctx",
    )(p3t, p3t, p3t, cw, hb, kt, consts["mf"], consts["mff"], consts["mi"])


def _mxu_const(a):
    return jnp.asarray(a, F32).astype(BF)


def _dft_consts_two_stage(n1f):
    n2 = DFT_N2
    n = n1f * n2
    n1h = n1f // 2
    k1 = np.arange(n1f)
    ang1 = 2.0 * np.pi * np.outer(k1, k1) / n1f
    f1r, f1i = np.cos(ang1), -np.sin(ang1)
    m1 = np.block([[f1r[:, :n1h], -f1i[:, :n1h]], [f1i[:, :n1h], f1r[:, :n1h]]])
    m1f = np.concatenate([f1r, f1i], axis=0)
    angt = 2.0 * np.pi * np.outer(k1, np.arange(n2)) / n
    tw = np.stack([np.tile(np.cos(angt), (1, 2)), np.tile(-np.sin(angt), (1, 2))])
    a2 = 2.0 * np.pi * np.outer(np.arange(n2), np.arange(n2)) / n2
    f2r, f2i = np.cos(a2), -np.sin(a2)
    g2 = np.block([[f2r, f2i], [-f2i, f2r]])
    g2i = np.block([[f2r, -f2i], [f2i, f2r]])
    m1i = np.block([[f1r[:n1h, :], f1i[:n1h, :]], [-f1i[:n1h, :], f1r[:n1h, :]]]) / n
    return {
        "m1": _mxu_const(m1), "m1f": _mxu_const(m1f), "tw": jnp.asarray(tw, F32),
        "g2": _mxu_const(g2), "g2i": _mxu_const(g2i), "m1i": _mxu_const(m1i),
    }


def _dft_consts_single(lc):
    n = 2 * lc
    ang = 2.0 * np.pi * np.outer(np.arange(n), np.arange(n)) / n
    fr, fi = np.cos(ang), -np.sin(ang)
    mf = np.block([[fr[:lc], fi[:lc]], [-fi[:lc], fr[:lc]]])
    mff = np.concatenate([fr, fi], axis=1)
    gr, gi = fr[:, :lc] / n, -fi[:, :lc] / n
    mi = np.block([[gr, gi], [-gi, gr]])
    return {"mf": _mxu_const(mf), "mff": _mxu_const(mff), "mi": _mxu_const(mi)}


def _filter_tables(seq):
    pos = jnp.arange(seq, dtype=F32)
    t = pos / max(seq - 1, 1)
    w = 2.0 * math.pi * pos / seq
    bands = jnp.linspace(1e-4, HY_BANDS - 1, HY_BANDS, dtype=F32)
    ang = w[:, None] * bands[None, :]
    z = jnp.concatenate([t[:, None], jnp.cos(ang), -jnp.sin(ang)], axis=-1)
    min_decay = abs(math.log(HY_DECAY_TARGET) / HY_SLOW_PCT)
    max_decay = abs(math.log(HY_DECAY_TARGET) / HY_FAST_PCT)
    deltas = jnp.linspace(min_decay, max_decay, HY_C, dtype=F32)
    window = jnp.exp(-t[:, None] * deltas[None, :]) + HY_WINDOW_SHIFT
    idx = np.concatenate([np.arange(seq), np.array([0]), np.arange(seq - 1, 0, -1)])
    et = jnp.pad(z[idx], ((0, 0), (0, HY_FFN - HY_EMB))).T
    win = jnp.tile(window[idx].T, (2, 1))
    return {"et": et, "win": win}


def _rope_tables(n_lat, n_ctx):
    n_rows = n_lat // GRID_W
    rows = jnp.repeat(jnp.arange(n_rows, dtype=F32), GRID_W)
    cols = jnp.tile(jnp.arange(GRID_W, dtype=F32), n_rows)
    quarter = MLA_ROPE // 4
    inv = 1.0 / (ROPE_BASE ** (jnp.arange(quarter, dtype=F32) / quarter))
    ang = jnp.stack([rows[:, None] * inv, cols[:, None] * inv], axis=1)
    r = np.arange(MLA_ROPE)
    axis, half, i = r // (2 * quarter), (r % (2 * quarter)) // quarter, r % quarter
    cos32 = jnp.cos(ang)[:, axis, i]
    ssin32 = jnp.sin(ang)[:, axis, i] * jnp.asarray(np.where(half == 0, -1.0, 1.0), F32)
    ones = jnp.ones((n_lat, MLA_NOPE), F32)
    zpad = jnp.zeros((n_lat, HEAD_PAD - MLA_NOPE - MLA_ROPE), F32)
    z64 = jnp.zeros((n_lat, MLA_NOPE), F32)
    lat = {
        "ca": Q_SCALE * jnp.concatenate([ones, cos32, zpad], axis=1),
        "sb": Q_SCALE * jnp.concatenate([z64, ssin32, zpad], axis=1),
        "ck": jnp.concatenate([cos32, ssin32, z64], axis=1),
    }
    one_c = jnp.ones((n_ctx, MLA_NOPE + MLA_ROPE), F32)
    ctx = {
        "ca": Q_SCALE * jnp.concatenate([one_c, jnp.zeros((n_ctx, HEAD_PAD - MLA_NOPE - MLA_ROPE), F32)], axis=1),
        "sb": jnp.zeros((n_ctx, HEAD_PAD), F32),
        "ck": jnp.concatenate([jnp.ones((n_ctx, MLA_ROPE), F32), jnp.zeros((n_ctx, LANE - MLA_ROPE), F32)], axis=1),
    }
    return lat, ctx


def _rope_swap():
    quarter = MLA_ROPE // 4
    r = np.arange(MLA_ROPE)
    half = (r % (2 * quarter)) // quarter
    return np.where(half == 0, r + quarter, r - quarter)


def _mix_weights(l, w_in, mla_q_norm, mla_w_uq, mla_kv_norm, mla_w_ukv, sg_ln_g, sg_ln_b, sg_w, sg_b):
    sw = _rope_swap()
    wi = w_in[l]
    kr = wi[:, OFF_KR:N_IN]
    w_all = jnp.concatenate(
        [wi, kr[:, sw], jnp.zeros((D_MODEL, N_IN_PAD - N_IN - MLA_ROPE), F32)], axis=1).astype(BF)
    dq = MLA_NOPE + MLA_ROPE
    uq = mla_w_uq[l].reshape(MLA_Q_RANK, MLA_HEADS, dq)
    pad = HEAD_PAD - dq
    wqa = jnp.pad(uq, ((0, 0), (0, 0), (0, pad)))
    rope_sw = uq[:, :, MLA_NOPE:][:, :, sw]
    wqb = jnp.pad(rope_sw, ((0, 0), (0, 0), (MLA_NOPE, pad)))
    wq = jnp.concatenate([wqa.reshape(MLA_Q_RANK, -1), wqb.reshape(MLA_Q_RANK, -1)], axis=1).astype(BF)
    ukv = mla_w_ukv[l].reshape(MLA_KV_RANK, MLA_HEADS, MLA_NOPE + MLA_V)
    wk = jnp.pad(ukv[:, :, :MLA_NOPE], ((0, 0), (0, 0), (0, HEAD_PAD - MLA_NOPE))).reshape(MLA_KV_RANK, -1)
    vcols = ukv[:, :, MLA_NOPE:]
    v_lo = jnp.pad(vcols, ((0, 0), (0, 0), (0, HEAD_PAD - MLA_V)))
    v_hi = jnp.pad(vcols, ((0, 0), (0, 0), (HEAD_PAD - MLA_V, 0)))
    odd_head = (np.arange(MLA_HEADS) % 2 == 1)[None, :, None]
    wv = jnp.where(odd_head, v_hi, v_lo).reshape(MLA_KV_RANK, -1)
    return {
        "w_all": w_all,
        "gq": mla_q_norm[l].reshape(1, -1), "wq": wq,
        "gkv": mla_kv_norm[l].reshape(1, -1), "wkv": wk.astype(BF), "wv": wv.astype(BF),
        "lng": sg_ln_g[l].reshape(1, -1), "lnb": sg_ln_b[l].reshape(1, -1),
        "wsg": sg_w[l].reshape(SG_GROUPS * SG_CHUNK, SG_CHUNK).astype(BF),
        "bsg": jnp.repeat(sg_b[l].T, SG_C // SG_GROUPS, axis=1),
    }


def _filter_weights(l, hy_w1, hy_b1, hy_w2, hy_b2, hy_w3, hy_freq):
    w1t = jnp.pad(hy_w1[l], ((0, HY_FFN - HY_EMB), (0, 0))).T.astype(BF)
    w3 = hy_w3[l].reshape(HY_FFN, 2, 2, HY_C)
    w3t = jnp.transpose(w3, (2, 1, 3, 0)).reshape(2, 2 * HY_C, HY_FFN).astype(BF)
    col = lambda a: a.reshape(HY_FFN, 1)
    return {"w1t": w1t, "b1": col(hy_b1[l]), "f0": col(hy_freq[l, 0]),
            "w2t": hy_w2[l].T.astype(BF), "b2": col(hy_b2[l]), "f1": col(hy_freq[l, 1]), "w3t": w3t}


def kernel(x, c, ctx, c_ctx, ada_w, ada_b, ffn1_wg, ffn1_wu, ffn1_wd, w_in, w_out, hy_conv_w, hy_conv_b, hy_w1, hy_b1, hy_w2, hy_b2, hy_w3, hy_freq, hy_bias, sg_ln_g, sg_ln_b, sg_w, sg_b, mla_q_norm, mla_w_uq, mla_kv_norm, mla_w_ukv, ffn2_wg, ffn2_wu, ffn2_wd, final_norm):
    batch, n_lat, d = x.shape
    n_ctx = ctx.shape[1]
    depth = ada_w.shape[0]
    assert d == D_MODEL and batch % 2 == 0 and batch <= 4
    assert n_lat % (2 * LANE) == 0 and n_ctx % SG_CHUNK == 0
    n1h = n_lat // LANE
    ctx_row = batch

    cc = jnp.concatenate([c, c_ctx[None, :], jnp.zeros((8 - batch - 1, d), F32)], axis=0)
    mods = _modulation(cc, ada_w, ada_b).reshape(depth, 8, N_MOD, d)

    rope_lat, rope_ctx = _rope_tables(n_lat, n_ctx)
    ftab_lat = _filter_tables(n_lat)
    ftab_ctx = _filter_tables(n_ctx)
    dft_lat = _dft_consts_two_stage(2 * n1h)
    dft_ctx = _dft_consts_single(n_ctx)

    xl = x.reshape(batch * n_lat, d)
    xc = ctx.reshape(batch * n_ctx, d)
    tm = 512
    for l in range(depth):
        last = l == depth - 1
        mod_l = mods[l]
        f1 = (ffn1_wg[l].astype(BF), ffn1_wu[l].astype(BF), ffn1_wd[l].astype(BF))
        f2 = (ffn2_wg[l].astype(BF), ffn2_wu[l].astype(BF), ffn2_wd[l].astype(BF))

        xl = _ffn(xl, mod_l, 0, *f1, tm=tm, seq=n_lat)
        xc = _ffn(xc, mod_l, 0, *f1, tm=tm, seq=n_ctx, cond_row=ctx_row)

        mw = _mix_weights(l, w_in, mla_q_norm, mla_w_uq, mla_kv_norm, mla_w_ukv, sg_ln_g, sg_ln_b, sg_w, sg_b)
        p3, ysg, q, k, v = _mix_in(xl, mod_l, None, n_lat, mw, rope_lat, tm=tm)
        p3c, ysgc, qc, kc, vc = _mix_in(xc, mod_l, ctx_row, n_ctx, mw, rope_ctx, tm=n_ctx)

        y_attn = _attention(q, [(kc, vc, n_ctx), (k, v, n_lat)], batch=batch, nq_seq=n_lat, tq=256)

        fw = _filter_weights(l, hy_w1, hy_b1, hy_w2, hy_b2, hy_w3, hy_freq)
        cw = jnp.concatenate([hy_conv_w[l], hy_conv_b[l][None, :]], axis=0)
        cw = jnp.transpose(cw.reshape(4, 3, HY_C), (1, 0, 2))

        ktime = _filter_time(ftab_lat, fw, pb=1024)
        hb = hy_bias[l] + ktime[:, n_lat].reshape(2, HY_C)
        kf = _filter_spectrum(ktime.reshape(2 * HY_C, 2 * n1h, LANE), dft_lat, cb=32)
        p3t = jnp.transpose(p3.reshape(batch, n_lat, OFF_SG), (0, 2, 1)).reshape(batch, OFF_SG, n1h, LANE)
        y_hy = _hyena_lat(
            p3t,
            jnp.broadcast_to(cw[:, :, :, None, None], (3, 4, HY_C, 1, LANE)),
            jnp.broadcast_to(hb[:, :, None, None], (2, HY_C, 1, LANE)),
            kf.reshape(2, HY_C * 2 * n1h, 2 * LANE), dft_lat, cb=32, rb=512)
        y_hy = jnp.transpose(y_hy.reshape(batch, HY_C, n_lat), (0, 2, 1)).reshape(batch * n_lat, HY_C)

        wo = w_out[l].astype(BF)
        xl = _wout(xl, mod_l, None, n_lat, y_hy, ysg, y_attn, wo, tm=tm)

        if not last:
            ktime_c = _filter_time(ftab_ctx, fw, pb=n_ctx)
            hbc = hy_bias[l] + ktime_c[:, n_ctx].reshape(2, HY_C)
            p3ct = jnp.transpose(p3c.reshape(batch, n_ctx, OFF_SG), (0, 2, 1))
            yc_hy = _hyena_ctx(p3ct, cw[:, :, :, None], hbc[:, :, None], ktime_c, dft_ctx)
            yc_hy = jnp.transpose(yc_hy, (0, 2, 1)).reshape(batch * n_ctx, HY_C)
            yc_attn = _attention(qc, [(kc, vc, n_ctx)], batch=batch, nq_seq=n_ctx, tq=n_ctx)
            xc = _wout(xc, mod_l, ctx_row, n_ctx, yc_hy, ysgc, yc_attn, wo, tm=n_ctx)

        xl = _ffn(xl, mod_l, 6, *f2, tm=tm, seq=n_lat, final_norm=final_norm if last else None)
        if not last:
            xc = _ffn(xc, mod_l, 6, *f2, tm=tm, seq=n_ctx, cond_row=ctx_row)
    return xl.reshape(batch, n_lat, d)
```

```python
import functools
import math

import numpy as np
import jax
import jax.numpy as jnp
from jax import lax
from jax.experimental import pallas as pl
from jax.experimental.pallas import tpu as pltpu

F32 = jnp.float32
BF = jnp.bfloat16

D_MODEL = 1024
D_FF = 2816
N_MOD = 9
EPS = 1e-6
GRID_W = 64

HY_C = 256
HY_BANDS = 16
HY_EMB = 1 + 2 * HY_BANDS
HY_FFN = 64
HY_DECAY_TARGET = 1e-2
HY_FAST_PCT = 0.3
HY_SLOW_PCT = 1.5
HY_WINDOW_SHIFT = 0.05

SG_C = 256
SG_GROUPS = 4
SG_CHUNK = 128

MLA_HEADS = 8
MLA_NOPE = 64
MLA_ROPE = 32
MLA_V = 64
MLA_Q_RANK = 384
MLA_KV_RANK = 256
MLA_SCALE = (MLA_NOPE + MLA_ROPE) ** -0.5
Q_SCALE = MLA_SCALE * math.log2(math.e)
ROPE_BASE = 10000.0

OFF_SG = 3 * HY_C
OFF_Q = OFF_SG + 2 * SG_C
OFF_KV = OFF_Q + MLA_Q_RANK
OFF_KR = OFF_KV + MLA_KV_RANK
N_IN = OFF_KR + MLA_ROPE
N_IN_PAD = 2048

LANE = 128
HEAD_PAD = 128
V_ROWS = 80
FFN_CHUNK = 256
VMEM_LIMIT = 56 * 1024 * 1024

DFT_N2 = LANE


def _cparams(n_axes, vmem=VMEM_LIMIT):
    return pltpu.CompilerParams(dimension_semantics=("arbitrary",) * n_axes, vmem_limit_bytes=vmem)


def _mod_spec(d, seq, tm, cond_row):
    if cond_row is None:
        tiles_per_seq = seq // tm
        return pl.BlockSpec((None, N_MOD, d), lambda i: (i // tiles_per_seq, 0, 0))
    return pl.BlockSpec((None, N_MOD, d), lambda i: (cond_row, 0, 0))


def _rms(x):
    return x * lax.rsqrt(jnp.mean(x * x, axis=-1, keepdims=True) + EPS)


def _dot(a, b):
    return jnp.dot(a, b, preferred_element_type=F32)


_NT = (((1,), (1,)), ((), ()))


def _mod_kernel(c_ref, w_ref, b_ref, o_ref):
    c = c_ref[...]
    s = c * jax.nn.sigmoid(c)
    o_ref[0] = _dot(s.astype(BF), w_ref[0].astype(BF)) + b_ref[0]


def _modulation(cc, ada_w, ada_b):
    depth, d, n = ada_w.shape
    tn = 1152
    return pl.pallas_call(
        _mod_kernel,
        grid=(depth, n // tn),
        in_specs=[
            pl.BlockSpec((8, d), lambda l, j: (0, 0)),
            pl.BlockSpec((1, d, tn), lambda l, j: (l, 0, j)),
            pl.BlockSpec((1, 1, tn), lambda l, j: (l, 0, j)),
        ],
        out_specs=pl.BlockSpec((1, 8, tn), lambda l, j: (l, 0, j)),
        out_shape=jax.ShapeDtypeStruct((depth, 8, n), F32),
        compiler_params=_cparams(2),
        name="adaln_mod",
    )(cc, ada_w, ada_b.reshape(depth, 1, n))


def _ffn_kernel(*refs, k0, mixer_out, final):
    refs = list(refs)
    x_ref, mod_ref = refs[:2]
    del refs[:2]
    if mixer_out:
        yh_ref, ys_ref, ya_ref, wo_ref = refs[:4]
        del refs[:4]
    wg_ref, wu_ref, wd_ref = refs[:3]
    del refs[:3]
    if final:
        fn_ref = refs.pop(0)
    o_ref, a_ref = refs
    x = x_ref[...]
    if mixer_out:
        acc = _dot(yh_ref[...].astype(BF), wo_ref[0:HY_C, :])
        acc = acc + _dot(ys_ref[...], wo_ref[HY_C:HY_C + SG_C, :])
        acc = acc + _dot(ya_ref[...], wo_ref[HY_C + SG_C:, :])
        x = x + mod_ref[5:6, :] * acc
    shift = mod_ref[k0:k0 + 1, :]
    scale = mod_ref[k0 + 1:k0 + 2, :]
    gate = mod_ref[k0 + 2:k0 + 3, :]
    hb = (_rms(x) * (1.0 + scale) + shift).astype(BF)
    for j in range(D_FF // FFN_CHUNK):
        cols = slice(j * FFN_CHUNK, (j + 1) * FFN_CHUNK)
        g = _dot(hb, wg_ref[:, cols])
        u = _dot(hb, wu_ref[:, cols])
        a_ref[:, cols] = (g * jax.nn.sigmoid(g) * u).astype(BF)
    y = x + (0.5 * gate) * _dot(a_ref[...], wd_ref[...])
    if final:
        y = _rms(y) * fn_ref[...]
    o_ref[...] = y


def _ffn(x, mod_l, k0, wg, wu, wd, *, tm, seq, cond_row=None, mixer=None, final_norm=None):
    t, d = x.shape
    final = final_norm is not None
    const = lambda i: (0, 0)
    row = lambda i: (i, 0)
    in_specs = [pl.BlockSpec((tm, d), row), _mod_spec(d, seq, tm, cond_row)]
    args = [x, mod_l]
    if mixer is not None:
        in_specs += [
            pl.BlockSpec((tm, HY_C), row),
            pl.BlockSpec((tm, SG_C), row),
            pl.BlockSpec((tm, MLA_HEADS * MLA_V), row),
            pl.BlockSpec((HY_C + SG_C + MLA_HEADS * MLA_V, d), const, pipeline_mode=pl.Buffered(1)),
        ]
        args += list(mixer)
    in_specs += [
        pl.BlockSpec((d, D_FF), const, pipeline_mode=pl.Buffered(1)),
        pl.BlockSpec((d, D_FF), const, pipeline_mode=pl.Buffered(1)),
        pl.BlockSpec((D_FF, d), const, pipeline_mode=pl.Buffered(1)),
    ]
    args += [wg, wu, wd]
    if final:
        in_specs.append(pl.BlockSpec((1, d), const))
        args.append(final_norm.reshape(1, d))
    return pl.pallas_call(
        functools.partial(_ffn_kernel, k0=k0, mixer_out=mixer is not None, final=final),
        grid=(t // tm,),
        in_specs=in_specs,
        out_specs=pl.BlockSpec((tm, d), lambda i: (i, 0)),
        out_shape=jax.ShapeDtypeStruct((t, d), F32),
        scratch_shapes=[pltpu.VMEM((tm, D_FF), BF)],
        compiler_params=_cparams(1),
        name="ffn",
    )(*args)


def _gelu(x):
    return 0.5 * x * (1.0 + lax.erf(x * (1.0 / math.sqrt(2.0))))


def _mix_in_kernel(x_ref, mod_ref, wall_ref, ca_ref, sb_ref, ck_ref, gq_ref, wq_ref, gkv_ref, wk_ref, wvt_ref,
                   vone_ref, lng_ref, lnb_ref, wsg_ref, bsg_ref,
                   p3_ref, ysg_ref, q_ref, k_ref, vt_ref, *, tm):
    x = x_ref[...]
    shift = mod_ref[3:4, :]
    scale = mod_ref[4:5, :]
    hb = (_rms(x) * (1.0 + scale) + shift).astype(BF)
    proj = _dot(hb, wall_ref[...])
    p3_ref[...] = proj[:, 0:OFF_SG]

    z = _gelu(proj[:, OFF_SG:OFF_Q])
    u = z[:, 0:SG_C]
    v = z[:, SG_C:2 * SG_C]
    mu = jnp.mean(v, axis=-1, keepdims=True)
    vc = v - mu
    var = jnp.mean(vc * vc, axis=-1, keepdims=True)
    vb = ((vc * lax.rsqrt(var + EPS)) * lng_ref[...] + lnb_ref[...]).astype(BF)
    lane_g = lax.broadcasted_iota(jnp.int32, (SG_CHUNK, SG_C), 1) // (SG_C // SG_GROUPS)
    for j in range(tm // SG_CHUNK):
        rows = slice(j * SG_CHUNK, (j + 1) * SG_CHUNK)
        o = _dot(wsg_ref[...], vb[rows, :])
        s = o[(SG_GROUPS - 1) * SG_CHUNK:, :]
        for g in range(SG_GROUPS - 2, -1, -1):
            s = jnp.where(lane_g == g, o[g * SG_CHUNK:(g + 1) * SG_CHUNK, :], s)
        ysg_ref[rows, :] = (u[rows, :] * (s + bsg_ref[...])).astype(BF)

    cqn = (_rms(proj[:, OFF_Q:OFF_KV]) * gq_ref[...]).astype(BF)
    qq = _dot(cqn, wq_ref[...])
    ca = ca_ref[...]
    sb = sb_ref[...]
    nq = MLA_HEADS * HEAD_PAD
    for h in range(MLA_HEADS):
        cols = slice(h * HEAD_PAD, (h + 1) * HEAD_PAD)
        cols_sw = slice(nq + h * HEAD_PAD, nq + (h + 1) * HEAD_PAD)
        q_ref[:, cols] = (qq[:, cols] * ca + qq[:, cols_sw] * sb).astype(BF)

    kvn = (_rms(proj[:, OFF_KV:OFF_KR]) * gkv_ref[...]).astype(BF)
    vt_ref[...] = (lax.dot_general(wvt_ref[...], kvn, _NT, preferred_element_type=F32) + vone_ref[...]).astype(BF)
    kv = _dot(kvn, wk_ref[...])
    t = proj[:, OFF_KR:OFF_KR + LANE] * ck_ref[...]
    krr = t + pltpu.roll(t, LANE - MLA_ROPE, 1)
    lane = lax.broadcasted_iota(jnp.int32, (tm, LANE), 1)
    in_rope = (lane >= MLA_NOPE) & (lane < MLA_NOPE + MLA_ROPE)
    krp = jnp.where(in_rope, pltpu.roll(krr, MLA_NOPE, 1), 0.0)
    for h in range(MLA_HEADS):
        cols = slice(h * HEAD_PAD, (h + 1) * HEAD_PAD)
        k_ref[:, cols] = (kv[:, cols] + krp).astype(BF)


def _mix_in(x, mod_l, cond_row, seq, w, tabs, *, tm):
    t, d = x.shape
    tiles_per_seq = seq // tm
    const = lambda i: (0, 0)
    tab = lambda i: (i % tiles_per_seq, 0)
    nq = MLA_HEADS * HEAD_PAD
    row = lambda i: (i, 0)
    return pl.pallas_call(
        functools.partial(_mix_in_kernel, tm=tm),
        grid=(t // tm,),
        in_specs=[
            pl.BlockSpec((tm, d), row),
            _mod_spec(d, seq, tm, cond_row),
            pl.BlockSpec((d, N_IN_PAD), const),
            pl.BlockSpec((tm, LANE), tab),
            pl.BlockSpec((tm, LANE), tab),
            pl.BlockSpec((tm, LANE), tab),
            pl.BlockSpec((1, MLA_Q_RANK), const),
            pl.BlockSpec((MLA_Q_RANK, 2 * nq), const),
            pl.BlockSpec((1, MLA_KV_RANK), const),
            pl.BlockSpec((MLA_KV_RANK, nq), const),
            pl.BlockSpec((MLA_HEADS * V_ROWS, MLA_KV_RANK), const),
            pl.BlockSpec((MLA_HEADS * V_ROWS, 1), const),
            pl.BlockSpec((1, SG_C), const),
            pl.BlockSpec((1, SG_C), const),
            pl.BlockSpec((SG_GROUPS * SG_CHUNK, SG_CHUNK), const),
            pl.BlockSpec((SG_CHUNK, SG_C), const),
        ],
        out_specs=[
            pl.BlockSpec((tm, OFF_SG), row),
            pl.BlockSpec((tm, SG_C), row),
            pl.BlockSpec((tm, nq), row),
            pl.BlockSpec((tm, nq), row),
            pl.BlockSpec((MLA_HEADS * V_ROWS, tm), lambda i: (0, i)),
        ],
        out_shape=[
            jax.ShapeDtypeStruct((t, OFF_SG), F32),
            jax.ShapeDtypeStruct((t, SG_C), BF),
            jax.ShapeDtypeStruct((t, nq), BF),
            jax.ShapeDtypeStruct((t, nq), BF),
            jax.ShapeDtypeStruct((MLA_HEADS * V_ROWS, t), BF),
        ],
        compiler_params=_cparams(1),
        name="mix_in",
    )(x, mod_l, w["w_all"], tabs["ca"], tabs["sb"], tabs["ck"], w["gq"], w["wq"], w["gkv"], w["wk"], w["wvt"], w["vone"],
      w["lng"], w["lnb"], w["wsg"], w["bsg"])


KEY_BLOCK = 128


def _attn_kernel(*refs, n_seg, tq):
    q_ref = refs[0]
    segs = [(refs[1 + 2 * s], refs[2 + 2 * s]) for s in range(n_seg)]
    o_ref, p_buf, m_buf = refs[1 + 2 * n_seg:]
    offs = [0]
    for k_ref, _ in segs:
        offs.append(offs[-1] + k_ref.shape[0])
    n_blk = offs[-1] // KEY_BLOCK
    stages = [(t, h) for t in range(q_ref.shape[0] // tq) for h in range(MLA_HEADS)]

    def scores(i):
        t, h = stages[i]
        cols = slice(h * HEAD_PAD, (h + 1) * HEAD_PAD)
        q = q_ref[t * tq:(t + 1) * tq, cols]
        for s, (k_ref, _) in enumerate(segs):
            for j in range(0, offs[s + 1] - offs[s], KEY_BLOCK):
                st = lax.dot_general(k_ref[j:j + KEY_BLOCK, cols], q, _NT, preferred_element_type=F32)
                mb = jnp.max(st, axis=0, keepdims=True)
                blk = (offs[s] + j) // KEY_BLOCK
                m_buf[i % 2, blk:blk + 1, :] = mb
                p_buf[i % 2, offs[s] + j:offs[s] + j + KEY_BLOCK, :] = jnp.exp2(st - mb).astype(BF)

    def values(i, prev):
        t, h = stages[i]
        mall = m_buf[i % 2, 0:n_blk, :]
        w = jnp.exp2(mall - jnp.max(mall, axis=0, keepdims=True))
        acc = None
        for s, (_, vt_ref) in enumerate(segs):
            for j in range(0, offs[s + 1] - offs[s], KEY_BLOCK):
                blk = (offs[s] + j) // KEY_BLOCK
                part = _dot(vt_ref[h * V_ROWS:(h + 1) * V_ROWS, j:j + KEY_BLOCK],
                            p_buf[i % 2, offs[s] + j:offs[s] + j + KEY_BLOCK, :]) * w[blk:blk + 1, :]
                acc = part if acc is None else acc + part
        o = acc[0:MLA_V, :] / acc[MLA_V:MLA_V + 1, :]
        if h % 2 == 0:
            return o
        pair = h // 2
        o_ref[t * tq:(t + 1) * tq, pair * HEAD_PAD:(pair + 1) * HEAD_PAD] = (
            jnp.concatenate([prev, o], axis=0).T.astype(BF))
        return None

    prev = None
    n = len(stages)
    for step in range(n + 1):
        if step < n:
            scores(step)
        if step >= 1:
            prev = values(step - 1, prev)


def _attention(q, segs, *, batch, nq_seq, tq, tiles_per_step):
    t = q.shape[0]
    rows = tq * tiles_per_step
    nqt = nq_seq // rows
    width = MLA_HEADS * HEAD_PAD
    in_specs = [pl.BlockSpec((rows, width), lambda b, i: (b * nqt + i, 0))]
    args = [q]
    nk_total = 0
    for k, vt, nk in segs:
        in_specs += [pl.BlockSpec((nk, width), lambda b, i: (b, 0)),
                     pl.BlockSpec((MLA_HEADS * V_ROWS, nk), lambda b, i: (0, b))]
        args += [k, vt]
        nk_total += nk
    return pl.pallas_call(
        functools.partial(_attn_kernel, n_seg=len(segs), tq=tq),
        grid=(batch, nqt),
        in_specs=in_specs,
        out_specs=pl.BlockSpec((rows, MLA_HEADS * MLA_V), lambda b, i: (b * nqt + i, 0)),
        out_shape=jax.ShapeDtypeStruct((t, MLA_HEADS * MLA_V), BF),
        scratch_shapes=[pltpu.VMEM((2, nk_total, tq), BF),
                        pltpu.VMEM((2, -(-(nk_total // KEY_BLOCK) // 8) * 8, tq), F32)],
        compiler_params=_cparams(2),
        name="attn_lat" if len(segs) > 1 else "attn_ctx",
    )(*args)


def _filt_kernel(et_ref, w1_ref, b1_ref, f0_ref, w2_ref, b2_ref, f1_ref, w3_ref, win_ref, o_ref):
    e = et_ref[...].astype(BF)
    h = jnp.sin(f0_ref[...] * (_dot(w1_ref[...], e) + b1_ref[...]))
    h = jnp.sin(f1_ref[...] * (_dot(w2_ref[...], h.astype(BF)) + b2_ref[...]))
    win = win_ref[...]
    o_ref[...] = _dot(w3_ref[0], h.astype(BF)) * jnp.concatenate([win, win], axis=0)


def _filter_time(tabs, fw, *, pb):
    n2l = tabs["et"].shape[1]
    nb = n2l // pb
    const = lambda i: (0, 0)
    col = lambda i: (0, i)
    rows = 2 * HY_C
    return pl.pallas_call(
        _filt_kernel,
        grid=(nb,),
        in_specs=[
            pl.BlockSpec((HY_FFN, pb), col),
            pl.BlockSpec((HY_FFN, HY_FFN), const),
            pl.BlockSpec((HY_FFN, 1), const),
            pl.BlockSpec((HY_FFN, 1), const),
            pl.BlockSpec((HY_FFN, HY_FFN), const),
            pl.BlockSpec((HY_FFN, 1), const),
            pl.BlockSpec((HY_FFN, 1), const),
            pl.BlockSpec((1, rows, HY_FFN), lambda i: (i // (nb // 2), 0, 0)),
            pl.BlockSpec((HY_C, pb), col),
        ],
        out_specs=pl.BlockSpec((rows, pb), col),
        out_shape=jax.ShapeDtypeStruct((rows, n2l), F32),
        compiler_params=_cparams(1),
        name="hyena_filter",
    )(tabs["et"], fw["w1t"], fw["b1"], fw["f0"], fw["w2t"], fw["b2"], fw["f1"], fw["w3t"], tabs["win"])


DFT_CG = 8


def _dft_stage1(m1, rhs, tw_ref, lhs_ref, c0, n1f):
    r = _dot(m1, rhs)
    tr = tw_ref[0]
    ti = tw_ref[1]
    for c in range(DFT_CG):
        br = r[0:n1f, c * LANE:(c + 1) * LANE]
        bi = r[n1f:2 * n1f, c * LANE:(c + 1) * LANE]
        q0 = (c0 + c) * n1f
        lhs_ref[q0:q0 + n1f, 0:LANE] = (br * tr - bi * ti).astype(BF)
        lhs_ref[q0:q0 + n1f, LANE:2 * LANE] = (br * ti + bi * tr).astype(BF)


def _kf_kernel(k_ref, m1_ref, tw_ref, g2_ref, o_ref, lhs_ref, *, cb, n1f):
    zero_at = n1f // 2
    rowi = lax.broadcasted_iota(jnp.int32, (n1f, DFT_CG * LANE), 0)
    lanei = lax.broadcasted_iota(jnp.int32, (n1f, DFT_CG * LANE), 1)
    pad_tap = (rowi == zero_at) & ((lanei & (LANE - 1)) == 0)
    for c0 in range(0, cb, DFT_CG):
        k = jnp.concatenate([k_ref[c0 + c] for c in range(DFT_CG)], axis=1)
        _dft_stage1(m1_ref[...], jnp.where(pad_tap, 0.0, k).astype(BF), tw_ref, lhs_ref, c0, n1f)
    o_ref[...] = _dot(lhs_ref[...], g2_ref[...])


def _filter_spectrum(ktime3, consts, *, cb):
    rows, n1f, _ = ktime3.shape
    const2 = lambda i: (0, 0)
    return pl.pallas_call(
        functools.partial(_kf_kernel, cb=cb, n1f=n1f),
        grid=(rows // cb,),
        in_specs=[
            pl.BlockSpec((cb, n1f, LANE), lambda i: (i, 0, 0)),
            pl.BlockSpec((2 * n1f, n1f), const2),
            pl.BlockSpec((2, n1f, LANE), lambda i: (0, 0, 0)),
            pl.BlockSpec((2 * LANE, 2 * LANE), const2),
        ],
        out_specs=pl.BlockSpec((cb * n1f, 2 * LANE), lambda i: (i, 0)),
        out_shape=jax.ShapeDtypeStruct((rows * n1f, 2 * LANE), F32),
        scratch_shapes=[pltpu.VMEM((cb * n1f, 2 * LANE), BF)],
        compiler_params=_cparams(1),
        name="hyena_filter_spectrum",
    )(ktime3, consts["m1f"], consts["tw"], consts["g2"])


def _hy_lat_kernel(x1_ref, x2_ref, v_ref, cw_ref, hb_ref, kf_ref, m1_ref, tw_ref, g2_ref, g2i_ref, m1i_ref,
                   o_ref, zx1, zx2, zv, vbuf, lhs_ref, c_ref, *, cb, n1h, rb):
    n1f = 2 * n1h
    r = cb * n1h
    rowi = lax.broadcasted_iota(jnp.int32, (r, LANE), 0)
    lanei = lax.broadcasted_iota(jnp.int32, (r, LANE), 1)
    n1i = rowi & (n1h - 1)
    first = (n1i == 0) & (lanei == 0)
    last = (n1i == n1h - 1) & (lanei == LANE - 1)

    def bro(p):
        return jnp.broadcast_to(p, (cb, n1h, LANE)).reshape(r, LANE)

    def sconv(u3, s):
        u = u3.reshape(r, LANE)
        a = pltpu.roll(u, 1, 1)
        prev = jnp.where(lanei == 0, pltpu.roll(a, 1, 0), a)
        prev = jnp.where(first, 0.0, prev)
        f = pltpu.roll(u, LANE - 1, 1)
        nxt = jnp.where(lanei == LANE - 1, pltpu.roll(f, r - 1, 0), f)
        nxt = jnp.where(last, 0.0, nxt)
        return prev * bro(cw_ref[s, 0]) + u * bro(cw_ref[s, 1]) + nxt * bro(cw_ref[s, 2]) + bro(cw_ref[s, 3])

    for b in range(2):
        zx1[b] = sconv(x1_ref[b], 0)
        zx2[b] = sconv(x2_ref[b], 1)
        zv[b] = sconv(v_ref[b], 2)

    cg = DFT_CG
    gr = cg * n1h

    def conv_order(o, vin, gate, write):
        for c0 in range(0, cb, cg):
            zr = vin[0, c0 * n1h:c0 * n1h + gr, :]
            zi = vin[1, c0 * n1h:c0 * n1h + gr, :]
            rhs = jnp.concatenate([
                jnp.concatenate([zr[c * n1h:(c + 1) * n1h] for c in range(cg)], axis=1),
                jnp.concatenate([zi[c * n1h:(c + 1) * n1h] for c in range(cg)], axis=1)], axis=0).astype(BF)
            _dft_stage1(m1_ref[...], rhs, tw_ref, lhs_ref, c0, n1f)

        for q in range(0, cb * n1f, rb):
            x = _dot(lhs_ref[q:q + rb, :], g2_ref[...])
            kf = kf_ref[o, q:q + rb, :]
            xr, xi = x[:, 0:LANE], x[:, LANE:]
            kr, ki = kf[:, 0:LANE], kf[:, LANE:]
            y = jnp.concatenate([xr * kr - xi * ki, xr * ki + xi * kr], axis=1).astype(BF)
            c_ref[q:q + rb, :] = _dot(y, g2i_ref[...])

        tr = tw_ref[0]
        ti = tw_ref[1]
        for c0 in range(0, cb, cg):
            ars, ais = [], []
            for c in range(cg):
                t = c_ref[(c0 + c) * n1f:(c0 + c + 1) * n1f, :]
                cr, ci = t[:, 0:LANE], t[:, LANE:]
                ars.append(cr * tr + ci * ti)
                ais.append(ci * tr - cr * ti)
            rhs = jnp.concatenate([jnp.concatenate(ars, axis=1), jnp.concatenate(ais, axis=1)], axis=0).astype(BF)
            out = _dot(m1i_ref[...], rhs)
            r0 = c0 * n1h
            bias = jnp.broadcast_to(hb_ref[o, c0:c0 + cg], (cg, n1h, LANE)).reshape(gr, LANE)
            for b in range(2):
                ob = out[b * n1h:(b + 1) * n1h, :]
                y = jnp.concatenate([ob[:, c * LANE:(c + 1) * LANE] for c in range(cg)], axis=0)
                write(b, c0, r0, gate[b, r0:r0 + gr, :] * (y + vin[b, r0:r0 + gr, :] * bias))

    def write_mid(b, c0, r0, val):
        vbuf[b, r0:r0 + gr, :] = val

    def write_out(b, c0, r0, val):
        o_ref[b, c0:c0 + cg] = val.reshape(cg, n1h, LANE)

    conv_order(0, zv, zx1, write_mid)
    conv_order(1, vbuf, zx2, write_out)


def _hyena_lat(p3t, cw, hb, kf, consts, *, cb, rb):
    batch, _, n1h, _ = p3t.shape
    n1f = 2 * n1h
    ncb = HY_C // cb
    r = cb * n1h
    const2 = lambda p, c: (0, 0)
    sec = lambda k: pl.BlockSpec((2, cb, n1h, LANE), lambda p, c, k=k: (p, c + k * ncb, 0, 0))
    return pl.pallas_call(
        functools.partial(_hy_lat_kernel, cb=cb, n1h=n1h, rb=rb),
        grid=(batch // 2, ncb),
        in_specs=[
            sec(0), sec(1), sec(2),
            pl.BlockSpec((3, 4, cb, 1, LANE), lambda p, c: (0, 0, c, 0, 0)),
            pl.BlockSpec((2, cb, 1, LANE), lambda p, c: (0, c, 0, 0)),
            pl.BlockSpec((2, cb * n1f, 2 * LANE), lambda p, c: (0, c, 0)),
            pl.BlockSpec((2 * n1f, n1f), const2),
            pl.BlockSpec((2, n1f, LANE), lambda p, c: (0, 0, 0)),
            pl.BlockSpec((2 * LANE, 2 * LANE), const2),
            pl.BlockSpec((2 * LANE, 2 * LANE), const2),
            pl.BlockSpec((n1f, 2 * n1f), const2),
        ],
        out_specs=pl.BlockSpec((2, cb, n1h, LANE), lambda p, c: (p, c, 0, 0)),
        out_shape=jax.ShapeDtypeStruct((batch, HY_C, n1h, LANE), F32),
        scratch_shapes=[
            pltpu.VMEM((2, r, LANE), F32), pltpu.VMEM((2, r, LANE), F32),
            pltpu.VMEM((2, r, LANE), F32), pltpu.VMEM((2, r, LANE), F32),
            pltpu.VMEM((cb * n1f, 2 * LANE), BF), pltpu.VMEM((cb * n1f, 2 * LANE), F32),
        ],
        compiler_params=_cparams(2),
        name="hyena_lat",
    )(p3t, p3t, p3t, cw, hb, kf, consts["m1"], consts["tw"], consts["g2"], consts["g2i"], consts["m1i"])


def _hy_ctx_kernel(x1_ref, x2_ref, v_ref, cw_ref, hb_ref, kt_ref, mf_ref, mff_ref, mi_ref, o_ref, *, lc):
    lanei = lax.broadcasted_iota(jnp.int32, (HY_C, lc), 1)

    def sconv(u, s):
        prev = jnp.where(lanei == 0, 0.0, pltpu.roll(u, 1, 1))
        nxt = jnp.where(lanei == lc - 1, 0.0, pltpu.roll(u, lc - 1, 1))
        return prev * cw_ref[s, 0] + u * cw_ref[s, 1] + nxt * cw_ref[s, 2] + cw_ref[s, 3]

    zx1 = [sconv(x1_ref[b], 0) for b in range(2)]
    zx2 = [sconv(x2_ref[b], 1) for b in range(2)]
    v = [sconv(v_ref[b], 2) for b in range(2)]

    kt = kt_ref[...]
    kl = lax.broadcasted_iota(jnp.int32, kt.shape, 1)
    kf = _dot(jnp.where(kl == lc, 0.0, kt).astype(BF), mff_ref[...])
    n = 2 * lc
    for o in range(2):
        z = jnp.concatenate([v[0], v[1]], axis=1).astype(BF)
        x = _dot(z, mf_ref[...])
        xr, xi = x[:, 0:n], x[:, n:]
        kr = kf[o * HY_C:(o + 1) * HY_C, 0:n]
        ki = kf[o * HY_C:(o + 1) * HY_C, n:]
        y = _dot(jnp.concatenate([xr * kr - xi * ki, xr * ki + xi * kr], axis=1).astype(BF), mi_ref[...])
        gate = zx1 if o == 0 else zx2
        v = [gate[b] * (y[:, b * lc:(b + 1) * lc] + v[b] * hb_ref[o]) for b in range(2)]
    for b in range(2):
        o_ref[b] = v[b]


def _hyena_ctx(p3t, cw, hb, kt, consts):
    batch, _, lc = p3t.shape
    n = 2 * lc
    sec = lambda k: pl.BlockSpec((2, HY_C, lc), lambda p, k=k: (p, k, 0))
    return pl.pallas_call(
        functools.partial(_hy_ctx_kernel, lc=lc),
        grid=(batch // 2,),
        in_specs=[
            sec(0), sec(1), sec(2),
            pl.BlockSpec((3, 4, HY_C, 1), lambda p: (0, 0, 0, 0)),
            pl.BlockSpec((2, HY_C, 1), lambda p: (0, 0, 0)),
            pl.BlockSpec((2 * HY_C, n), lambda p: (0, 0)),
            pl.BlockSpec((n, 2 * n), lambda p: (0, 0)),
            pl.BlockSpec((n, 2 * n), lambda p: (0, 0)),
            pl.BlockSpec((2 * n, n), lambda p: (0, 0)),
        ],
        out_specs=pl.BlockSpec((2, HY_C, lc), lambda p: (p, 0, 0)),
        out_shape=jax.ShapeDtypeStruct((batch, HY_C, lc), F32),
        compiler_params=_cparams(1),
        name="hyena_ctx",
    )(p3t, p3t, p3t, cw, hb, kt, consts["mf"], consts["mff"], consts["mi"])


def _mxu_const(a):
    return jnp.asarray(a, F32).astype(BF)


def _dft_consts_two_stage(n1f):
    n2 = DFT_N2
    n = n1f * n2
    n1h = n1f // 2
    k1 = np.arange(n1f)
    ang1 = 2.0 * np.pi * np.outer(k1, k1) / n1f
    f1r, f1i = np.cos(ang1), -np.sin(ang1)
    m1 = np.block([[f1r[:, :n1h], -f1i[:, :n1h]], [f1i[:, :n1h], f1r[:, :n1h]]])
    m1f = np.concatenate([f1r, f1i], axis=0)
    angt = 2.0 * np.pi * np.outer(k1, np.arange(n2)) / n
    tw = np.stack([np.cos(angt), -np.sin(angt)])
    a2 = 2.0 * np.pi * np.outer(np.arange(n2), np.arange(n2)) / n2
    f2r, f2i = np.cos(a2), -np.sin(a2)
    g2 = np.block([[f2r, f2i], [-f2i, f2r]])
    g2i = np.block([[f2r, -f2i], [f2i, f2r]])
    m1i = np.block([[f1r[:n1h, :], f1i[:n1h, :]], [-f1i[:n1h, :], f1r[:n1h, :]]]) / n
    return {
        "m1": _mxu_const(m1), "m1f": _mxu_const(m1f), "tw": jnp.asarray(tw, F32),
        "g2": _mxu_const(g2), "g2i": _mxu_const(g2i), "m1i": _mxu_const(m1i),
    }


def _dft_consts_single(lc):
    n = 2 * lc
    ang = 2.0 * np.pi * np.outer(np.arange(n), np.arange(n)) / n
    fr, fi = np.cos(ang), -np.sin(ang)
    mf = np.block([[fr[:lc], fi[:lc]], [-fi[:lc], fr[:lc]]])
    mff = np.concatenate([fr, fi], axis=1)
    gr, gi = fr[:, :lc] / n, -fi[:, :lc] / n
    mi = np.block([[gr, gi], [-gi, gr]])
    return {"mf": _mxu_const(mf), "mff": _mxu_const(mff), "mi": _mxu_const(mi)}


def _filter_tables(seq):
    f32 = np.float32
    pos = np.arange(seq, dtype=f32)
    t = pos / f32(max(seq - 1, 1))
    w = f32(2.0 * math.pi) * pos / f32(seq)
    bands = np.linspace(1e-4, HY_BANDS - 1, HY_BANDS, dtype=f32)
    ang = w[:, None] * bands[None, :]
    z = np.concatenate([t[:, None], np.cos(ang), -np.sin(ang)], axis=-1)
    min_decay = abs(math.log(HY_DECAY_TARGET) / HY_SLOW_PCT)
    max_decay = abs(math.log(HY_DECAY_TARGET) / HY_FAST_PCT)
    deltas = np.linspace(min_decay, max_decay, HY_C, dtype=f32)
    window = np.exp(-t[:, None] * deltas[None, :]) + f32(HY_WINDOW_SHIFT)
    idx = np.concatenate([np.arange(seq), np.array([0]), np.arange(seq - 1, 0, -1)])
    et = np.pad(z[idx], ((0, 0), (0, HY_FFN - HY_EMB))).T
    win = window[idx].T
    return {"et": jnp.asarray(et, F32), "win": jnp.asarray(win, F32)}


def _rope_tables(n_lat, n_ctx):
    f32 = np.float32
    n_rows = n_lat // GRID_W
    rows = np.repeat(np.arange(n_rows, dtype=f32), GRID_W)
    cols = np.tile(np.arange(GRID_W, dtype=f32), n_rows)
    quarter = MLA_ROPE // 4
    inv = (f32(1.0) / np.power(f32(ROPE_BASE), np.arange(quarter, dtype=f32) / f32(quarter))).astype(f32)
    ang = np.stack([rows[:, None] * inv, cols[:, None] * inv], axis=1)
    r = np.arange(MLA_ROPE)
    axis, half, i = r // (2 * quarter), (r % (2 * quarter)) // quarter, r % quarter
    cos32 = np.cos(ang)[:, axis, i]
    ssin32 = np.sin(ang)[:, axis, i] * np.where(half == 0, -1.0, 1.0).astype(f32)
    ones = np.ones((n_lat, MLA_NOPE), f32)
    zpad = np.zeros((n_lat, HEAD_PAD - MLA_NOPE - MLA_ROPE), f32)
    z64 = np.zeros((n_lat, MLA_NOPE), f32)
    qs = f32(Q_SCALE)
    lat = {
        "ca": qs * np.concatenate([ones, cos32, zpad], axis=1),
        "sb": qs * np.concatenate([z64, ssin32, zpad], axis=1),
        "ck": np.concatenate([cos32, ssin32, z64], axis=1),
    }
    one_c = np.ones((n_ctx, MLA_NOPE + MLA_ROPE), f32)
    ctx = {
        "ca": qs * np.concatenate([one_c, np.zeros((n_ctx, HEAD_PAD - MLA_NOPE - MLA_ROPE), f32)], axis=1),
        "sb": np.zeros((n_ctx, HEAD_PAD), f32),
        "ck": np.concatenate([np.ones((n_ctx, MLA_ROPE), f32), np.zeros((n_ctx, LANE - MLA_ROPE), f32)], axis=1),
    }
    to_dev = lambda d: {k: jnp.asarray(v, F32) for k, v in d.items()}
    return to_dev(lat), to_dev(ctx)


def _rope_swap():
    quarter = MLA_ROPE // 4
    r = np.arange(MLA_ROPE)
    half = (r % (2 * quarter)) // quarter
    return np.where(half == 0, r + quarter, r - quarter)


def _mix_weights(l, w_in, mla_q_norm, mla_w_uq, mla_kv_norm, mla_w_ukv, sg_ln_g, sg_ln_b, sg_w, sg_b):
    sw = _rope_swap()
    wi = w_in[l]
    kr = wi[:, OFF_KR:N_IN]
    w_all = jnp.concatenate(
        [wi, kr[:, sw], jnp.zeros((D_MODEL, N_IN_PAD - N_IN - MLA_ROPE), F32)], axis=1).astype(BF)
    dq = MLA_NOPE + MLA_ROPE
    uq = mla_w_uq[l].reshape(MLA_Q_RANK, MLA_HEADS, dq)
    pad = HEAD_PAD - dq
    wqa = jnp.pad(uq, ((0, 0), (0, 0), (0, pad)))
    rope_sw = uq[:, :, MLA_NOPE:][:, :, sw]
    wqb = jnp.pad(rope_sw, ((0, 0), (0, 0), (MLA_NOPE, pad)))
    wq = jnp.concatenate([wqa.reshape(MLA_Q_RANK, -1), wqb.reshape(MLA_Q_RANK, -1)], axis=1).astype(BF)
    ukv = mla_w_ukv[l].reshape(MLA_KV_RANK, MLA_HEADS, MLA_NOPE + MLA_V)
    wk = jnp.pad(ukv[:, :, :MLA_NOPE], ((0, 0), (0, 0), (0, HEAD_PAD - MLA_NOPE))).reshape(MLA_KV_RANK, -1)
    wvt = jnp.pad(ukv[:, :, MLA_NOPE:], ((0, 0), (0, 0), (0, V_ROWS - MLA_V))).reshape(MLA_KV_RANK, -1).T
    vone = np.zeros((MLA_HEADS, V_ROWS, 1), np.float32)
    vone[:, MLA_V, 0] = 1.0
    return {
        "w_all": w_all,
        "gq": mla_q_norm[l].reshape(1, -1), "wq": wq,
        "gkv": mla_kv_norm[l].reshape(1, -1), "wk": wk.astype(BF), "wvt": wvt.astype(BF),
        "vone": jnp.asarray(vone.reshape(MLA_HEADS * V_ROWS, 1)),
        "lng": sg_ln_g[l].reshape(1, -1), "lnb": sg_ln_b[l].reshape(1, -1),
        "wsg": sg_w[l].reshape(SG_GROUPS * SG_CHUNK, SG_CHUNK).astype(BF),
        "bsg": jnp.repeat(sg_b[l].T, SG_C // SG_GROUPS, axis=1),
    }


def _filter_weights(l, hy_w1, hy_b1, hy_w2, hy_b2, hy_w3, hy_freq):
    w1t = jnp.pad(hy_w1[l], ((0, HY_FFN - HY_EMB), (0, 0))).T.astype(BF)
    w3 = hy_w3[l].reshape(HY_FFN, 2, 2, HY_C)
    w3t = jnp.transpose(w3, (2, 1, 3, 0)).reshape(2, 2 * HY_C, HY_FFN).astype(BF)
    col = lambda a: a.reshape(HY_FFN, 1)
    return {"w1t": w1t, "b1": col(hy_b1[l]), "f0": col(hy_freq[l, 0]),
            "w2t": hy_w2[l].T.astype(BF), "b2": col(hy_b2[l]), "f1": col(hy_freq[l, 1]), "w3t": w3t}


def kernel(x, c, ctx, c_ctx, ada_w, ada_b, ffn1_wg, ffn1_wu, ffn1_wd, w_in, w_out, hy_conv_w, hy_conv_b, hy_w1, hy_b1, hy_w2, hy_b2, hy_w3, hy_freq, hy_bias, sg_ln_g, sg_ln_b, sg_w, sg_b, mla_q_norm, mla_w_uq, mla_kv_norm, mla_w_ukv, ffn2_wg, ffn2_wu, ffn2_wd, final_norm):
    batch, n_lat, d = x.shape
    n_ctx = ctx.shape[1]
    depth = ada_w.shape[0]
    assert d == D_MODEL and batch % 2 == 0 and batch <= 4
    assert n_lat % (2 * LANE) == 0 and n_ctx % SG_CHUNK == 0
    n1h = n_lat // LANE
    ctx_row = batch

    cc = jnp.concatenate([c, c_ctx[None, :], jnp.zeros((8 - batch - 1, d), F32)], axis=0)
    mods = _modulation(cc, ada_w, ada_b).reshape(depth, 8, N_MOD, d)

    rope_lat, rope_ctx = _rope_tables(n_lat, n_ctx)
    ftab_lat = _filter_tables(n_lat)
    ftab_ctx = _filter_tables(n_ctx)
    dft_lat = _dft_consts_two_stage(2 * n1h)
    dft_ctx = _dft_consts_single(n_ctx)

    xl = x.reshape(batch * n_lat, d)
    xc = ctx.reshape(batch * n_ctx, d)
    tm = 512
    ffn1_bf = [w.astype(BF) for w in (ffn1_wg, ffn1_wu, ffn1_wd)]
    ffn2_bf = [w.astype(BF) for w in (ffn2_wg, ffn2_wu, ffn2_wd)]
    for l in range(depth):
        last = l == depth - 1
        mod_l = mods[l]
        f1 = tuple(w[l] for w in ffn1_bf)
        f2 = tuple(w[l] for w in ffn2_bf)

        xl = _ffn(xl, mod_l, 0, *f1, tm=tm, seq=n_lat)
        xc = _ffn(xc, mod_l, 0, *f1, tm=tm, seq=n_ctx, cond_row=ctx_row)

        mw = _mix_weights(l, w_in, mla_q_norm, mla_w_uq, mla_kv_norm, mla_w_ukv, sg_ln_g, sg_ln_b, sg_w, sg_b)
        p3, ysg, q, k, vt = _mix_in(xl, mod_l, None, n_lat, mw, rope_lat, tm=tm)
        p3c, ysgc, qc, kc, vtc = _mix_in(xc, mod_l, ctx_row, n_ctx, mw, rope_ctx, tm=n_ctx)

        y_attn = _attention(q, [(kc, vtc, n_ctx), (k, vt, n_lat)], batch=batch, nq_seq=n_lat, tq=256,
                            tiles_per_step=1)

        fw = _filter_weights(l, hy_w1, hy_b1, hy_w2, hy_b2, hy_w3, hy_freq)
        cw = jnp.concatenate([hy_conv_w[l], hy_conv_b[l][None, :]], axis=0)
        cw = jnp.transpose(cw.reshape(4, 3, HY_C), (1, 0, 2))

        ktime = _filter_time(ftab_lat, fw, pb=1024)
        hb = hy_bias[l] + ktime[:, n_lat].reshape(2, HY_C)
        kf = _filter_spectrum(ktime.reshape(2 * HY_C, 2 * n1h, LANE), dft_lat, cb=32)
        p3t = jnp.transpose(p3.reshape(batch, n_lat, OFF_SG), (0, 2, 1)).reshape(batch, OFF_SG, n1h, LANE)
        y_hy = _hyena_lat(
            p3t,
            jnp.broadcast_to(cw[:, :, :, None, None], (3, 4, HY_C, 1, LANE)),
            jnp.broadcast_to(hb[:, :, None, None], (2, HY_C, 1, LANE)),
            kf.reshape(2, HY_C * 2 * n1h, 2 * LANE), dft_lat, cb=32, rb=512)
        y_hy = jnp.transpose(y_hy.reshape(batch, HY_C, n_lat), (0, 2, 1)).reshape(batch * n_lat, HY_C)

        wo = w_out[l].astype(BF)
        xl = _ffn(xl, mod_l, 6, *f2, tm=tm, seq=n_lat, mixer=(y_hy, ysg, y_attn, wo),
                  final_norm=final_norm if last else None)

        if not last:
            ktime_c = _filter_time(ftab_ctx, fw, pb=n_ctx)
            hbc = hy_bias[l] + ktime_c[:, n_ctx].reshape(2, HY_C)
            p3ct = jnp.transpose(p3c.reshape(batch, n_ctx, OFF_SG), (0, 2, 1))
            yc_hy = _hyena_ctx(p3ct, cw[:, :, :, None], hbc[:, :, None], ktime_c, dft_ctx)
            yc_hy = jnp.transpose(yc_hy, (0, 2, 1)).reshape(batch * n_ctx, HY_C)
            yc_attn = _attention(qc, [(kc, vtc, n_ctx)], batch=batch, nq_seq=n_ctx, tq=n_ctx, tiles_per_step=1)
            xc = _ffn(xc, mod_l, 6, *f2, tm=tm, seq=n_ctx, cond_row=ctx_row, mixer=(yc_hy, ysgc, yc_attn, wo))
    return xl.reshape(batch, n_lat, d)
```

```python
import functools
import math

import numpy as np
import jax
import jax.numpy as jnp
from jax import lax
from jax.experimental import pallas as pl
from jax.experimental.pallas import tpu as pltpu

F32 = jnp.float32
BF = jnp.bfloat16

D_MODEL = 1024
D_FF = 2816
N_MOD = 9
EPS = 1e-6
GRID_W = 64

HY_C = 256
HY_BANDS = 16
HY_EMB = 1 + 2 * HY_BANDS
HY_FFN = 64
HY_DECAY_TARGET = 1e-2
HY_FAST_PCT = 0.3
HY_SLOW_PCT = 1.5
HY_WINDOW_SHIFT = 0.05

SG_C = 256
SG_GROUPS = 4
SG_CHUNK = 128

MLA_HEADS = 8
MLA_NOPE = 64
MLA_ROPE = 32
MLA_V = 64
MLA_Q_RANK = 384
MLA_KV_RANK = 256
MLA_SCALE = (MLA_NOPE + MLA_ROPE) ** -0.5
Q_SCALE = MLA_SCALE * math.log2(math.e)
ROPE_BASE = 10000.0

OFF_SG = 3 * HY_C
OFF_Q = OFF_SG + 2 * SG_C
OFF_KV = OFF_Q + MLA_Q_RANK
OFF_KR = OFF_KV + MLA_KV_RANK
N_IN = OFF_KR + MLA_ROPE
N_IN_PAD = 2048

LANE = 128
HEAD_PAD = 128
V_ROWS = 80
FFN_CHUNK = 256
VMEM_LIMIT = 56 * 1024 * 1024

DFT_N2 = LANE


def _cparams(n_axes, vmem=VMEM_LIMIT):
    return pltpu.CompilerParams(dimension_semantics=("arbitrary",) * n_axes, vmem_limit_bytes=vmem)


def _mod_spec(d, seq, tm, cond_row):
    if cond_row is None:
        tiles_per_seq = seq // tm
        return pl.BlockSpec((None, N_MOD, d), lambda i: (i // tiles_per_seq, 0, 0))
    return pl.BlockSpec((None, N_MOD, d), lambda i: (cond_row, 0, 0))


def _rms(x):
    return x * lax.rsqrt(jnp.mean(x * x, axis=-1, keepdims=True) + EPS)


def _dot(a, b):
    return jnp.dot(a, b, preferred_element_type=F32)


_NT = (((1,), (1,)), ((), ()))


def _mod_kernel(c_ref, w_ref, b_ref, o_ref):
    c = c_ref[...]
    s = c * jax.nn.sigmoid(c)
    o_ref[0] = _dot(s.astype(BF), w_ref[0].astype(BF)) + b_ref[0]


def _modulation(cc, ada_w, ada_b):
    depth, d, n = ada_w.shape
    tn = 1152
    return pl.pallas_call(
        _mod_kernel,
        grid=(depth, n // tn),
        in_specs=[
            pl.BlockSpec((8, d), lambda l, j: (0, 0)),
            pl.BlockSpec((1, d, tn), lambda l, j: (l, 0, j)),
            pl.BlockSpec((1, 1, tn), lambda l, j: (l, 0, j)),
        ],
        out_specs=pl.BlockSpec((1, 8, tn), lambda l, j: (l, 0, j)),
        out_shape=jax.ShapeDtypeStruct((depth, 8, n), F32),
        compiler_params=_cparams(2),
        name="adaln_mod",
    )(cc, ada_w, ada_b.reshape(depth, 1, n))


def _ffn_kernel(*refs, k0, mixer_out, final):
    refs = list(refs)
    x_ref, mod_ref = refs[:2]
    del refs[:2]
    if mixer_out:
        yh_ref, ys_ref, ya_ref, wo_ref = refs[:4]
        del refs[:4]
    wg_ref, wu_ref, wd_ref = refs[:3]
    del refs[:3]
    if final:
        fn_ref = refs.pop(0)
    o_ref, a_ref = refs
    x = x_ref[...]
    if mixer_out:
        acc = _dot(yh_ref[...].astype(BF), wo_ref[0:HY_C, :])
        acc = acc + _dot(ys_ref[...], wo_ref[HY_C:HY_C + SG_C, :])
        acc = acc + _dot(ya_ref[...], wo_ref[HY_C + SG_C:, :])
        x = x + mod_ref[5:6, :] * acc
    shift = mod_ref[k0:k0 + 1, :]
    scale = mod_ref[k0 + 1:k0 + 2, :]
    gate = mod_ref[k0 + 2:k0 + 3, :]
    hb = (_rms(x) * (1.0 + scale) + shift).astype(BF)
    for j in range(D_FF // FFN_CHUNK):
        cols = slice(j * FFN_CHUNK, (j + 1) * FFN_CHUNK)
        g = _dot(hb, wg_ref[:, cols])
        u = _dot(hb, wu_ref[:, cols])
        a_ref[:, cols] = (g * jax.nn.sigmoid(g) * u).astype(BF)
    y = x + (0.5 * gate) * _dot(a_ref[...], wd_ref[...])
    if final:
        y = _rms(y) * fn_ref[...]
    o_ref[...] = y


def _ffn(x, mod_l, k0, layer, wg, wu, wd, *, tm, seq, cond_row=None, mixer=None, final_norm=None):
    t, d = x.shape
    final = final_norm is not None
    const = lambda i: (0, 0)
    row = lambda i: (i, 0)
    slab = lambda i: (layer, 0, 0)
    in_specs = [pl.BlockSpec((tm, d), row), _mod_spec(d, seq, tm, cond_row)]
    args = [x, mod_l]
    if mixer is not None:
        in_specs += [
            pl.BlockSpec((tm, HY_C), row),
            pl.BlockSpec((tm, SG_C), row),
            pl.BlockSpec((tm, MLA_HEADS * MLA_V), row),
            pl.BlockSpec((None, HY_C + SG_C + MLA_HEADS * MLA_V, d), slab, pipeline_mode=pl.Buffered(1)),
        ]
        args += list(mixer)
    in_specs += [
        pl.BlockSpec((None, d, D_FF), slab, pipeline_mode=pl.Buffered(1)),
        pl.BlockSpec((None, d, D_FF), slab, pipeline_mode=pl.Buffered(1)),
        pl.BlockSpec((None, D_FF, d), slab, pipeline_mode=pl.Buffered(1)),
    ]
    args += [wg, wu, wd]
    if final:
        in_specs.append(pl.BlockSpec((1, d), const))
        args.append(final_norm.reshape(1, d))
    return pl.pallas_call(
        functools.partial(_ffn_kernel, k0=k0, mixer_out=mixer is not None, final=final),
        grid=(t // tm,),
        in_specs=in_specs,
        out_specs=pl.BlockSpec((tm, d), lambda i: (i, 0)),
        out_shape=jax.ShapeDtypeStruct((t, d), F32),
        scratch_shapes=[pltpu.VMEM((tm, D_FF), BF)],
        compiler_params=_cparams(1),
        name="ffn",
    )(*args)


def _gelu(x):
    return 0.5 * x * (1.0 + lax.erf(x * (1.0 / math.sqrt(2.0))))


def _mix_in_kernel(x_ref, mod_ref, wall_ref, ca_ref, sb_ref, ck_ref, gq_ref, wq_ref, gkv_ref, wk_ref, wvt_ref,
                   vone_ref, lng_ref, lnb_ref, wsg_ref, bsg_ref,
                   p3_ref, ysg_ref, q_ref, k_ref, vt_ref, *, tm):
    x = x_ref[...]
    shift = mod_ref[3:4, :]
    scale = mod_ref[4:5, :]
    hb = (_rms(x) * (1.0 + scale) + shift).astype(BF)
    proj = _dot(hb, wall_ref[...])
    p3_ref[...] = proj[:, 0:OFF_SG]

    z = _gelu(proj[:, OFF_SG:OFF_Q])
    u = z[:, 0:SG_C]
    v = z[:, SG_C:2 * SG_C]
    mu = jnp.mean(v, axis=-1, keepdims=True)
    vc = v - mu
    var = jnp.mean(vc * vc, axis=-1, keepdims=True)
    vb = ((vc * lax.rsqrt(var + EPS)) * lng_ref[...] + lnb_ref[...]).astype(BF)
    lane_g = lax.broadcasted_iota(jnp.int32, (SG_CHUNK, SG_C), 1) // (SG_C // SG_GROUPS)
    for j in range(tm // SG_CHUNK):
        rows = slice(j * SG_CHUNK, (j + 1) * SG_CHUNK)
        o = _dot(wsg_ref[...], vb[rows, :])
        s = o[(SG_GROUPS - 1) * SG_CHUNK:, :]
        for g in range(SG_GROUPS - 2, -1, -1):
            s = jnp.where(lane_g == g, o[g * SG_CHUNK:(g + 1) * SG_CHUNK, :], s)
        ysg_ref[rows, :] = (u[rows, :] * (s + bsg_ref[...])).astype(BF)

    cqn = (_rms(proj[:, OFF_Q:OFF_KV]) * gq_ref[...]).astype(BF)
    qq = _dot(cqn, wq_ref[...])
    ca = ca_ref[...]
    sb = sb_ref[...]
    nq = MLA_HEADS * HEAD_PAD
    for h in range(MLA_HEADS):
        cols = slice(h * HEAD_PAD, (h + 1) * HEAD_PAD)
        cols_sw = slice(nq + h * HEAD_PAD, nq + (h + 1) * HEAD_PAD)
        q_ref[:, cols] = (qq[:, cols] * ca + qq[:, cols_sw] * sb).astype(BF)

    kvn = (_rms(proj[:, OFF_KV:OFF_KR]) * gkv_ref[...]).astype(BF)
    vt_ref[...] = (lax.dot_general(wvt_ref[...], kvn, _NT, preferred_element_type=F32) + vone_ref[...]).astype(BF)
    kv = _dot(kvn, wk_ref[...])
    t = proj[:, OFF_KR:OFF_KR + LANE] * ck_ref[...]
    krr = t + pltpu.roll(t, LANE - MLA_ROPE, 1)
    lane = lax.broadcasted_iota(jnp.int32, (tm, LANE), 1)
    in_rope = (lane >= MLA_NOPE) & (lane < MLA_NOPE + MLA_ROPE)
    krp = jnp.where(in_rope, pltpu.roll(krr, MLA_NOPE, 1), 0.0)
    for h in range(MLA_HEADS):
        cols = slice(h * HEAD_PAD, (h + 1) * HEAD_PAD)
        k_ref[:, cols] = (kv[:, cols] + krp).astype(BF)


def _mix_in(x, mod_l, cond_row, seq, w, tabs, *, tm):
    t, d = x.shape
    tiles_per_seq = seq // tm
    const = lambda i: (0, 0)
    tab = lambda i: (i % tiles_per_seq, 0)
    nq = MLA_HEADS * HEAD_PAD
    row = lambda i: (i, 0)
    return pl.pallas_call(
        functools.partial(_mix_in_kernel, tm=tm),
        grid=(t // tm,),
        in_specs=[
            pl.BlockSpec((tm, d), row),
            _mod_spec(d, seq, tm, cond_row),
            pl.BlockSpec((d, N_IN_PAD), const),
            pl.BlockSpec((tm, LANE), tab),
            pl.BlockSpec((tm, LANE), tab),
            pl.BlockSpec((tm, LANE), tab),
            pl.BlockSpec((1, MLA_Q_RANK), const),
            pl.BlockSpec((MLA_Q_RANK, 2 * nq), const),
            pl.BlockSpec((1, MLA_KV_RANK), const),
            pl.BlockSpec((MLA_KV_RANK, nq), const),
            pl.BlockSpec((MLA_HEADS * V_ROWS, MLA_KV_RANK), const),
            pl.BlockSpec((MLA_HEADS * V_ROWS, 1), const),
            pl.BlockSpec((1, SG_C), const),
            pl.BlockSpec((1, SG_C), const),
            pl.BlockSpec((SG_GROUPS * SG_CHUNK, SG_CHUNK), const),
            pl.BlockSpec((SG_CHUNK, SG_C), const),
        ],
        out_specs=[
            pl.BlockSpec((tm, OFF_SG), row),
            pl.BlockSpec((tm, SG_C), row),
            pl.BlockSpec((tm, nq), row),
            pl.BlockSpec((tm, nq), row),
            pl.BlockSpec((MLA_HEADS * V_ROWS, tm), lambda i: (0, i)),
        ],
        out_shape=[
            jax.ShapeDtypeStruct((t, OFF_SG), F32),
            jax.ShapeDtypeStruct((t, SG_C), BF),
            jax.ShapeDtypeStruct((t, nq), BF),
            jax.ShapeDtypeStruct((t, nq), BF),
            jax.ShapeDtypeStruct((MLA_HEADS * V_ROWS, t), BF),
        ],
        compiler_params=_cparams(1),
        name="mix_in",
    )(x, mod_l, w["w_all"], tabs["ca"], tabs["sb"], tabs["ck"], w["gq"], w["wq"], w["gkv"], w["wk"], w["wvt"], w["vone"],
      w["lng"], w["lnb"], w["wsg"], w["bsg"])


KEY_BLOCK = 256


def _attn_kernel(*refs, n_seg, tq):
    q_ref = refs[0]
    segs = [(refs[1 + 2 * s], refs[2 + 2 * s]) for s in range(n_seg)]
    o_ref, p_buf, m_buf = refs[1 + 2 * n_seg:]
    offs = [0]
    for k_ref, _ in segs:
        offs.append(offs[-1] + k_ref.shape[0])
    blocks = [(s, j, min(KEY_BLOCK, offs[s + 1] - offs[s]))
              for s in range(n_seg) for j in range(0, offs[s + 1] - offs[s], KEY_BLOCK)]
    stages = [(t, h) for t in range(q_ref.shape[0] // tq) for h in range(MLA_HEADS)]

    def scores(i):
        t, h = stages[i]
        cols = slice(h * HEAD_PAD, (h + 1) * HEAD_PAD)
        q = q_ref[t * tq:(t + 1) * tq, cols]
        for b, (s, j, n) in enumerate(blocks):
            st = lax.dot_general(segs[s][0][j:j + n, cols], q, _NT, preferred_element_type=F32)
            mb = jnp.max(st, axis=0, keepdims=True)
            m_buf[i % 2, b:b + 1, :] = mb
            p_buf[i % 2, offs[s] + j:offs[s] + j + n, :] = jnp.exp2(st - mb).astype(BF)

    def values(i, prev):
        t, h = stages[i]
        mall = m_buf[i % 2, 0:len(blocks), :]
        w = jnp.exp2(mall - jnp.max(mall, axis=0, keepdims=True))
        acc = None
        for b, (s, j, n) in enumerate(blocks):
            part = _dot(segs[s][1][h * V_ROWS:(h + 1) * V_ROWS, j:j + n],
                        p_buf[i % 2, offs[s] + j:offs[s] + j + n, :]) * w[b:b + 1, :]
            acc = part if acc is None else acc + part
        o = acc[0:MLA_V, :] / acc[MLA_V:MLA_V + 1, :]
        if h % 2 == 0:
            return o
        pair = h // 2
        o_ref[t * tq:(t + 1) * tq, pair * HEAD_PAD:(pair + 1) * HEAD_PAD] = (
            jnp.concatenate([prev, o], axis=0).T.astype(BF))
        return None

    prev = None
    n = len(stages)
    for step in range(n + 1):
        if step < n:
            scores(step)
        if step >= 1:
            prev = values(step - 1, prev)


def _attention(q, segs, *, batch, nq_seq, tq, tiles_per_step):
    t = q.shape[0]
    rows = tq * tiles_per_step
    nqt = nq_seq // rows
    width = MLA_HEADS * HEAD_PAD
    in_specs = [pl.BlockSpec((rows, width), lambda b, i: (b * nqt + i, 0))]
    args = [q]
    nk_total = n_blocks = 0
    for k, vt, nk in segs:
        in_specs += [pl.BlockSpec((nk, width), lambda b, i: (b, 0)),
                     pl.BlockSpec((MLA_HEADS * V_ROWS, nk), lambda b, i: (0, b))]
        args += [k, vt]
        nk_total += nk
        n_blocks += pl.cdiv(nk, KEY_BLOCK)
    assert all(nk % min(KEY_BLOCK, nk) == 0 for _, _, nk in segs)
    return pl.pallas_call(
        functools.partial(_attn_kernel, n_seg=len(segs), tq=tq),
        grid=(batch, nqt),
        in_specs=in_specs,
        out_specs=pl.BlockSpec((rows, MLA_HEADS * MLA_V), lambda b, i: (b * nqt + i, 0)),
        out_shape=jax.ShapeDtypeStruct((t, MLA_HEADS * MLA_V), BF),
        scratch_shapes=[pltpu.VMEM((2, nk_total, tq), BF),
                        pltpu.VMEM((2, 8 * pl.cdiv(n_blocks, 8), tq), F32)],
        compiler_params=_cparams(2),
        name="attn_lat" if len(segs) > 1 else "attn_ctx",
    )(*args)


def _filt_kernel(et_ref, w1_ref, b1_ref, f0_ref, w2_ref, b2_ref, f1_ref, w3_ref, win_ref, o_ref):
    e = et_ref[...].astype(BF)
    h = jnp.sin(f0_ref[...] * (_dot(w1_ref[...], e) + b1_ref[...]))
    h = jnp.sin(f1_ref[...] * (_dot(w2_ref[...], h.astype(BF)) + b2_ref[...]))
    win = win_ref[...]
    o_ref[...] = _dot(w3_ref[0], h.astype(BF)) * jnp.concatenate([win, win], axis=0)


def _filter_time(tabs, fw, *, pb):
    n2l = tabs["et"].shape[1]
    nb = n2l // pb
    const = lambda i: (0, 0)
    col = lambda i: (0, i)
    rows = 2 * HY_C
    return pl.pallas_call(
        _filt_kernel,
        grid=(nb,),
        in_specs=[
            pl.BlockSpec((HY_FFN, pb), col),
            pl.BlockSpec((HY_FFN, HY_FFN), const),
            pl.BlockSpec((HY_FFN, 1), const),
            pl.BlockSpec((HY_FFN, 1), const),
            pl.BlockSpec((HY_FFN, HY_FFN), const),
            pl.BlockSpec((HY_FFN, 1), const),
            pl.BlockSpec((HY_FFN, 1), const),
            pl.BlockSpec((1, rows, HY_FFN), lambda i: (i // (nb // 2), 0, 0)),
            pl.BlockSpec((HY_C, pb), col),
        ],
        out_specs=pl.BlockSpec((rows, pb), col),
        out_shape=jax.ShapeDtypeStruct((rows, n2l), F32),
        compiler_params=_cparams(1),
        name="hyena_filter",
    )(tabs["et"], fw["w1t"], fw["b1"], fw["f0"], fw["w2t"], fw["b2"], fw["f1"], fw["w3t"], tabs["win"])


DFT_CG = 8


def _dft_stage1(m1, rhs, tw_ref, lhs_ref, c0, n1f):
    r = _dot(m1, rhs)
    tr = tw_ref[0]
    ti = tw_ref[1]
    for c in range(DFT_CG):
        br = r[0:n1f, c * LANE:(c + 1) * LANE]
        bi = r[n1f:2 * n1f, c * LANE:(c + 1) * LANE]
        q0 = (c0 + c) * n1f
        lhs_ref[q0:q0 + n1f, 0:LANE] = (br * tr - bi * ti).astype(BF)
        lhs_ref[q0:q0 + n1f, LANE:2 * LANE] = (br * ti + bi * tr).astype(BF)


def _kf_kernel(k_ref, m1_ref, tw_ref, g2_ref, o_ref, lhs_ref, *, cb, n1f):
    zero_at = n1f // 2
    rowi = lax.broadcasted_iota(jnp.int32, (n1f, DFT_CG * LANE), 0)
    lanei = lax.broadcasted_iota(jnp.int32, (n1f, DFT_CG * LANE), 1)
    pad_tap = (rowi == zero_at) & ((lanei & (LANE - 1)) == 0)
    for c0 in range(0, cb, DFT_CG):
        k = jnp.concatenate([k_ref[c0 + c] for c in range(DFT_CG)], axis=1)
        _dft_stage1(m1_ref[...], jnp.where(pad_tap, 0.0, k).astype(BF), tw_ref, lhs_ref, c0, n1f)
    o_ref[...] = _dot(lhs_ref[...], g2_ref[...])


def _filter_spectrum(ktime3, consts, *, cb):
    rows, n1f, _ = ktime3.shape
    const2 = lambda i: (0, 0)
    return pl.pallas_call(
        functools.partial(_kf_kernel, cb=cb, n1f=n1f),
        grid=(rows // cb,),
        in_specs=[
            pl.BlockSpec((cb, n1f, LANE), lambda i: (i, 0, 0)),
            pl.BlockSpec((2 * n1f, n1f), const2),
            pl.BlockSpec((2, n1f, LANE), lambda i: (0, 0, 0)),
            pl.BlockSpec((2 * LANE, 2 * LANE), const2),
        ],
        out_specs=pl.BlockSpec((cb * n1f, 2 * LANE), lambda i: (i, 0)),
        out_shape=jax.ShapeDtypeStruct((rows * n1f, 2 * LANE), F32),
        scratch_shapes=[pltpu.VMEM((cb * n1f, 2 * LANE), BF)],
        compiler_params=_cparams(1),
        name="hyena_filter_spectrum",
    )(ktime3, consts["m1f"], consts["tw"], consts["g2"])


def _hy_lat_kernel(x1_ref, x2_ref, v_ref, cw_ref, hb_ref, kf_ref, m1_ref, tw_ref, g2_ref, g2i_ref, m1i_ref,
                   o_ref, zx1, zx2, zv, vbuf, lhs_ref, c_ref, *, cb, n1h, rb):
    n1f = 2 * n1h
    r = cb * n1h
    rowi = lax.broadcasted_iota(jnp.int32, (r, LANE), 0)
    lanei = lax.broadcasted_iota(jnp.int32, (r, LANE), 1)
    n1i = rowi & (n1h - 1)
    first = (n1i == 0) & (lanei == 0)
    last = (n1i == n1h - 1) & (lanei == LANE - 1)

    def bro(p):
        return jnp.broadcast_to(p, (cb, n1h, LANE)).reshape(r, LANE)

    def sconv(u3, s):
        u = u3.reshape(r, LANE)
        a = pltpu.roll(u, 1, 1)
        prev = jnp.where(lanei == 0, pltpu.roll(a, 1, 0), a)
        prev = jnp.where(first, 0.0, prev)
        f = pltpu.roll(u, LANE - 1, 1)
        nxt = jnp.where(lanei == LANE - 1, pltpu.roll(f, r - 1, 0), f)
        nxt = jnp.where(last, 0.0, nxt)
        return prev * bro(cw_ref[s, 0]) + u * bro(cw_ref[s, 1]) + nxt * bro(cw_ref[s, 2]) + bro(cw_ref[s, 3])

    for b in range(2):
        zx1[b] = sconv(x1_ref[b], 0)
        zx2[b] = sconv(x2_ref[b], 1)
        zv[b] = sconv(v_ref[b], 2)

    cg = DFT_CG
    gr = cg * n1h

    def conv_order(o, vin, gate, write):
        for c0 in range(0, cb, cg):
            zr = vin[0, c0 * n1h:c0 * n1h + gr, :]
            zi = vin[1, c0 * n1h:c0 * n1h + gr, :]
            rhs = jnp.concatenate([
                jnp.concatenate([zr[c * n1h:(c + 1) * n1h] for c in range(cg)], axis=1),
                jnp.concatenate([zi[c * n1h:(c + 1) * n1h] for c in range(cg)], axis=1)], axis=0).astype(BF)
            _dft_stage1(m1_ref[...], rhs, tw_ref, lhs_ref, c0, n1f)

        for q in range(0, cb * n1f, rb):
            x = _dot(lhs_ref[q:q + rb, :], g2_ref[...])
            kf = kf_ref[o, q:q + rb, :]
            xr, xi = x[:, 0:LANE], x[:, LANE:]
            kr, ki = kf[:, 0:LANE], kf[:, LANE:]
            y = jnp.concatenate([xr * kr - xi * ki, xr * ki + xi * kr], axis=1).astype(BF)
            c_ref[q:q + rb, :] = _dot(y, g2i_ref[...])

        tr = tw_ref[0]
        ti = tw_ref[1]
        for c0 in range(0, cb, cg):
            ars, ais = [], []
            for c in range(cg):
                t = c_ref[(c0 + c) * n1f:(c0 + c + 1) * n1f, :]
                cr, ci = t[:, 0:LANE], t[:, LANE:]
                ars.append(cr * tr + ci * ti)
                ais.append(ci * tr - cr * ti)
            rhs = jnp.concatenate([jnp.concatenate(ars, axis=1), jnp.concatenate(ais, axis=1)], axis=0).astype(BF)
            out = _dot(m1i_ref[...], rhs)
            r0 = c0 * n1h
            bias = jnp.broadcast_to(hb_ref[o, c0:c0 + cg], (cg, n1h, LANE)).reshape(gr, LANE)
            for b in range(2):
                ob = out[b * n1h:(b + 1) * n1h, :]
                y = jnp.concatenate([ob[:, c * LANE:(c + 1) * LANE] for c in range(cg)], axis=0)
                write(b, c0, r0, gate[b, r0:r0 + gr, :] * (y + vin[b, r0:r0 + gr, :] * bias))

    def write_mid(b, c0, r0, val):
        vbuf[b, r0:r0 + gr, :] = val

    def write_out(b, c0, r0, val):
        o_ref[b, c0:c0 + cg] = val.reshape(cg, n1h, LANE)

    conv_order(0, zv, zx1, write_mid)
    conv_order(1, vbuf, zx2, write_out)


def _hyena_lat(p3t, cw, hb, kf, consts, *, cb, rb):
    batch, _, n1h, _ = p3t.shape
    n1f = 2 * n1h
    ncb = HY_C // cb
    r = cb * n1h
    const2 = lambda p, c: (0, 0)
    sec = lambda k: pl.BlockSpec((2, cb, n1h, LANE), lambda p, c, k=k: (p, c + k * ncb, 0, 0))
    return pl.pallas_call(
        functools.partial(_hy_lat_kernel, cb=cb, n1h=n1h, rb=rb),
        grid=(batch // 2, ncb),
        in_specs=[
            sec(0), sec(1), sec(2),
            pl.BlockSpec((3, 4, cb, 1, LANE), lambda p, c: (0, 0, c, 0, 0)),
            pl.BlockSpec((2, cb, 1, LANE), lambda p, c: (0, c, 0, 0)),
            pl.BlockSpec((2, cb * n1f, 2 * LANE), lambda p, c: (0, c, 0)),
            pl.BlockSpec((2 * n1f, n1f), const2),
            pl.BlockSpec((2, n1f, LANE), lambda p, c: (0, 0, 0)),
            pl.BlockSpec((2 * LANE, 2 * LANE), const2),
            pl.BlockSpec((2 * LANE, 2 * LANE), const2),
            pl.BlockSpec((n1f, 2 * n1f), const2),
        ],
        out_specs=pl.BlockSpec((2, cb, n1h, LANE), lambda p, c: (p, c, 0, 0)),
        out_shape=jax.ShapeDtypeStruct((batch, HY_C, n1h, LANE), F32),
        scratch_shapes=[
            pltpu.VMEM((2, r, LANE), F32), pltpu.VMEM((2, r, LANE), F32),
            pltpu.VMEM((2, r, LANE), F32), pltpu.VMEM((2, r, LANE), F32),
            pltpu.VMEM((cb * n1f, 2 * LANE), BF), pltpu.VMEM((cb * n1f, 2 * LANE), F32),
        ],
        compiler_params=_cparams(2),
        name="hyena_lat",
    )(p3t, p3t, p3t, cw, hb, kf, consts["m1"], consts["tw"], consts["g2"], consts["g2i"], consts["m1i"])


def _hy_ctx_kernel(x1_ref, x2_ref, v_ref, cw_ref, hb_ref, kt_ref, mf_ref, mff_ref, mi_ref, o_ref, *, lc):
    lanei = lax.broadcasted_iota(jnp.int32, (HY_C, lc), 1)

    def sconv(u, s):
        prev = jnp.where(lanei == 0, 0.0, pltpu.roll(u, 1, 1))
        nxt = jnp.where(lanei == lc - 1, 0.0, pltpu.roll(u, lc - 1, 1))
        return prev * cw_ref[s, 0] + u * cw_ref[s, 1] + nxt * cw_ref[s, 2] + cw_ref[s, 3]

    zx1 = [sconv(x1_ref[b], 0) for b in range(2)]
    zx2 = [sconv(x2_ref[b], 1) for b in range(2)]
    v = [sconv(v_ref[b], 2) for b in range(2)]

    kt = kt_ref[...]
    kl = lax.broadcasted_iota(jnp.int32, kt.shape, 1)
    kf = _dot(jnp.where(kl == lc, 0.0, kt).astype(BF), mff_ref[...])
    n = 2 * lc
    for o in range(2):
        z = jnp.concatenate([v[0], v[1]], axis=1).astype(BF)
        x = _dot(z, mf_ref[...])
        xr, xi = x[:, 0:n], x[:, n:]
        kr = kf[o * HY_C:(o + 1) * HY_C, 0:n]
        ki = kf[o * HY_C:(o + 1) * HY_C, n:]
        y = _dot(jnp.concatenate([xr * kr - xi * ki, xr * ki + xi * kr], axis=1).astype(BF), mi_ref[...])
        gate = zx1 if o == 0 else zx2
        v = [gate[b] * (y[:, b * lc:(b + 1) * lc] + v[b] * hb_ref[o]) for b in range(2)]
    for b in range(2):
        o_ref[b] = v[b]


def _hyena_ctx(p3t, cw, hb, kt, consts):
    batch, _, lc = p3t.shape
    n = 2 * lc
    sec = lambda k: pl.BlockSpec((2, HY_C, lc), lambda p, k=k: (p, k, 0))
    return pl.pallas_call(
        functools.partial(_hy_ctx_kernel, lc=lc),
        grid=(batch // 2,),
        in_specs=[
            sec(0), sec(1), sec(2),
            pl.BlockSpec((3, 4, HY_C, 1), lambda p: (0, 0, 0, 0)),
            pl.BlockSpec((2, HY_C, 1), lambda p: (0, 0, 0)),
            pl.BlockSpec((2 * HY_C, n), lambda p: (0, 0)),
            pl.BlockSpec((n, 2 * n), lambda p: (0, 0)),
            pl.BlockSpec((n, 2 * n), lambda p: (0, 0)),
            pl.BlockSpec((2 * n, n), lambda p: (0, 0)),
        ],
        out_specs=pl.BlockSpec((2, HY_C, lc), lambda p: (p, 0, 0)),
        out_shape=jax.ShapeDtypeStruct((batch, HY_C, lc), F32),
        compiler_params=_cparams(1),
        name="hyena_ctx",
    )(p3t, p3t, p3t, cw, hb, kt, consts["mf"], consts["mff"], consts["mi"])


def _mxu_const(a):
    return jnp.asarray(a, F32).astype(BF)


def _dft_consts_two_stage(n1f):
    n2 = DFT_N2
    n = n1f * n2
    n1h = n1f // 2
    k1 = np.arange(n1f)
    ang1 = 2.0 * np.pi * np.outer(k1, k1) / n1f
    f1r, f1i = np.cos(ang1), -np.sin(ang1)
    m1 = np.block([[f1r[:, :n1h], -f1i[:, :n1h]], [f1i[:, :n1h], f1r[:, :n1h]]])
    m1f = np.concatenate([f1r, f1i], axis=0)
    angt = 2.0 * np.pi * np.outer(k1, np.arange(n2)) / n
    tw = np.stack([np.cos(angt), -np.sin(angt)])
    a2 = 2.0 * np.pi * np.outer(np.arange(n2), np.arange(n2)) / n2
    f2r, f2i = np.cos(a2), -np.sin(a2)
    g2 = np.block([[f2r, f2i], [-f2i, f2r]])
    g2i = np.block([[f2r, -f2i], [f2i, f2r]])
    m1i = np.block([[f1r[:n1h, :], f1i[:n1h, :]], [-f1i[:n1h, :], f1r[:n1h, :]]]) / n
    return {
        "m1": _mxu_const(m1), "m1f": _mxu_const(m1f), "tw": jnp.asarray(tw, F32),
        "g2": _mxu_const(g2), "g2i": _mxu_const(g2i), "m1i": _mxu_const(m1i),
    }


def _dft_consts_single(lc):
    n = 2 * lc
    ang = 2.0 * np.pi * np.outer(np.arange(n), np.arange(n)) / n
    fr, fi = np.cos(ang), -np.sin(ang)
    mf = np.block([[fr[:lc], fi[:lc]], [-fi[:lc], fr[:lc]]])
    mff = np.concatenate([fr, fi], axis=1)
    gr, gi = fr[:, :lc] / n, -fi[:, :lc] / n
    mi = np.block([[gr, gi], [-gi, gr]])
    return {"mf": _mxu_const(mf), "mff": _mxu_const(mff), "mi": _mxu_const(mi)}


def _filter_tables(seq):
    f32 = np.float32
    pos = np.arange(seq, dtype=f32)
    t = pos / f32(max(seq - 1, 1))
    w = f32(2.0 * math.pi) * pos / f32(seq)
    bands = np.linspace(1e-4, HY_BANDS - 1, HY_BANDS, dtype=f32)
    ang = w[:, None] * bands[None, :]
    z = np.concatenate([t[:, None], np.cos(ang), -np.sin(ang)], axis=-1)
    min_decay = abs(math.log(HY_DECAY_TARGET) / HY_SLOW_PCT)
    max_decay = abs(math.log(HY_DECAY_TARGET) / HY_FAST_PCT)
    deltas = np.linspace(min_decay, max_decay, HY_C, dtype=f32)
    window = np.exp(-t[:, None] * deltas[None, :]) + f32(HY_WINDOW_SHIFT)
    idx = np.concatenate([np.arange(seq), np.array([0]), np.arange(seq - 1, 0, -1)])
    et = np.pad(z[idx], ((0, 0), (0, HY_FFN - HY_EMB))).T
    win = window[idx].T
    return {"et": jnp.asarray(et, F32), "win": jnp.asarray(win, F32)}


def _rope_tables(n_lat, n_ctx):
    f32 = np.float32
    n_rows = n_lat // GRID_W
    rows = np.repeat(np.arange(n_rows, dtype=f32), GRID_W)
    cols = np.tile(np.arange(GRID_W, dtype=f32), n_rows)
    quarter = MLA_ROPE // 4
    inv = (f32(1.0) / np.power(f32(ROPE_BASE), np.arange(quarter, dtype=f32) / f32(quarter))).astype(f32)
    ang = np.stack([rows[:, None] * inv, cols[:, None] * inv], axis=1)
    r = np.arange(MLA_ROPE)
    axis, half, i = r // (2 * quarter), (r % (2 * quarter)) // quarter, r % quarter
    cos32 = np.cos(ang)[:, axis, i]
    ssin32 = np.sin(ang)[:, axis, i] * np.where(half == 0, -1.0, 1.0).astype(f32)
    ones = np.ones((n_lat, MLA_NOPE), f32)
    zpad = np.zeros((n_lat, HEAD_PAD - MLA_NOPE - MLA_ROPE), f32)
    z64 = np.zeros((n_lat, MLA_NOPE), f32)
    qs = f32(Q_SCALE)
    lat = {
        "ca": qs * np.concatenate([ones, cos32, zpad], axis=1),
        "sb": qs * np.concatenate([z64, ssin32, zpad], axis=1),
        "ck": np.concatenate([cos32, ssin32, z64], axis=1),
    }
    one_c = np.ones((n_ctx, MLA_NOPE + MLA_ROPE), f32)
    ctx = {
        "ca": qs * np.concatenate([one_c, np.zeros((n_ctx, HEAD_PAD - MLA_NOPE - MLA_ROPE), f32)], axis=1),
        "sb": np.zeros((n_ctx, HEAD_PAD), f32),
        "ck": np.concatenate([np.ones((n_ctx, MLA_ROPE), f32), np.zeros((n_ctx, LANE - MLA_ROPE), f32)], axis=1),
    }
    to_dev = lambda d: {k: jnp.asarray(v, F32) for k, v in d.items()}
    return to_dev(lat), to_dev(ctx)


def _rope_swap():
    quarter = MLA_ROPE // 4
    r = np.arange(MLA_ROPE)
    half = (r % (2 * quarter)) // quarter
    return np.where(half == 0, r + quarter, r - quarter)


def _mix_weights(l, w_in, mla_q_norm, mla_w_uq, mla_kv_norm, mla_w_ukv, sg_ln_g, sg_ln_b, sg_w, sg_b):
    sw = _rope_swap()
    wi = w_in[l]
    kr = wi[:, OFF_KR:N_IN]
    w_all = jnp.concatenate(
        [wi, kr[:, sw], jnp.zeros((D_MODEL, N_IN_PAD - N_IN - MLA_ROPE), F32)], axis=1).astype(BF)
    dq = MLA_NOPE + MLA_ROPE
    uq = mla_w_uq[l].reshape(MLA_Q_RANK, MLA_HEADS, dq)
    pad = HEAD_PAD - dq
    wqa = jnp.pad(uq, ((0, 0), (0, 0), (0, pad)))
    rope_sw = uq[:, :, MLA_NOPE:][:, :, sw]
    wqb = jnp.pad(rope_sw, ((0, 0), (0, 0), (MLA_NOPE, pad)))
    wq = jnp.concatenate([wqa.reshape(MLA_Q_RANK, -1), wqb.reshape(MLA_Q_RANK, -1)], axis=1).astype(BF)
    ukv = mla_w_ukv[l].reshape(MLA_KV_RANK, MLA_HEADS, MLA_NOPE + MLA_V)
    wk = jnp.pad(ukv[:, :, :MLA_NOPE], ((0, 0), (0, 0), (0, HEAD_PAD - MLA_NOPE))).reshape(MLA_KV_RANK, -1)
    wvt = jnp.pad(ukv[:, :, MLA_NOPE:], ((0, 0), (0, 0), (0, V_ROWS - MLA_V))).reshape(MLA_KV_RANK, -1).T
    vone = np.zeros((MLA_HEADS, V_ROWS, 1), np.float32)
    vone[:, MLA_V, 0] = 1.0
    return {
        "w_all": w_all,
        "gq": mla_q_norm[l].reshape(1, -1), "wq": wq,
        "gkv": mla_kv_norm[l].reshape(1, -1), "wk": wk.astype(BF), "wvt": wvt.astype(BF),
        "vone": jnp.asarray(vone.reshape(MLA_HEADS * V_ROWS, 1)),
        "lng": sg_ln_g[l].reshape(1, -1), "lnb": sg_ln_b[l].reshape(1, -1),
        "wsg": sg_w[l].reshape(SG_GROUPS * SG_CHUNK, SG_CHUNK).astype(BF),
        "bsg": jnp.repeat(sg_b[l].T, SG_C // SG_GROUPS, axis=1),
    }


def _filter_weights(l, hy_w1, hy_b1, hy_w2, hy_b2, hy_w3, hy_freq):
    w1t = jnp.pad(hy_w1[l], ((0, HY_FFN - HY_EMB), (0, 0))).T.astype(BF)
    w3 = hy_w3[l].reshape(HY_FFN, 2, 2, HY_C)
    w3t = jnp.transpose(w3, (2, 1, 3, 0)).reshape(2, 2 * HY_C, HY_FFN).astype(BF)
    col = lambda a: a.reshape(HY_FFN, 1)
    return {"w1t": w1t, "b1": col(hy_b1[l]), "f0": col(hy_freq[l, 0]),
            "w2t": hy_w2[l].T.astype(BF), "b2": col(hy_b2[l]), "f1": col(hy_freq[l, 1]), "w3t": w3t}


def kernel(x, c, ctx, c_ctx, ada_w, ada_b, ffn1_wg, ffn1_wu, ffn1_wd, w_in, w_out, hy_conv_w, hy_conv_b, hy_w1, hy_b1, hy_w2, hy_b2, hy_w3, hy_freq, hy_bias, sg_ln_g, sg_ln_b, sg_w, sg_b, mla_q_norm, mla_w_uq, mla_kv_norm, mla_w_ukv, ffn2_wg, ffn2_wu, ffn2_wd, final_norm):
    batch, n_lat, d = x.shape
    n_ctx = ctx.shape[1]
    depth = ada_w.shape[0]
    assert d == D_MODEL and batch % 2 == 0 and batch <= 4
    assert n_lat % (2 * LANE) == 0 and n_ctx % SG_CHUNK == 0
    n1h = n_lat // LANE
    ctx_row = batch

    cc = jnp.concatenate([c, c_ctx[None, :], jnp.zeros((8 - batch - 1, d), F32)], axis=0)
    mods = _modulation(cc, ada_w, ada_b).reshape(depth, 8, N_MOD, d)

    rope_lat, rope_ctx = _rope_tables(n_lat, n_ctx)
    ftab_lat = _filter_tables(n_lat)
    ftab_ctx = _filter_tables(n_ctx)
    dft_lat = _dft_consts_two_stage(2 * n1h)
    dft_ctx = _dft_consts_single(n_ctx)

    xl = x.reshape(batch * n_lat, d)
    xc = ctx.reshape(batch * n_ctx, d)
    tm = 512
    ffn1_bf = [w.astype(BF) for w in (ffn1_wg, ffn1_wu, ffn1_wd)]
    ffn2_bf = [w.astype(BF) for w in (ffn2_wg, ffn2_wu, ffn2_wd)]
    wo = w_out.astype(BF)
    for l in range(depth):
        last = l == depth - 1
        mod_l = mods[l]

        xl = _ffn(xl, mod_l, 0, l, *ffn1_bf, tm=tm, seq=n_lat)
        xc = _ffn(xc, mod_l, 0, l, *ffn1_bf, tm=tm, seq=n_ctx, cond_row=ctx_row)

        mw = _mix_weights(l, w_in, mla_q_norm, mla_w_uq, mla_kv_norm, mla_w_ukv, sg_ln_g, sg_ln_b, sg_w, sg_b)
        p3, ysg, q, k, vt = _mix_in(xl, mod_l, None, n_lat, mw, rope_lat, tm=tm)
        p3c, ysgc, qc, kc, vtc = _mix_in(xc, mod_l, ctx_row, n_ctx, mw, rope_ctx, tm=n_ctx)

        y_attn = _attention(q, [(kc, vtc, n_ctx), (k, vt, n_lat)], batch=batch, nq_seq=n_lat, tq=256,
                            tiles_per_step=1)

        fw = _filter_weights(l, hy_w1, hy_b1, hy_w2, hy_b2, hy_w3, hy_freq)
        cw = jnp.concatenate([hy_conv_w[l], hy_conv_b[l][None, :]], axis=0)
        cw = jnp.transpose(cw.reshape(4, 3, HY_C), (1, 0, 2))

        ktime = _filter_time(ftab_lat, fw, pb=1024)
        hb = hy_bias[l] + ktime[:, n_lat].reshape(2, HY_C)
        kf = _filter_spectrum(ktime.reshape(2 * HY_C, 2 * n1h, LANE), dft_lat, cb=32)
        p3t = jnp.transpose(p3.reshape(batch, n_lat, OFF_SG), (0, 2, 1)).reshape(batch, OFF_SG, n1h, LANE)
        y_hy = _hyena_lat(
            p3t,
            jnp.broadcast_to(cw[:, :, :, None, None], (3, 4, HY_C, 1, LANE)),
            jnp.broadcast_to(hb[:, :, None, None], (2, HY_C, 1, LANE)),
            kf.reshape(2, HY_C * 2 * n1h, 2 * LANE), dft_lat, cb=32, rb=512)
        y_hy = jnp.transpose(y_hy.reshape(batch, HY_C, n_lat), (0, 2, 1)).reshape(batch * n_lat, HY_C)

        xl = _ffn(xl, mod_l, 6, l, *ffn2_bf, tm=tm, seq=n_lat, mixer=(y_hy, ysg, y_attn, wo),
                  final_norm=final_norm if last else None)

        if not last:
            ktime_c = _filter_time(ftab_ctx, fw, pb=n_ctx)
            hbc = hy_bias[l] + ktime_c[:, n_ctx].reshape(2, HY_C)
            p3ct = jnp.transpose(p3c.reshape(batch, n_ctx, OFF_SG), (0, 2, 1))
            yc_hy = _hyena_ctx(p3ct, cw[:, :, :, None], hbc[:, :, None], ktime_c, dft_ctx)
            yc_hy = jnp.transpose(yc_hy, (0, 2, 1)).reshape(batch * n_ctx, HY_C)
            yc_attn = _attention(qc, [(kc, vtc, n_ctx)], batch=batch, nq_seq=n_ctx, tq=n_ctx, tiles_per_step=1)
            xc = _ffn(xc, mod_l, 6, l, *ffn2_bf, tm=tm, seq=n_ctx, cond_row=ctx_row,
                      mixer=(yc_hy, ysgc, yc_attn, wo))
    return xl.reshape(batch, n_lat, d)
```

```python
import functools
import math

import numpy as np
import jax
import jax.numpy as jnp
from jax import lax
from jax.experimental import pallas as pl
from jax.experimental.pallas import tpu as pltpu

F32 = jnp.float32
BF = jnp.bfloat16

D_MODEL = 1024
D_FF = 2816
N_MOD = 9
EPS = 1e-6
GRID_W = 64

HY_C = 256
HY_BANDS = 16
HY_EMB = 1 + 2 * HY_BANDS
HY_FFN = 64
HY_DECAY_TARGET = 1e-2
HY_FAST_PCT = 0.3
HY_SLOW_PCT = 1.5
HY_WINDOW_SHIFT = 0.05

SG_C = 256
SG_GROUPS = 4
SG_CHUNK = 128

MLA_HEADS = 8
MLA_NOPE = 64
MLA_ROPE = 32
MLA_V = 64
MLA_Q_RANK = 384
MLA_KV_RANK = 256
MLA_SCALE = (MLA_NOPE + MLA_ROPE) ** -0.5
Q_SCALE = MLA_SCALE * math.log2(math.e)
ROPE_BASE = 10000.0

OFF_SG = 3 * HY_C
OFF_Q = OFF_SG + 2 * SG_C
OFF_KV = OFF_Q + MLA_Q_RANK
OFF_KR = OFF_KV + MLA_KV_RANK
N_IN = OFF_KR + MLA_ROPE
N_IN_PAD = 2048

LANE = 128
HEAD_PAD = 128
V_ROWS = 80
FFN_CHUNK = 256
VMEM_LIMIT = 56 * 1024 * 1024

DFT_N2 = LANE


def _cparams(n_axes, vmem=VMEM_LIMIT):
    return pltpu.CompilerParams(dimension_semantics=("arbitrary",) * n_axes, vmem_limit_bytes=vmem)


def _mod_spec(d, seq, tm, cond_row):
    if cond_row is None:
        tiles_per_seq = seq // tm
        return pl.BlockSpec((None, N_MOD, d), lambda i: (i // tiles_per_seq, 0, 0))
    return pl.BlockSpec((None, N_MOD, d), lambda i: (cond_row, 0, 0))


def _rms(x):
    return x * lax.rsqrt(jnp.mean(x * x, axis=-1, keepdims=True) + EPS)


def _dot(a, b):
    return jnp.dot(a, b, preferred_element_type=F32)


_NT = (((1,), (1,)), ((), ()))


def _mod_kernel(c_ref, w_ref, b_ref, o_ref):
    c = c_ref[...]
    s = c * jax.nn.sigmoid(c)
    o_ref[0] = _dot(s.astype(BF), w_ref[0].astype(BF)) + b_ref[0]


def _modulation(cc, ada_w, ada_b):
    depth, d, n = ada_w.shape
    tn = 1152
    return pl.pallas_call(
        _mod_kernel,
        grid=(depth, n // tn),
        in_specs=[
            pl.BlockSpec((8, d), lambda l, j: (0, 0)),
            pl.BlockSpec((1, d, tn), lambda l, j: (l, 0, j)),
            pl.BlockSpec((1, 1, tn), lambda l, j: (l, 0, j)),
        ],
        out_specs=pl.BlockSpec((1, 8, tn), lambda l, j: (l, 0, j)),
        out_shape=jax.ShapeDtypeStruct((depth, 8, n), F32),
        compiler_params=_cparams(2),
        name="adaln_mod",
    )(cc, ada_w, ada_b.reshape(depth, 1, n))


def _ffn_kernel(*refs, k0, mixer_out, final):
    refs = list(refs)
    x_ref, mod_ref = refs[:2]
    del refs[:2]
    if mixer_out:
        yh_ref, ys_ref, ya_ref, wo_ref = refs[:4]
        del refs[:4]
    wg_ref, wu_ref, wd_ref = refs[:3]
    del refs[:3]
    if final:
        fn_ref = refs.pop(0)
    o_ref, a_ref = refs
    x = x_ref[...]
    if mixer_out:
        acc = _dot(yh_ref[...].astype(BF), wo_ref[0:HY_C, :])
        acc = acc + _dot(ys_ref[...], wo_ref[HY_C:HY_C + SG_C, :])
        acc = acc + _dot(ya_ref[...], wo_ref[HY_C + SG_C:, :])
        x = x + mod_ref[5:6, :] * acc
    shift = mod_ref[k0:k0 + 1, :]
    scale = mod_ref[k0 + 1:k0 + 2, :]
    gate = mod_ref[k0 + 2:k0 + 3, :]
    hb = (_rms(x) * (1.0 + scale) + shift).astype(BF)
    for j in range(D_FF // FFN_CHUNK):
        cols = slice(j * FFN_CHUNK, (j + 1) * FFN_CHUNK)
        g = _dot(hb, wg_ref[:, cols])
        u = _dot(hb, wu_ref[:, cols])
        a_ref[:, cols] = (g * jax.nn.sigmoid(g) * u).astype(BF)
    y = x + (0.5 * gate) * _dot(a_ref[...], wd_ref[...])
    if final:
        y = _rms(y) * fn_ref[...]
    o_ref[...] = y


def _ffn(x, mod_l, k0, layer, wg, wu, wd, *, tm, seq, cond_row=None, mixer=None, final_norm=None):
    t, d = x.shape
    final = final_norm is not None
    const = lambda i: (0, 0)
    row = lambda i: (i, 0)
    slab = lambda i: (layer, 0, 0)
    in_specs = [pl.BlockSpec((tm, d), row), _mod_spec(d, seq, tm, cond_row)]
    args = [x, mod_l]
    if mixer is not None:
        in_specs += [
            pl.BlockSpec((tm, HY_C), row),
            pl.BlockSpec((tm, SG_C), row),
            pl.BlockSpec((tm, MLA_HEADS * MLA_V), row),
            pl.BlockSpec((None, HY_C + SG_C + MLA_HEADS * MLA_V, d), slab, pipeline_mode=pl.Buffered(1)),
        ]
        args += list(mixer)
    in_specs += [
        pl.BlockSpec((None, d, D_FF), slab, pipeline_mode=pl.Buffered(1)),
        pl.BlockSpec((None, d, D_FF), slab, pipeline_mode=pl.Buffered(1)),
        pl.BlockSpec((None, D_FF, d), slab, pipeline_mode=pl.Buffered(1)),
    ]
    args += [wg, wu, wd]
    if final:
        in_specs.append(pl.BlockSpec((1, d), const))
        args.append(final_norm.reshape(1, d))
    return pl.pallas_call(
        functools.partial(_ffn_kernel, k0=k0, mixer_out=mixer is not None, final=final),
        grid=(t // tm,),
        in_specs=in_specs,
        out_specs=pl.BlockSpec((tm, d), lambda i: (i, 0)),
        out_shape=jax.ShapeDtypeStruct((t, d), F32),
        scratch_shapes=[pltpu.VMEM((tm, D_FF), BF)],
        compiler_params=_cparams(1),
        name="ffn",
    )(*args)


def _gelu(x):
    return 0.5 * x * (1.0 + lax.erf(x * (1.0 / math.sqrt(2.0))))


def _mix_in_kernel(x_ref, mod_ref, wall_ref, ca_ref, sb_ref, ck_ref, gq_ref, wq_ref, gkv_ref, wk_ref, wvt_ref,
                   vone_ref, lng_ref, lnb_ref, wsg_ref, bsg_ref,
                   p3_ref, ysg_ref, q_ref, k_ref, vt_ref, *, tm):
    x = x_ref[...]
    shift = mod_ref[3:4, :]
    scale = mod_ref[4:5, :]
    hb = (_rms(x) * (1.0 + scale) + shift).astype(BF)
    proj = _dot(hb, wall_ref[...])
    p3_ref[...] = proj[:, 0:OFF_SG]

    z = _gelu(proj[:, OFF_SG:OFF_Q])
    u = z[:, 0:SG_C]
    v = z[:, SG_C:2 * SG_C]
    mu = jnp.mean(v, axis=-1, keepdims=True)
    vc = v - mu
    var = jnp.mean(vc * vc, axis=-1, keepdims=True)
    vb = ((vc * lax.rsqrt(var + EPS)) * lng_ref[...] + lnb_ref[...]).astype(BF)
    lane_g = lax.broadcasted_iota(jnp.int32, (SG_CHUNK, SG_C), 1) // (SG_C // SG_GROUPS)
    for j in range(tm // SG_CHUNK):
        rows = slice(j * SG_CHUNK, (j + 1) * SG_CHUNK)
        o = _dot(wsg_ref[...], vb[rows, :])
        s = o[(SG_GROUPS - 1) * SG_CHUNK:, :]
        for g in range(SG_GROUPS - 2, -1, -1):
            s = jnp.where(lane_g == g, o[g * SG_CHUNK:(g + 1) * SG_CHUNK, :], s)
        ysg_ref[rows, :] = (u[rows, :] * (s + bsg_ref[...])).astype(BF)

    cqn = (_rms(proj[:, OFF_Q:OFF_KV]) * gq_ref[...]).astype(BF)
    qq = _dot(cqn, wq_ref[...])
    ca = ca_ref[...]
    sb = sb_ref[...]
    nq = MLA_HEADS * HEAD_PAD
    for h in range(MLA_HEADS):
        cols = slice(h * HEAD_PAD, (h + 1) * HEAD_PAD)
        cols_sw = slice(nq + h * HEAD_PAD, nq + (h + 1) * HEAD_PAD)
        q_ref[:, cols] = (qq[:, cols] * ca + qq[:, cols_sw] * sb).astype(BF)

    kvn = (_rms(proj[:, OFF_KV:OFF_KR]) * gkv_ref[...]).astype(BF)
    vt_ref[...] = (lax.dot_general(wvt_ref[...], kvn, _NT, preferred_element_type=F32) + vone_ref[...]).astype(BF)
    kv = _dot(kvn, wk_ref[...])
    t = proj[:, OFF_KR:OFF_KR + LANE] * ck_ref[...]
    krr = t + pltpu.roll(t, LANE - MLA_ROPE, 1)
    lane = lax.broadcasted_iota(jnp.int32, (tm, LANE), 1)
    in_rope = (lane >= MLA_NOPE) & (lane < MLA_NOPE + MLA_ROPE)
    krp = jnp.where(in_rope, pltpu.roll(krr, MLA_NOPE, 1), 0.0)
    for h in range(MLA_HEADS):
        cols = slice(h * HEAD_PAD, (h + 1) * HEAD_PAD)
        k_ref[:, cols] = (kv[:, cols] + krp).astype(BF)


def _mix_in(x, mod_l, cond_row, seq, w, tabs, *, tm):
    t, d = x.shape
    tiles_per_seq = seq // tm
    const = lambda i: (0, 0)
    tab = lambda i: (i % tiles_per_seq, 0)
    nq = MLA_HEADS * HEAD_PAD
    row = lambda i: (i, 0)
    return pl.pallas_call(
        functools.partial(_mix_in_kernel, tm=tm),
        grid=(t // tm,),
        in_specs=[
            pl.BlockSpec((tm, d), row),
            _mod_spec(d, seq, tm, cond_row),
            pl.BlockSpec((d, N_IN_PAD), const),
            pl.BlockSpec((tm, LANE), tab),
            pl.BlockSpec((tm, LANE), tab),
            pl.BlockSpec((tm, LANE), tab),
            pl.BlockSpec((1, MLA_Q_RANK), const),
            pl.BlockSpec((MLA_Q_RANK, 2 * nq), const),
            pl.BlockSpec((1, MLA_KV_RANK), const),
            pl.BlockSpec((MLA_KV_RANK, nq), const),
            pl.BlockSpec((MLA_HEADS * V_ROWS, MLA_KV_RANK), const),
            pl.BlockSpec((MLA_HEADS * V_ROWS, 1), const),
            pl.BlockSpec((1, SG_C), const),
            pl.BlockSpec((1, SG_C), const),
            pl.BlockSpec((SG_GROUPS * SG_CHUNK, SG_CHUNK), const),
            pl.BlockSpec((SG_CHUNK, SG_C), const),
        ],
        out_specs=[
            pl.BlockSpec((tm, OFF_SG), row),
            pl.BlockSpec((tm, SG_C), row),
            pl.BlockSpec((tm, nq), row),
            pl.BlockSpec((tm, nq), row),
            pl.BlockSpec((MLA_HEADS * V_ROWS, tm), lambda i: (0, i)),
        ],
        out_shape=[
            jax.ShapeDtypeStruct((t, OFF_SG), F32),
            jax.ShapeDtypeStruct((t, SG_C), BF),
            jax.ShapeDtypeStruct((t, nq), BF),
            jax.ShapeDtypeStruct((t, nq), BF),
            jax.ShapeDtypeStruct((MLA_HEADS * V_ROWS, t), BF),
        ],
        compiler_params=_cparams(1),
        name="mix_in",
    )(x, mod_l, w["w_all"], tabs["ca"], tabs["sb"], tabs["ck"], w["gq"], w["wq"], w["gkv"], w["wk"], w["wvt"], w["vone"],
      w["lng"], w["lnb"], w["wsg"], w["bsg"])


KEY_BLOCK = 128


def _attn_kernel(*refs, n_seg, tq):
    q_ref = refs[0]
    segs = [(refs[1 + 2 * s], refs[2 + 2 * s]) for s in range(n_seg)]
    o_ref, p_buf, m_buf = refs[1 + 2 * n_seg:]
    offs = [0]
    for k_ref, _ in segs:
        offs.append(offs[-1] + k_ref.shape[0])
    blocks = [(s, j, min(KEY_BLOCK, offs[s + 1] - offs[s]))
              for s in range(n_seg) for j in range(0, offs[s + 1] - offs[s], KEY_BLOCK)]
    stages = [(t, h) for t in range(q_ref.shape[0] // tq) for h in range(MLA_HEADS)]

    def scores(i):
        t, h = stages[i]
        cols = slice(h * HEAD_PAD, (h + 1) * HEAD_PAD)
        qt = q_ref[t * tq:(t + 1) * tq, cols].astype(F32).T.astype(BF)
        for b, (s, j, n) in enumerate(blocks):
            st = _dot(segs[s][0][j:j + n, cols], qt)
            mb = jnp.max(st, axis=0, keepdims=True)
            m_buf[i % 2, b:b + 1, :] = mb
            p_buf[i % 2, offs[s] + j:offs[s] + j + n, :] = jnp.exp2(st - mb).astype(BF)

    def values(i, prev):
        t, h = stages[i]
        mall = m_buf[i % 2, 0:len(blocks), :]
        w = jnp.exp2(mall - jnp.max(mall, axis=0, keepdims=True))
        acc = None
        for b, (s, j, n) in enumerate(blocks):
            part = _dot(segs[s][1][h * V_ROWS:(h + 1) * V_ROWS, j:j + n],
                        p_buf[i % 2, offs[s] + j:offs[s] + j + n, :]) * w[b:b + 1, :]
            acc = part if acc is None else acc + part
        o = acc[0:MLA_V, :] / acc[MLA_V:MLA_V + 1, :]
        if h % 2 == 0:
            return o
        pair = h // 2
        o_ref[t * tq:(t + 1) * tq, pair * HEAD_PAD:(pair + 1) * HEAD_PAD] = (
            jnp.concatenate([prev, o], axis=0).T.astype(BF))
        return None

    prev = None
    n = len(stages)
    for step in range(n + 1):
        if step < n:
            scores(step)
        if step >= 1:
            prev = values(step - 1, prev)


def _attention(q, segs, *, batch, nq_seq, tq, tiles_per_step):
    t = q.shape[0]
    rows = tq * tiles_per_step
    nqt = nq_seq // rows
    width = MLA_HEADS * HEAD_PAD
    in_specs = [pl.BlockSpec((rows, width), lambda b, i: (b * nqt + i, 0))]
    args = [q]
    nk_total = n_blocks = 0
    for k, vt, nk in segs:
        in_specs += [pl.BlockSpec((nk, width), lambda b, i: (b, 0)),
                     pl.BlockSpec((MLA_HEADS * V_ROWS, nk), lambda b, i: (0, b))]
        args += [k, vt]
        nk_total += nk
        n_blocks += pl.cdiv(nk, KEY_BLOCK)
    assert all(nk % min(KEY_BLOCK, nk) == 0 for _, _, nk in segs)
    return pl.pallas_call(
        functools.partial(_attn_kernel, n_seg=len(segs), tq=tq),
        grid=(batch, nqt),
        in_specs=in_specs,
        out_specs=pl.BlockSpec((rows, MLA_HEADS * MLA_V), lambda b, i: (b * nqt + i, 0)),
        out_shape=jax.ShapeDtypeStruct((t, MLA_HEADS * MLA_V), BF),
        scratch_shapes=[pltpu.VMEM((2, nk_total, tq), BF),
                        pltpu.VMEM((2, 8 * pl.cdiv(n_blocks, 8), tq), F32)],
        compiler_params=_cparams(2),
        name="attn_lat" if len(segs) > 1 else "attn_ctx",
    )(*args)


def _filt_kernel(et_ref, w1_ref, b1_ref, f0_ref, w2_ref, b2_ref, f1_ref, w3_ref, win_ref, o_ref):
    e = et_ref[...].astype(BF)
    h = jnp.sin(f0_ref[...] * (_dot(w1_ref[...], e) + b1_ref[...]))
    h = jnp.sin(f1_ref[...] * (_dot(w2_ref[...], h.astype(BF)) + b2_ref[...]))
    win = win_ref[...]
    o_ref[...] = _dot(w3_ref[0], h.astype(BF)) * jnp.concatenate([win, win], axis=0)


def _filter_time(tabs, fw, *, pb):
    n2l = tabs["et"].shape[1]
    nb = n2l // pb
    const = lambda i: (0, 0)
    col = lambda i: (0, i)
    rows = 2 * HY_C
    return pl.pallas_call(
        _filt_kernel,
        grid=(nb,),
        in_specs=[
            pl.BlockSpec((HY_FFN, pb), col),
            pl.BlockSpec((HY_FFN, HY_FFN), const),
            pl.BlockSpec((HY_FFN, 1), const),
            pl.BlockSpec((HY_FFN, 1), const),
            pl.BlockSpec((HY_FFN, HY_FFN), const),
            pl.BlockSpec((HY_FFN, 1), const),
            pl.BlockSpec((HY_FFN, 1), const),
            pl.BlockSpec((1, rows, HY_FFN), lambda i: (i // (nb // 2), 0, 0)),
            pl.BlockSpec((HY_C, pb), col),
        ],
        out_specs=pl.BlockSpec((rows, pb), col),
        out_shape=jax.ShapeDtypeStruct((rows, n2l), F32),
        compiler_params=_cparams(1),
        name="hyena_filter",
    )(tabs["et"], fw["w1t"], fw["b1"], fw["f0"], fw["w2t"], fw["b2"], fw["f1"], fw["w3t"], tabs["win"])


DFT_CG = 8


def _dft_stage1(m1, rhs, tw_ref, lhs_ref, c0, n1f):
    r = _dot(m1, rhs)
    tr = tw_ref[0]
    ti = tw_ref[1]
    for c in range(DFT_CG):
        br = r[0:n1f, c * LANE:(c + 1) * LANE]
        bi = r[n1f:2 * n1f, c * LANE:(c + 1) * LANE]
        q0 = (c0 + c) * n1f
        lhs_ref[q0:q0 + n1f, 0:LANE] = (br * tr - bi * ti).astype(BF)
        lhs_ref[q0:q0 + n1f, LANE:2 * LANE] = (br * ti + bi * tr).astype(BF)


def _kf_kernel(k_ref, m1_ref, tw_ref, g2_ref, o_ref, lhs_ref, *, cb, n1f):
    zero_at = n1f // 2
    rowi = lax.broadcasted_iota(jnp.int32, (n1f, DFT_CG * LANE), 0)
    lanei = lax.broadcasted_iota(jnp.int32, (n1f, DFT_CG * LANE), 1)
    pad_tap = (rowi == zero_at) & ((lanei & (LANE - 1)) == 0)
    for c0 in range(0, cb, DFT_CG):
        k = jnp.concatenate([k_ref[c0 + c] for c in range(DFT_CG)], axis=1)
        _dft_stage1(m1_ref[...], jnp.where(pad_tap, 0.0, k).astype(BF), tw_ref, lhs_ref, c0, n1f)
    o_ref[...] = _dot(lhs_ref[...], g2_ref[...])


def _filter_spectrum(ktime3, consts, *, cb):
    rows, n1f, _ = ktime3.shape
    const2 = lambda i: (0, 0)
    return pl.pallas_call(
        functools.partial(_kf_kernel, cb=cb, n1f=n1f),
        grid=(rows // cb,),
        in_specs=[
            pl.BlockSpec((cb, n1f, LANE), lambda i: (i, 0, 0)),
            pl.BlockSpec((2 * n1f, n1f), const2),
            pl.BlockSpec((2, n1f, LANE), lambda i: (0, 0, 0)),
            pl.BlockSpec((2 * LANE, 2 * LANE), const2),
        ],
        out_specs=pl.BlockSpec((cb * n1f, 2 * LANE), lambda i: (i, 0)),
        out_shape=jax.ShapeDtypeStruct((rows * n1f, 2 * LANE), F32),
        scratch_shapes=[pltpu.VMEM((cb * n1f, 2 * LANE), BF)],
        compiler_params=_cparams(1),
        name="hyena_filter_spectrum",
    )(ktime3, consts["m1f"], consts["tw"], consts["g2"])


def _hy_lat_kernel(x1_ref, x2_ref, v_ref, cw_ref, hb_ref, kf_ref, m1_ref, tw_ref, g2_ref, g2i_ref, m1i_ref,
                   o_ref, zx1, zx2, zv, vbuf, lhs_ref, c_ref, *, cb, n1h, rb):
    n1f = 2 * n1h
    r = cb * n1h
    rowi = lax.broadcasted_iota(jnp.int32, (r, LANE), 0)
    lanei = lax.broadcasted_iota(jnp.int32, (r, LANE), 1)
    n1i = rowi & (n1h - 1)
    first = (n1i == 0) & (lanei == 0)
    last = (n1i == n1h - 1) & (lanei == LANE - 1)

    def bro(p):
        return jnp.broadcast_to(p, (cb, n1h, LANE)).reshape(r, LANE)

    def sconv(u3, s):
        u = u3.reshape(r, LANE)
        a = pltpu.roll(u, 1, 1)
        prev = jnp.where(lanei == 0, pltpu.roll(a, 1, 0), a)
        prev = jnp.where(first, 0.0, prev)
        f = pltpu.roll(u, LANE - 1, 1)
        nxt = jnp.where(lanei == LANE - 1, pltpu.roll(f, r - 1, 0), f)
        nxt = jnp.where(last, 0.0, nxt)
        return prev * bro(cw_ref[s, 0]) + u * bro(cw_ref[s, 1]) + nxt * bro(cw_ref[s, 2]) + bro(cw_ref[s, 3])

    for b in range(2):
        zx1[b] = sconv(x1_ref[b], 0)
        zx2[b] = sconv(x2_ref[b], 1)
        zv[b] = sconv(v_ref[b], 2)

    cg = DFT_CG
    gr = cg * n1h

    def conv_order(o, vin, gate, write):
        for c0 in range(0, cb, cg):
            zr = vin[0, c0 * n1h:c0 * n1h + gr, :]
            zi = vin[1, c0 * n1h:c0 * n1h + gr, :]
            rhs = jnp.concatenate([
                jnp.concatenate([zr[c * n1h:(c + 1) * n1h] for c in range(cg)], axis=1),
                jnp.concatenate([zi[c * n1h:(c + 1) * n1h] for c in range(cg)], axis=1)], axis=0).astype(BF)
            _dft_stage1(m1_ref[...], rhs, tw_ref, lhs_ref, c0, n1f)

        for q in range(0, cb * n1f, rb):
            x = _dot(lhs_ref[q:q + rb, :], g2_ref[...])
            kf = kf_ref[o, q:q + rb, :]
            xr, xi = x[:, 0:LANE], x[:, LANE:]
            kr, ki = kf[:, 0:LANE], kf[:, LANE:]
            y = jnp.concatenate([xr * kr - xi * ki, xr * ki + xi * kr], axis=1).astype(BF)
            c_ref[q:q + rb, :] = _dot(y, g2i_ref[...])

        tr = tw_ref[0]
        ti = tw_ref[1]
        for c0 in range(0, cb, cg):
            ars, ais = [], []
            for c in range(cg):
                t = c_ref[(c0 + c) * n1f:(c0 + c + 1) * n1f, :]
                cr, ci = t[:, 0:LANE], t[:, LANE:]
                ars.append(cr * tr + ci * ti)
                ais.append(ci * tr - cr * ti)
            rhs = jnp.concatenate([jnp.concatenate(ars, axis=1), jnp.concatenate(ais, axis=1)], axis=0).astype(BF)
            out = _dot(m1i_ref[...], rhs)
            r0 = c0 * n1h
            bias = jnp.broadcast_to(hb_ref[o, c0:c0 + cg], (cg, n1h, LANE)).reshape(gr, LANE)
            for b in range(2):
                ob = out[b * n1h:(b + 1) * n1h, :]
                y = jnp.concatenate([ob[:, c * LANE:(c + 1) * LANE] for c in range(cg)], axis=0)
                write(b, c0, r0, gate[b, r0:r0 + gr, :] * (y + vin[b, r0:r0 + gr, :] * bias))

    def write_mid(b, c0, r0, val):
        vbuf[b, r0:r0 + gr, :] = val

    def write_out(b, c0, r0, val):
        o_ref[b, c0:c0 + cg] = val.reshape(cg, n1h, LANE)

    conv_order(0, zv, zx1, write_mid)
    conv_order(1, vbuf, zx2, write_out)


def _hyena_lat(p3t, cw, hb, kf, consts, *, cb, rb):
    batch, _, n1h, _ = p3t.shape
    n1f = 2 * n1h
    ncb = HY_C // cb
    r = cb * n1h
    const2 = lambda p, c: (0, 0)
    sec = lambda k: pl.BlockSpec((2, cb, n1h, LANE), lambda p, c, k=k: (p, c + k * ncb, 0, 0))
    return pl.pallas_call(
        functools.partial(_hy_lat_kernel, cb=cb, n1h=n1h, rb=rb),
        grid=(batch // 2, ncb),
        in_specs=[
            sec(0), sec(1), sec(2),
            pl.BlockSpec((3, 4, cb, 1, LANE), lambda p, c: (0, 0, c, 0, 0)),
            pl.BlockSpec((2, cb, 1, LANE), lambda p, c: (0, c, 0, 0)),
            pl.BlockSpec((2, cb * n1f, 2 * LANE), lambda p, c: (0, c, 0)),
            pl.BlockSpec((2 * n1f, n1f), const2),
            pl.BlockSpec((2, n1f, LANE), lambda p, c: (0, 0, 0)),
            pl.BlockSpec((2 * LANE, 2 * LANE), const2),
            pl.BlockSpec((2 * LANE, 2 * LANE), const2),
            pl.BlockSpec((n1f, 2 * n1f), const2),
        ],
        out_specs=pl.BlockSpec((2, cb, n1h, LANE), lambda p, c: (p, c, 0, 0)),
        out_shape=jax.ShapeDtypeStruct((batch, HY_C, n1h, LANE), F32),
        scratch_shapes=[
            pltpu.VMEM((2, r, LANE), F32), pltpu.VMEM((2, r, LANE), F32),
            pltpu.VMEM((2, r, LANE), F32), pltpu.VMEM((2, r, LANE), F32),
            pltpu.VMEM((cb * n1f, 2 * LANE), BF), pltpu.VMEM((cb * n1f, 2 * LANE), F32),
        ],
        compiler_params=_cparams(2),
        name="hyena_lat",
    )(p3t, p3t, p3t, cw, hb, kf, consts["m1"], consts["tw"], consts["g2"], consts["g2i"], consts["m1i"])


def _hy_ctx_kernel(x1_ref, x2_ref, v_ref, cw_ref, hb_ref, kt_ref, mf_ref, mff_ref, mi_ref, o_ref, *, lc):
    lanei = lax.broadcasted_iota(jnp.int32, (HY_C, lc), 1)

    def sconv(u, s):
        prev = jnp.where(lanei == 0, 0.0, pltpu.roll(u, 1, 1))
        nxt = jnp.where(lanei == lc - 1, 0.0, pltpu.roll(u, lc - 1, 1))
        return prev * cw_ref[s, 0] + u * cw_ref[s, 1] + nxt * cw_ref[s, 2] + cw_ref[s, 3]

    zx1 = [sconv(x1_ref[b], 0) for b in range(2)]
    zx2 = [sconv(x2_ref[b], 1) for b in range(2)]
    v = [sconv(v_ref[b], 2) for b in range(2)]

    kt = kt_ref[...]
    kl = lax.broadcasted_iota(jnp.int32, kt.shape, 1)
    kf = _dot(jnp.where(kl == lc, 0.0, kt).astype(BF), mff_ref[...])
    n = 2 * lc
    for o in range(2):
        z = jnp.concatenate([v[0], v[1]], axis=1).astype(BF)
        x = _dot(z, mf_ref[...])
        xr, xi = x[:, 0:n], x[:, n:]
        kr = kf[o * HY_C:(o + 1) * HY_C, 0:n]
        ki = kf[o * HY_C:(o + 1) * HY_C, n:]
        y = _dot(jnp.concatenate([xr * kr - xi * ki, xr * ki + xi * kr], axis=1).astype(BF), mi_ref[...])
        gate = zx1 if o == 0 else zx2
        v = [gate[b] * (y[:, b * lc:(b + 1) * lc] + v[b] * hb_ref[o]) for b in range(2)]
    for b in range(2):
        o_ref[b] = v[b]


def _hyena_ctx(p3t, cw, hb, kt, consts):
    batch, _, lc = p3t.shape
    n = 2 * lc
    sec = lambda k: pl.BlockSpec((2, HY_C, lc), lambda p, k=k: (p, k, 0))
    return pl.pallas_call(
        functools.partial(_hy_ctx_kernel, lc=lc),
        grid=(batch // 2,),
        in_specs=[
            sec(0), sec(1), sec(2),
            pl.BlockSpec((3, 4, HY_C, 1), lambda p: (0, 0, 0, 0)),
            pl.BlockSpec((2, HY_C, 1), lambda p: (0, 0, 0)),
            pl.BlockSpec((2 * HY_C, n), lambda p: (0, 0)),
            pl.BlockSpec((n, 2 * n), lambda p: (0, 0)),
            pl.BlockSpec((n, 2 * n), lambda p: (0, 0)),
            pl.BlockSpec((2 * n, n), lambda p: (0, 0)),
        ],
        out_specs=pl.BlockSpec((2, HY_C, lc), lambda p: (p, 0, 0)),
        out_shape=jax.ShapeDtypeStruct((batch, HY_C, lc), F32),
        compiler_params=_cparams(1),
        name="hyena_ctx",
    )(p3t, p3t, p3t, cw, hb, kt, consts["mf"], consts["mff"], consts["mi"])


def _mxu_const(a):
    return jnp.asarray(a, F32).astype(BF)


def _dft_consts_two_stage(n1f):
    n2 = DFT_N2
    n = n1f * n2
    n1h = n1f // 2
    k1 = np.arange(n1f)
    ang1 = 2.0 * np.pi * np.outer(k1, k1) / n1f
    f1r, f1i = np.cos(ang1), -np.sin(ang1)
    m1 = np.block([[f1r[:, :n1h], -f1i[:, :n1h]], [f1i[:, :n1h], f1r[:, :n1h]]])
    m1f = np.concatenate([f1r, f1i], axis=0)
    angt = 2.0 * np.pi * np.outer(k1, np.arange(n2)) / n
    tw = np.stack([np.cos(angt), -np.sin(angt)])
    a2 = 2.0 * np.pi * np.outer(np.arange(n2), np.arange(n2)) / n2
    f2r, f2i = np.cos(a2), -np.sin(a2)
    g2 = np.block([[f2r, f2i], [-f2i, f2r]])
    g2i = np.block([[f2r, -f2i], [f2i, f2r]])
    m1i = np.block([[f1r[:n1h, :], f1i[:n1h, :]], [-f1i[:n1h, :], f1r[:n1h, :]]]) / n
    return {
        "m1": _mxu_const(m1), "m1f": _mxu_const(m1f), "tw": jnp.asarray(tw, F32),
        "g2": _mxu_const(g2), "g2i": _mxu_const(g2i), "m1i": _mxu_const(m1i),
    }


def _dft_consts_single(lc):
    n = 2 * lc
    ang = 2.0 * np.pi * np.outer(np.arange(n), np.arange(n)) / n
    fr, fi = np.cos(ang), -np.sin(ang)
    mf = np.block([[fr[:lc], fi[:lc]], [-fi[:lc], fr[:lc]]])
    mff = np.concatenate([fr, fi], axis=1)
    gr, gi = fr[:, :lc] / n, -fi[:, :lc] / n
    mi = np.block([[gr, gi], [-gi, gr]])
    return {"mf": _mxu_const(mf), "mff": _mxu_const(mff), "mi": _mxu_const(mi)}


def _filter_tables(seq):
    f32 = np.float32
    pos = np.arange(seq, dtype=f32)
    t = pos / f32(max(seq - 1, 1))
    w = f32(2.0 * math.pi) * pos / f32(seq)
    bands = np.linspace(1e-4, HY_BANDS - 1, HY_BANDS, dtype=f32)
    ang = w[:, None] * bands[None, :]
    z = np.concatenate([t[:, None], np.cos(ang), -np.sin(ang)], axis=-1)
    min_decay = abs(math.log(HY_DECAY_TARGET) / HY_SLOW_PCT)
    max_decay = abs(math.log(HY_DECAY_TARGET) / HY_FAST_PCT)
    deltas = np.linspace(min_decay, max_decay, HY_C, dtype=f32)
    window = np.exp(-t[:, None] * deltas[None, :]) + f32(HY_WINDOW_SHIFT)
    idx = np.concatenate([np.arange(seq), np.array([0]), np.arange(seq - 1, 0, -1)])
    et = np.pad(z[idx], ((0, 0), (0, HY_FFN - HY_EMB))).T
    win = window[idx].T
    return {"et": jnp.asarray(et, F32), "win": jnp.asarray(win, F32)}


def _rope_tables(n_lat, n_ctx):
    f32 = np.float32
    n_rows = n_lat // GRID_W
    rows = np.repeat(np.arange(n_rows, dtype=f32), GRID_W)
    cols = np.tile(np.arange(GRID_W, dtype=f32), n_rows)
    quarter = MLA_ROPE // 4
    inv = (f32(1.0) / np.power(f32(ROPE_BASE), np.arange(quarter, dtype=f32) / f32(quarter))).astype(f32)
    ang = np.stack([rows[:, None] * inv, cols[:, None] * inv], axis=1)
    r = np.arange(MLA_ROPE)
    axis, half, i = r // (2 * quarter), (r % (2 * quarter)) // quarter, r % quarter
    cos32 = np.cos(ang)[:, axis, i]
    ssin32 = np.sin(ang)[:, axis, i] * np.where(half == 0, -1.0, 1.0).astype(f32)
    ones = np.ones((n_lat, MLA_NOPE), f32)
    zpad = np.zeros((n_lat, HEAD_PAD - MLA_NOPE - MLA_ROPE), f32)
    z64 = np.zeros((n_lat, MLA_NOPE), f32)
    qs = f32(Q_SCALE)
    lat = {
        "ca": qs * np.concatenate([ones, cos32, zpad], axis=1),
        "sb": qs * np.concatenate([z64, ssin32, zpad], axis=1),
        "ck": np.concatenate([cos32, ssin32, z64], axis=1),
    }
    one_c = np.ones((n_ctx, MLA_NOPE + MLA_ROPE), f32)
    ctx = {
        "ca": qs * np.concatenate([one_c, np.zeros((n_ctx, HEAD_PAD - MLA_NOPE - MLA_ROPE), f32)], axis=1),
        "sb": np.zeros((n_ctx, HEAD_PAD), f32),
        "ck": np.concatenate([np.ones((n_ctx, MLA_ROPE), f32), np.zeros((n_ctx, LANE - MLA_ROPE), f32)], axis=1),
    }
    to_dev = lambda d: {k: jnp.asarray(v, F32) for k, v in d.items()}
    return to_dev(lat), to_dev(ctx)


def _rope_swap():
    quarter = MLA_ROPE // 4
    r = np.arange(MLA_ROPE)
    half = (r % (2 * quarter)) // quarter
    return np.where(half == 0, r + quarter, r - quarter)


def _mix_weights(l, w_in, mla_q_norm, mla_w_uq, mla_kv_norm, mla_w_ukv, sg_ln_g, sg_ln_b, sg_w, sg_b):
    sw = _rope_swap()
    wi = w_in[l]
    kr = wi[:, OFF_KR:N_IN]
    w_all = jnp.concatenate(
        [wi, kr[:, sw], jnp.zeros((D_MODEL, N_IN_PAD - N_IN - MLA_ROPE), F32)], axis=1).astype(BF)
    dq = MLA_NOPE + MLA_ROPE
    uq = mla_w_uq[l].reshape(MLA_Q_RANK, MLA_HEADS, dq)
    pad = HEAD_PAD - dq
    wqa = jnp.pad(uq, ((0, 0), (0, 0), (0, pad)))
    rope_sw = uq[:, :, MLA_NOPE:][:, :, sw]
    wqb = jnp.pad(rope_sw, ((0, 0), (0, 0), (MLA_NOPE, pad)))
    wq = jnp.concatenate([wqa.reshape(MLA_Q_RANK, -1), wqb.reshape(MLA_Q_RANK, -1)], axis=1).astype(BF)
    ukv = mla_w_ukv[l].reshape(MLA_KV_RANK, MLA_HEADS, MLA_NOPE + MLA_V)
    wk = jnp.pad(ukv[:, :, :MLA_NOPE], ((0, 0), (0, 0), (0, HEAD_PAD - MLA_NOPE))).reshape(MLA_KV_RANK, -1)
    wvt = jnp.pad(ukv[:, :, MLA_NOPE:], ((0, 0), (0, 0), (0, V_ROWS - MLA_V))).reshape(MLA_KV_RANK, -1).T
    vone = np.zeros((MLA_HEADS, V_ROWS, 1), np.float32)
    vone[:, MLA_V, 0] = 1.0
    return {
        "w_all": w_all,
        "gq": mla_q_norm[l].reshape(1, -1), "wq": wq,
        "gkv": mla_kv_norm[l].reshape(1, -1), "wk": wk.astype(BF), "wvt": wvt.astype(BF),
        "vone": jnp.asarray(vone.reshape(MLA_HEADS * V_ROWS, 1)),
        "lng": sg_ln_g[l].reshape(1, -1), "lnb": sg_ln_b[l].reshape(1, -1),
        "wsg": sg_w[l].reshape(SG_GROUPS * SG_CHUNK, SG_CHUNK).astype(BF),
        "bsg": jnp.repeat(sg_b[l].T, SG_C // SG_GROUPS, axis=1),
    }


def _filter_weights(l, hy_w1, hy_b1, hy_w2, hy_b2, hy_w3, hy_freq):
    w1t = jnp.pad(hy_w1[l], ((0, HY_FFN - HY_EMB), (0, 0))).T.astype(BF)
    w3 = hy_w3[l].reshape(HY_FFN, 2, 2, HY_C)
    w3t = jnp.transpose(w3, (2, 1, 3, 0)).reshape(2, 2 * HY_C, HY_FFN).astype(BF)
    col = lambda a: a.reshape(HY_FFN, 1)
    return {"w1t": w1t, "b1": col(hy_b1[l]), "f0": col(hy_freq[l, 0]),
            "w2t": hy_w2[l].T.astype(BF), "b2": col(hy_b2[l]), "f1": col(hy_freq[l, 1]), "w3t": w3t}


def kernel(x, c, ctx, c_ctx, ada_w, ada_b, ffn1_wg, ffn1_wu, ffn1_wd, w_in, w_out, hy_conv_w, hy_conv_b, hy_w1, hy_b1, hy_w2, hy_b2, hy_w3, hy_freq, hy_bias, sg_ln_g, sg_ln_b, sg_w, sg_b, mla_q_norm, mla_w_uq, mla_kv_norm, mla_w_ukv, ffn2_wg, ffn2_wu, ffn2_wd, final_norm):
    batch, n_lat, d = x.shape
    n_ctx = ctx.shape[1]
    depth = ada_w.shape[0]
    assert d == D_MODEL and batch % 2 == 0 and batch <= 4
    assert n_lat % (2 * LANE) == 0 and n_ctx % SG_CHUNK == 0
    n1h = n_lat // LANE
    ctx_row = batch

    cc = jnp.concatenate([c, c_ctx[None, :], jnp.zeros((8 - batch - 1, d), F32)], axis=0)
    mods = _modulation(cc, ada_w, ada_b).reshape(depth, 8, N_MOD, d)

    rope_lat, rope_ctx = _rope_tables(n_lat, n_ctx)
    ftab_lat = _filter_tables(n_lat)
    ftab_ctx = _filter_tables(n_ctx)
    dft_lat = _dft_consts_two_stage(2 * n1h)
    dft_ctx = _dft_consts_single(n_ctx)

    xl = x.reshape(batch * n_lat, d)
    xc = ctx.reshape(batch * n_ctx, d)
    tm = 512
    ffn_tm = 1024
    ffn1_bf = [w.astype(BF) for w in (ffn1_wg, ffn1_wu, ffn1_wd)]
    ffn2_bf = [w.astype(BF) for w in (ffn2_wg, ffn2_wu, ffn2_wd)]
    wo = w_out.astype(BF)
    for l in range(depth):
        last = l == depth - 1
        mod_l = mods[l]

        xl = _ffn(xl, mod_l, 0, l, *ffn1_bf, tm=ffn_tm, seq=n_lat)
        xc = _ffn(xc, mod_l, 0, l, *ffn1_bf, tm=ffn_tm, seq=n_ctx, cond_row=ctx_row)

        mw = _mix_weights(l, w_in, mla_q_norm, mla_w_uq, mla_kv_norm, mla_w_ukv, sg_ln_g, sg_ln_b, sg_w, sg_b)
        p3, ysg, q, k, vt = _mix_in(xl, mod_l, None, n_lat, mw, rope_lat, tm=tm)
        p3c, ysgc, qc, kc, vtc = _mix_in(xc, mod_l, ctx_row, n_ctx, mw, rope_ctx, tm=n_ctx)

        y_attn = _attention(q, [(kc, vtc, n_ctx), (k, vt, n_lat)], batch=batch, nq_seq=n_lat, tq=256,
                            tiles_per_step=1)

        fw = _filter_weights(l, hy_w1, hy_b1, hy_w2, hy_b2, hy_w3, hy_freq)
        cw = jnp.concatenate([hy_conv_w[l], hy_conv_b[l][None, :]], axis=0)
        cw = jnp.transpose(cw.reshape(4, 3, HY_C), (1, 0, 2))

        ktime = _filter_time(ftab_lat, fw, pb=1024)
        hb = hy_bias[l] + ktime[:, n_lat].reshape(2, HY_C)
        kf = _filter_spectrum(ktime.reshape(2 * HY_C, 2 * n1h, LANE), dft_lat, cb=32)
        p3t = jnp.transpose(p3.reshape(batch, n_lat, OFF_SG), (0, 2, 1)).reshape(batch, OFF_SG, n1h, LANE)
        y_hy = _hyena_lat(
            p3t,
            jnp.broadcast_to(cw[:, :, :, None, None], (3, 4, HY_C, 1, LANE)),
            jnp.broadcast_to(hb[:, :, None, None], (2, HY_C, 1, LANE)),
            kf.reshape(2, HY_C * 2 * n1h, 2 * LANE), dft_lat, cb=32, rb=512)
        y_hy = jnp.transpose(y_hy.reshape(batch, HY_C, n_lat), (0, 2, 1)).reshape(batch * n_lat, HY_C)

        xl = _ffn(xl, mod_l, 6, l, *ffn2_bf, tm=ffn_tm, seq=n_lat, mixer=(y_hy, ysg, y_attn, wo),
                  final_norm=final_norm if last else None)

        if not last:
            ktime_c = _filter_time(ftab_ctx, fw, pb=n_ctx)
            hbc = hy_bias[l] + ktime_c[:, n_ctx].reshape(2, HY_C)
            p3ct = jnp.transpose(p3c.reshape(batch, n_ctx, OFF_SG), (0, 2, 1))
            yc_hy = _hyena_ctx(p3ct, cw[:, :, :, None], hbc[:, :, None], ktime_c, dft_ctx)
            yc_hy = jnp.transpose(yc_hy, (0, 2, 1)).reshape(batch * n_ctx, HY_C)
            yc_attn = _attention(qc, [(kc, vtc, n_ctx)], batch=batch, nq_seq=n_ctx, tq=n_ctx, tiles_per_step=1)
            xc = _ffn(xc, mod_l, 6, l, *ffn2_bf, tm=ffn_tm, seq=n_ctx, cond_row=ctx_row,
                      mixer=(yc_hy, ysgc, yc_attn, wo))
    return xl.reshape(batch, n_lat, d)
```

```python
import functools
import math

import numpy as np
import jax
import jax.numpy as jnp
from jax import lax
from jax.experimental import pallas as pl
from jax.experimental.pallas import tpu as pltpu

F32 = jnp.float32
BF = jnp.bfloat16

D_MODEL = 1024
D_FF = 2816
N_MOD = 9
EPS = 1e-6
GRID_W = 64

HY_C = 256
HY_BANDS = 16
HY_EMB = 1 + 2 * HY_BANDS
HY_FFN = 64
HY_DECAY_TARGET = 1e-2
HY_FAST_PCT = 0.3
HY_SLOW_PCT = 1.5
HY_WINDOW_SHIFT = 0.05

SG_C = 256
SG_GROUPS = 4
SG_CHUNK = 128

MLA_HEADS = 8
MLA_NOPE = 64
MLA_ROPE = 32
MLA_V = 64
MLA_Q_RANK = 384
MLA_KV_RANK = 256
MLA_SCALE = (MLA_NOPE + MLA_ROPE) ** -0.5
Q_SCALE = MLA_SCALE * math.log2(math.e)
ROPE_BASE = 10000.0

OFF_SG = 3 * HY_C
OFF_Q = OFF_SG + 2 * SG_C
OFF_KV = OFF_Q + MLA_Q_RANK
OFF_KR = OFF_KV + MLA_KV_RANK
N_IN = OFF_KR + MLA_ROPE
N_IN_PAD = 2048

LANE = 128
HEAD_PAD = 128
V_ROWS = 80
FFN_CHUNK = 256
VMEM_LIMIT = 56 * 1024 * 1024

DFT_N2 = LANE


def _cparams(n_axes, vmem=VMEM_LIMIT):
    return pltpu.CompilerParams(dimension_semantics=("arbitrary",) * n_axes, vmem_limit_bytes=vmem)


def _mod_spec(d, seq, tm, cond_row):
    if cond_row is None:
        tiles_per_seq = seq // tm
        return pl.BlockSpec((None, N_MOD, d), lambda i: (i // tiles_per_seq, 0, 0))
    return pl.BlockSpec((None, N_MOD, d), lambda i: (cond_row, 0, 0))


def _rms(x):
    return x * lax.rsqrt(jnp.mean(x * x, axis=-1, keepdims=True) + EPS)


def _dot(a, b):
    return jnp.dot(a, b, preferred_element_type=F32)


_NT = (((1,), (1,)), ((), ()))


def _mod_kernel(c_ref, w_ref, b_ref, o_ref):
    c = c_ref[...]
    s = c * jax.nn.sigmoid(c)
    o_ref[0] = _dot(s.astype(BF), w_ref[0].astype(BF)) + b_ref[0]


def _modulation(cc, ada_w, ada_b):
    depth, d, n = ada_w.shape
    tn = 1152
    return pl.pallas_call(
        _mod_kernel,
        grid=(depth, n // tn),
        in_specs=[
            pl.BlockSpec((8, d), lambda l, j: (0, 0)),
            pl.BlockSpec((1, d, tn), lambda l, j: (l, 0, j)),
            pl.BlockSpec((1, 1, tn), lambda l, j: (l, 0, j)),
        ],
        out_specs=pl.BlockSpec((1, 8, tn), lambda l, j: (l, 0, j)),
        out_shape=jax.ShapeDtypeStruct((depth, 8, n), F32),
        compiler_params=_cparams(2),
        name="adaln_mod",
    )(cc, ada_w, ada_b.reshape(depth, 1, n))


def _ffn_kernel(*refs, k0, mixer_out, final):
    refs = list(refs)
    x_ref, mod_ref = refs[:2]
    del refs[:2]
    if mixer_out:
        yh_ref, ys_ref, ya_ref, wo_ref = refs[:4]
        del refs[:4]
    wg_ref, wu_ref, wd_ref = refs[:3]
    del refs[:3]
    if final:
        fn_ref = refs.pop(0)
    o_ref, a_ref = refs
    x = x_ref[...]
    if mixer_out:
        acc = _dot(yh_ref[...].astype(BF), wo_ref[0:HY_C, :])
        acc = acc + _dot(ys_ref[...], wo_ref[HY_C:HY_C + SG_C, :])
        acc = acc + _dot(ya_ref[...], wo_ref[HY_C + SG_C:, :])
        x = x + mod_ref[5:6, :] * acc
    shift = mod_ref[k0:k0 + 1, :]
    scale = mod_ref[k0 + 1:k0 + 2, :]
    gate = mod_ref[k0 + 2:k0 + 3, :]
    hb = (_rms(x) * (1.0 + scale) + shift).astype(BF)
    for j in range(D_FF // FFN_CHUNK):
        cols = slice(j * FFN_CHUNK, (j + 1) * FFN_CHUNK)
        g = _dot(hb, wg_ref[:, cols])
        u = _dot(hb, wu_ref[:, cols])
        a_ref[:, cols] = (g * jax.nn.sigmoid(g) * u).astype(BF)
    y = x + (0.5 * gate) * _dot(a_ref[...], wd_ref[...])
    if final:
        y = _rms(y) * fn_ref[...]
    o_ref[...] = y


def _ffn(x, mod_l, k0, layer, wg, wu, wd, *, tm, seq, cond_row=None, mixer=None, final_norm=None):
    t, d = x.shape
    final = final_norm is not None
    const = lambda i: (0, 0)
    row = lambda i: (i, 0)
    slab = lambda i: (layer, 0, 0)
    in_specs = [pl.BlockSpec((tm, d), row), _mod_spec(d, seq, tm, cond_row)]
    args = [x, mod_l]
    if mixer is not None:
        in_specs += [
            pl.BlockSpec((tm, HY_C), row),
            pl.BlockSpec((tm, SG_C), row),
            pl.BlockSpec((tm, MLA_HEADS * MLA_V), row),
            pl.BlockSpec((None, HY_C + SG_C + MLA_HEADS * MLA_V, d), slab, pipeline_mode=pl.Buffered(1)),
        ]
        args += list(mixer)
    in_specs += [
        pl.BlockSpec((None, d, D_FF), slab, pipeline_mode=pl.Buffered(1)),
        pl.BlockSpec((None, d, D_FF), slab, pipeline_mode=pl.Buffered(1)),
        pl.BlockSpec((None, D_FF, d), slab, pipeline_mode=pl.Buffered(1)),
    ]
    args += [wg, wu, wd]
    if final:
        in_specs.append(pl.BlockSpec((1, d), const))
        args.append(final_norm.reshape(1, d))
    return pl.pallas_call(
        functools.partial(_ffn_kernel, k0=k0, mixer_out=mixer is not None, final=final),
        grid=(t // tm,),
        in_specs=in_specs,
        out_specs=pl.BlockSpec((tm, d), lambda i: (i, 0)),
        out_shape=jax.ShapeDtypeStruct((t, d), F32),
        scratch_shapes=[pltpu.VMEM((tm, D_FF), BF)],
        compiler_params=_cparams(1),
        name="ffn",
    )(*args)


def _gelu(x):
    return 0.5 * x * (1.0 + lax.erf(x * (1.0 / math.sqrt(2.0))))


def _mix_in_kernel(x_ref, mod_ref, wall_ref, ca_ref, sb_ref, ck_ref, gq_ref, wq_ref, gkv_ref, wk_ref, wvt_ref,
                   vone_ref, lng_ref, lnb_ref, wsg_ref, bsg_ref,
                   p3_ref, ysg_ref, q_ref, k_ref, vt_ref, *, tm):
    x = x_ref[...]
    shift = mod_ref[3:4, :]
    scale = mod_ref[4:5, :]
    hb = (_rms(x) * (1.0 + scale) + shift).astype(BF)
    proj = _dot(hb, wall_ref[...])
    p3_ref[...] = proj[:, 0:OFF_SG]

    z = _gelu(proj[:, OFF_SG:OFF_Q])
    u = z[:, 0:SG_C]
    v = z[:, SG_C:2 * SG_C]
    mu = jnp.mean(v, axis=-1, keepdims=True)
    vc = v - mu
    var = jnp.mean(vc * vc, axis=-1, keepdims=True)
    vb = ((vc * lax.rsqrt(var + EPS)) * lng_ref[...] + lnb_ref[...]).astype(BF)
    lane_g = lax.broadcasted_iota(jnp.int32, (SG_CHUNK, SG_C), 1) // (SG_C // SG_GROUPS)
    for j in range(tm // SG_CHUNK):
        rows = slice(j * SG_CHUNK, (j + 1) * SG_CHUNK)
        o = _dot(wsg_ref[...], vb[rows, :])
        s = o[(SG_GROUPS - 1) * SG_CHUNK:, :]
        for g in range(SG_GROUPS - 2, -1, -1):
            s = jnp.where(lane_g == g, o[g * SG_CHUNK:(g + 1) * SG_CHUNK, :], s)
        ysg_ref[rows, :] = (u[rows, :] * (s + bsg_ref[...])).astype(BF)

    cqn = (_rms(proj[:, OFF_Q:OFF_KV]) * gq_ref[...]).astype(BF)
    qq = _dot(cqn, wq_ref[...])
    ca = ca_ref[...]
    sb = sb_ref[...]
    nq = MLA_HEADS * HEAD_PAD
    for h in range(MLA_HEADS):
        cols = slice(h * HEAD_PAD, (h + 1) * HEAD_PAD)
        cols_sw = slice(nq + h * HEAD_PAD, nq + (h + 1) * HEAD_PAD)
        q_ref[:, cols] = (qq[:, cols] * ca + qq[:, cols_sw] * sb).astype(BF)

    kvn = (_rms(proj[:, OFF_KV:OFF_KR]) * gkv_ref[...]).astype(BF)
    vt_ref[...] = (lax.dot_general(wvt_ref[...], kvn, _NT, preferred_element_type=F32) + vone_ref[...]).astype(BF)
    kv = _dot(kvn, wk_ref[...])
    t = proj[:, OFF_KR:OFF_KR + LANE] * ck_ref[...]
    krr = t + pltpu.roll(t, LANE - MLA_ROPE, 1)
    lane = lax.broadcasted_iota(jnp.int32, (tm, LANE), 1)
    in_rope = (lane >= MLA_NOPE) & (lane < MLA_NOPE + MLA_ROPE)
    krp = jnp.where(in_rope, pltpu.roll(krr, MLA_NOPE, 1), 0.0)
    for h in range(MLA_HEADS):
        cols = slice(h * HEAD_PAD, (h + 1) * HEAD_PAD)
        k_ref[:, cols] = (kv[:, cols] + krp).astype(BF)


def _mix_in(x, mod_l, cond_row, seq, w, layer, vone, tabs, *, tm):
    t, d = x.shape
    tiles_per_seq = seq // tm
    const = lambda i: (0, 0)
    tab = lambda i: (i % tiles_per_seq, 0)
    nq = MLA_HEADS * HEAD_PAD
    row = lambda i: (i, 0)
    wspec = lambda *shape: pl.BlockSpec((None,) + shape, lambda i: (layer, 0, 0))
    return pl.pallas_call(
        functools.partial(_mix_in_kernel, tm=tm),
        grid=(t // tm,),
        in_specs=[
            pl.BlockSpec((tm, d), row),
            _mod_spec(d, seq, tm, cond_row),
            wspec(d, N_IN_PAD),
            pl.BlockSpec((tm, LANE), tab),
            pl.BlockSpec((tm, LANE), tab),
            pl.BlockSpec((tm, LANE), tab),
            wspec(1, MLA_Q_RANK),
            wspec(MLA_Q_RANK, 2 * nq),
            wspec(1, MLA_KV_RANK),
            wspec(MLA_KV_RANK, nq),
            wspec(MLA_HEADS * V_ROWS, MLA_KV_RANK),
            pl.BlockSpec((MLA_HEADS * V_ROWS, 1), const),
            wspec(1, SG_C),
            wspec(1, SG_C),
            wspec(SG_GROUPS * SG_CHUNK, SG_CHUNK),
            wspec(SG_CHUNK, SG_C),
        ],
        out_specs=[
            pl.BlockSpec((tm, OFF_SG), row),
            pl.BlockSpec((tm, SG_C), row),
            pl.BlockSpec((tm, nq), row),
            pl.BlockSpec((tm, nq), row),
            pl.BlockSpec((MLA_HEADS * V_ROWS, tm), lambda i: (0, i)),
        ],
        out_shape=[
            jax.ShapeDtypeStruct((t, OFF_SG), F32),
            jax.ShapeDtypeStruct((t, SG_C), BF),
            jax.ShapeDtypeStruct((t, nq), BF),
            jax.ShapeDtypeStruct((t, nq), BF),
            jax.ShapeDtypeStruct((MLA_HEADS * V_ROWS, t), BF),
        ],
        compiler_params=_cparams(1),
        name="mix_in",
    )(x, mod_l, w["w_all"], tabs["ca"], tabs["sb"], tabs["ck"], w["gq"], w["wq"], w["gkv"], w["wk"], w["wvt"], vone,
      w["lng"], w["lnb"], w["wsg"], w["bsg"])


KEY_BLOCK = 128


def _attn_kernel(*refs, n_seg, tq):
    q_ref = refs[0]
    segs = [(refs[1 + 2 * s], refs[2 + 2 * s]) for s in range(n_seg)]
    o_ref, p_buf, m_buf = refs[1 + 2 * n_seg:]
    offs = [0]
    for k_ref, _ in segs:
        offs.append(offs[-1] + k_ref.shape[0])
    blocks = [(s, j, min(KEY_BLOCK, offs[s + 1] - offs[s]))
              for s in range(n_seg) for j in range(0, offs[s + 1] - offs[s], KEY_BLOCK)]
    stages = [(t, h) for t in range(q_ref.shape[0] // tq) for h in range(MLA_HEADS)]

    def scores(i):
        t, h = stages[i]
        cols = slice(h * HEAD_PAD, (h + 1) * HEAD_PAD)
        qt = q_ref[t * tq:(t + 1) * tq, cols].astype(F32).T.astype(BF)
        for b, (s, j, n) in enumerate(blocks):
            st = _dot(segs[s][0][j:j + n, cols], qt)
            mb = jnp.max(st, axis=0, keepdims=True)
            m_buf[i % 2, b:b + 1, :] = mb
            p_buf[i % 2, offs[s] + j:offs[s] + j + n, :] = jnp.exp2(st - mb).astype(BF)

    def values(i, prev):
        t, h = stages[i]
        mall = m_buf[i % 2, 0:len(blocks), :]
        w = jnp.exp2(mall - jnp.max(mall, axis=0, keepdims=True))
        acc = None
        for b, (s, j, n) in enumerate(blocks):
            part = _dot(segs[s][1][h * V_ROWS:(h + 1) * V_ROWS, j:j + n],
                        p_buf[i % 2, offs[s] + j:offs[s] + j + n, :]) * w[b:b + 1, :]
            acc = part if acc is None else acc + part
        o = acc[0:MLA_V, :] / acc[MLA_V:MLA_V + 1, :]
        if h % 2 == 0:
            return o
        pair = h // 2
        o_ref[t * tq:(t + 1) * tq, pair * HEAD_PAD:(pair + 1) * HEAD_PAD] = (
            jnp.concatenate([prev, o], axis=0).T.astype(BF))
        return None

    prev = None
    n = len(stages)
    for step in range(n + 1):
        if step < n:
            scores(step)
        if step >= 1:
            prev = values(step - 1, prev)


def _attention(q, segs, *, batch, nq_seq, tq, tiles_per_step):
    t = q.shape[0]
    rows = tq * tiles_per_step
    nqt = nq_seq // rows
    width = MLA_HEADS * HEAD_PAD
    in_specs = [pl.BlockSpec((rows, width), lambda b, i: (b * nqt + i, 0))]
    args = [q]
    nk_total = n_blocks = 0
    for k, vt, nk in segs:
        in_specs += [pl.BlockSpec((nk, width), lambda b, i: (b, 0)),
                     pl.BlockSpec((MLA_HEADS * V_ROWS, nk), lambda b, i: (0, b))]
        args += [k, vt]
        nk_total += nk
        n_blocks += pl.cdiv(nk, KEY_BLOCK)
    assert all(nk % min(KEY_BLOCK, nk) == 0 for _, _, nk in segs)
    return pl.pallas_call(
        functools.partial(_attn_kernel, n_seg=len(segs), tq=tq),
        grid=(batch, nqt),
        in_specs=in_specs,
        out_specs=pl.BlockSpec((rows, MLA_HEADS * MLA_V), lambda b, i: (b * nqt + i, 0)),
        out_shape=jax.ShapeDtypeStruct((t, MLA_HEADS * MLA_V), BF),
        scratch_shapes=[pltpu.VMEM((2, nk_total, tq), BF),
                        pltpu.VMEM((2, 8 * pl.cdiv(n_blocks, 8), tq), F32)],
        compiler_params=_cparams(2),
        name="attn_lat" if len(segs) > 1 else "attn_ctx",
    )(*args)


def _filt_kernel(et_ref, w1_ref, b1_ref, f0_ref, w2_ref, b2_ref, f1_ref, w3_ref, win_ref, o_ref):
    e = et_ref[...].astype(BF)
    h = jnp.sin(f0_ref[...] * (_dot(w1_ref[...], e) + b1_ref[...]))
    h = jnp.sin(f1_ref[...] * (_dot(w2_ref[...], h.astype(BF)) + b2_ref[...]))
    win = win_ref[...]
    o_ref[...] = _dot(w3_ref[0], h.astype(BF)) * jnp.concatenate([win, win], axis=0)


def _filter_time(tabs, fw, layer, *, pb):
    n2l = tabs["et"].shape[1]
    nb = n2l // pb
    col = lambda i: (0, i)
    rows = 2 * HY_C
    wspec = lambda *shape: pl.BlockSpec((None,) + shape, lambda i: (layer, 0, 0))
    return pl.pallas_call(
        _filt_kernel,
        grid=(nb,),
        in_specs=[
            pl.BlockSpec((HY_FFN, pb), col),
            wspec(HY_FFN, HY_FFN),
            wspec(HY_FFN, 1),
            wspec(HY_FFN, 1),
            wspec(HY_FFN, HY_FFN),
            wspec(HY_FFN, 1),
            wspec(HY_FFN, 1),
            pl.BlockSpec((None, 1, rows, HY_FFN), lambda i: (layer, i // (nb // 2), 0, 0)),
            pl.BlockSpec((HY_C, pb), col),
        ],
        out_specs=pl.BlockSpec((rows, pb), col),
        out_shape=jax.ShapeDtypeStruct((rows, n2l), F32),
        compiler_params=_cparams(1),
        name="hyena_filter",
    )(tabs["et"], fw["w1t"], fw["b1"], fw["f0"], fw["w2t"], fw["b2"], fw["f1"], fw["w3t"], tabs["win"])


DFT_CG = 8


def _dft_stage1(m1, rhs, tw_ref, lhs_ref, c0, n1f):
    r = _dot(m1, rhs)
    tr = tw_ref[0]
    ti = tw_ref[1]
    for c in range(DFT_CG):
        br = r[0:n1f, c * LANE:(c + 1) * LANE]
        bi = r[n1f:2 * n1f, c * LANE:(c + 1) * LANE]
        q0 = (c0 + c) * n1f
        lhs_ref[q0:q0 + n1f, 0:LANE] = (br * tr - bi * ti).astype(BF)
        lhs_ref[q0:q0 + n1f, LANE:2 * LANE] = (br * ti + bi * tr).astype(BF)


def _kf_kernel(k_ref, m1_ref, tw_ref, g2_ref, o_ref, lhs_ref, *, cb, n1f):
    zero_at = n1f // 2
    rowi = lax.broadcasted_iota(jnp.int32, (n1f, DFT_CG * LANE), 0)
    lanei = lax.broadcasted_iota(jnp.int32, (n1f, DFT_CG * LANE), 1)
    pad_tap = (rowi == zero_at) & ((lanei & (LANE - 1)) == 0)
    for c0 in range(0, cb, DFT_CG):
        k = jnp.concatenate([k_ref[c0 + c] for c in range(DFT_CG)], axis=1)
        _dft_stage1(m1_ref[...], jnp.where(pad_tap, 0.0, k).astype(BF), tw_ref, lhs_ref, c0, n1f)
    o_ref[...] = _dot(lhs_ref[...], g2_ref[...])


def _filter_spectrum(ktime3, consts, *, cb):
    rows, n1f, _ = ktime3.shape
    const2 = lambda i: (0, 0)
    return pl.pallas_call(
        functools.partial(_kf_kernel, cb=cb, n1f=n1f),
        grid=(rows // cb,),
        in_specs=[
            pl.BlockSpec((cb, n1f, LANE), lambda i: (i, 0, 0)),
            pl.BlockSpec((2 * n1f, n1f), const2),
            pl.BlockSpec((2, n1f, LANE), lambda i: (0, 0, 0)),
            pl.BlockSpec((2 * LANE, 2 * LANE), const2),
        ],
        out_specs=pl.BlockSpec((cb * n1f, 2 * LANE), lambda i: (i, 0)),
        out_shape=jax.ShapeDtypeStruct((rows * n1f, 2 * LANE), F32),
        scratch_shapes=[pltpu.VMEM((cb * n1f, 2 * LANE), BF)],
        compiler_params=_cparams(1),
        name="hyena_filter_spectrum",
    )(ktime3, consts["m1f"], consts["tw"], consts["g2"])


def _hy_lat_kernel(x1_ref, x2_ref, v_ref, cw_ref, hb_ref, kf_ref, m1_ref, tw_ref, g2_ref, g2i_ref, m1i_ref,
                   o_ref, zx1, zx2, zv, vbuf, lhs_ref, c_ref, *, cb, n1h, rb):
    n1f = 2 * n1h
    r = cb * n1h
    rowi = lax.broadcasted_iota(jnp.int32, (r, LANE), 0)
    lanei = lax.broadcasted_iota(jnp.int32, (r, LANE), 1)
    n1i = rowi & (n1h - 1)
    first = (n1i == 0) & (lanei == 0)
    last = (n1i == n1h - 1) & (lanei == LANE - 1)

    def bro(p):
        return jnp.broadcast_to(p, (cb, n1h, LANE)).reshape(r, LANE)

    def sconv(u3, s):
        u = u3.reshape(r, LANE)
        a = pltpu.roll(u, 1, 1)
        prev = jnp.where(lanei == 0, pltpu.roll(a, 1, 0), a)
        prev = jnp.where(first, 0.0, prev)
        f = pltpu.roll(u, LANE - 1, 1)
        nxt = jnp.where(lanei == LANE - 1, pltpu.roll(f, r - 1, 0), f)
        nxt = jnp.where(last, 0.0, nxt)
        return prev * bro(cw_ref[s, 0]) + u * bro(cw_ref[s, 1]) + nxt * bro(cw_ref[s, 2]) + bro(cw_ref[s, 3])

    for b in range(2):
        zx1[b] = sconv(x1_ref[b], 0)
        zx2[b] = sconv(x2_ref[b], 1)
        zv[b] = sconv(v_ref[b], 2)

    cg = DFT_CG
    gr = cg * n1h

    def conv_order(o, vin, gate, write):
        for c0 in range(0, cb, cg):
            zr = vin[0, c0 * n1h:c0 * n1h + gr, :]
            zi = vin[1, c0 * n1h:c0 * n1h + gr, :]
            rhs = jnp.concatenate([
                jnp.concatenate([zr[c * n1h:(c + 1) * n1h] for c in range(cg)], axis=1),
                jnp.concatenate([zi[c * n1h:(c + 1) * n1h] for c in range(cg)], axis=1)], axis=0).astype(BF)
            _dft_stage1(m1_ref[...], rhs, tw_ref, lhs_ref, c0, n1f)

        for q in range(0, cb * n1f, rb):
            x = _dot(lhs_ref[q:q + rb, :], g2_ref[...])
            kf = kf_ref[o, q:q + rb, :]
            xr, xi = x[:, 0:LANE], x[:, LANE:]
            kr, ki = kf[:, 0:LANE], kf[:, LANE:]
            y = jnp.concatenate([xr * kr - xi * ki, xr * ki + xi * kr], axis=1).astype(BF)
            c_ref[q:q + rb, :] = _dot(y, g2i_ref[...])

        tr = tw_ref[0]
        ti = tw_ref[1]
        for c0 in range(0, cb, cg):
            ars, ais = [], []
            for c in range(cg):
                t = c_ref[(c0 + c) * n1f:(c0 + c + 1) * n1f, :]
                cr, ci = t[:, 0:LANE], t[:, LANE:]
                ars.append(cr * tr + ci * ti)
                ais.append(ci * tr - cr * ti)
            rhs = jnp.concatenate([jnp.concatenate(ars, axis=1), jnp.concatenate(ais, axis=1)], axis=0).astype(BF)
            out = _dot(m1i_ref[...], rhs)
            r0 = c0 * n1h
            bias = jnp.broadcast_to(hb_ref[o, c0:c0 + cg], (cg, n1h, LANE)).reshape(gr, LANE)
            for b in range(2):
                ob = out[b * n1h:(b + 1) * n1h, :]
                y = jnp.concatenate([ob[:, c * LANE:(c + 1) * LANE] for c in range(cg)], axis=0)
                write(b, c0, r0, gate[b, r0:r0 + gr, :] * (y + vin[b, r0:r0 + gr, :] * bias))

    def write_mid(b, c0, r0, val):
        vbuf[b, r0:r0 + gr, :] = val

    def write_out(b, c0, r0, val):
        o_ref[b, c0:c0 + cg] = val.reshape(cg, n1h, LANE).astype(o_ref.dtype)

    conv_order(0, zv, zx1, write_mid)
    conv_order(1, vbuf, zx2, write_out)


def _hyena_lat(p3t, cw, hb, kf, consts, *, cb, rb):
    batch, _, n1h, _ = p3t.shape
    n1f = 2 * n1h
    ncb = HY_C // cb
    r = cb * n1h
    const2 = lambda p, c: (0, 0)
    sec = lambda k: pl.BlockSpec((2, cb, n1h, LANE), lambda p, c, k=k: (p, c + k * ncb, 0, 0))
    return pl.pallas_call(
        functools.partial(_hy_lat_kernel, cb=cb, n1h=n1h, rb=rb),
        grid=(batch // 2, ncb),
        in_specs=[
            sec(0), sec(1), sec(2),
            pl.BlockSpec((3, 4, cb, 1, LANE), lambda p, c: (0, 0, c, 0, 0)),
            pl.BlockSpec((2, cb, 1, LANE), lambda p, c: (0, c, 0, 0)),
            pl.BlockSpec((2, cb * n1f, 2 * LANE), lambda p, c: (0, c, 0)),
            pl.BlockSpec((2 * n1f, n1f), const2),
            pl.BlockSpec((2, n1f, LANE), lambda p, c: (0, 0, 0)),
            pl.BlockSpec((2 * LANE, 2 * LANE), const2),
            pl.BlockSpec((2 * LANE, 2 * LANE), const2),
            pl.BlockSpec((n1f, 2 * n1f), const2),
        ],
        out_specs=pl.BlockSpec((2, cb, n1h, LANE), lambda p, c: (p, c, 0, 0)),
        out_shape=jax.ShapeDtypeStruct((batch, HY_C, n1h, LANE), BF),
        scratch_shapes=[
            pltpu.VMEM((2, r, LANE), F32), pltpu.VMEM((2, r, LANE), F32),
            pltpu.VMEM((2, r, LANE), F32), pltpu.VMEM((2, r, LANE), F32),
            pltpu.VMEM((cb * n1f, 2 * LANE), BF), pltpu.VMEM((cb * n1f, 2 * LANE), F32),
        ],
        compiler_params=_cparams(2),
        name="hyena_lat",
    )(p3t, p3t, p3t, cw, hb, kf, consts["m1"], consts["tw"], consts["g2"], consts["g2i"], consts["m1i"])


def _hy_ctx_kernel(x1_ref, x2_ref, v_ref, cw_ref, hb_ref, kt_ref, mf_ref, mff_ref, mi_ref, o_ref, *, lc):
    lanei = lax.broadcasted_iota(jnp.int32, (HY_C, lc), 1)

    def sconv(u, s):
        prev = jnp.where(lanei == 0, 0.0, pltpu.roll(u, 1, 1))
        nxt = jnp.where(lanei == lc - 1, 0.0, pltpu.roll(u, lc - 1, 1))
        return prev * cw_ref[s, 0] + u * cw_ref[s, 1] + nxt * cw_ref[s, 2] + cw_ref[s, 3]

    zx1 = [sconv(x1_ref[b], 0) for b in range(2)]
    zx2 = [sconv(x2_ref[b], 1) for b in range(2)]
    v = [sconv(v_ref[b], 2) for b in range(2)]

    kt = kt_ref[...]
    kl = lax.broadcasted_iota(jnp.int32, kt.shape, 1)
    kf = _dot(jnp.where(kl == lc, 0.0, kt).astype(BF), mff_ref[...])
    n = 2 * lc
    for o in range(2):
        z = jnp.concatenate([v[0], v[1]], axis=1).astype(BF)
        x = _dot(z, mf_ref[...])
        xr, xi = x[:, 0:n], x[:, n:]
        kr = kf[o * HY_C:(o + 1) * HY_C, 0:n]
        ki = kf[o * HY_C:(o + 1) * HY_C, n:]
        y = _dot(jnp.concatenate([xr * kr - xi * ki, xr * ki + xi * kr], axis=1).astype(BF), mi_ref[...])
        gate = zx1 if o == 0 else zx2
        v = [gate[b] * (y[:, b * lc:(b + 1) * lc] + v[b] * hb_ref[o]) for b in range(2)]
    for b in range(2):
        o_ref[b] = v[b]


def _hyena_ctx(p3t, cw, hb, kt, consts):
    batch, _, lc = p3t.shape
    n = 2 * lc
    sec = lambda k: pl.BlockSpec((2, HY_C, lc), lambda p, k=k: (p, k, 0))
    return pl.pallas_call(
        functools.partial(_hy_ctx_kernel, lc=lc),
        grid=(batch // 2,),
        in_specs=[
            sec(0), sec(1), sec(2),
            pl.BlockSpec((3, 4, HY_C, 1), lambda p: (0, 0, 0, 0)),
            pl.BlockSpec((2, HY_C, 1), lambda p: (0, 0, 0)),
            pl.BlockSpec((2 * HY_C, n), lambda p: (0, 0)),
            pl.BlockSpec((n, 2 * n), lambda p: (0, 0)),
            pl.BlockSpec((n, 2 * n), lambda p: (0, 0)),
            pl.BlockSpec((2 * n, n), lambda p: (0, 0)),
        ],
        out_specs=pl.BlockSpec((2, HY_C, lc), lambda p: (p, 0, 0)),
        out_shape=jax.ShapeDtypeStruct((batch, HY_C, lc), F32),
        compiler_params=_cparams(1),
        name="hyena_ctx",
    )(p3t, p3t, p3t, cw, hb, kt, consts["mf"], consts["mff"], consts["mi"])


def _mxu_const(a):
    return jnp.asarray(a, F32).astype(BF)


def _dft_consts_two_stage(n1f):
    n2 = DFT_N2
    n = n1f * n2
    n1h = n1f // 2
    k1 = np.arange(n1f)
    ang1 = 2.0 * np.pi * np.outer(k1, k1) / n1f
    f1r, f1i = np.cos(ang1), -np.sin(ang1)
    m1 = np.block([[f1r[:, :n1h], -f1i[:, :n1h]], [f1i[:, :n1h], f1r[:, :n1h]]])
    m1f = np.concatenate([f1r, f1i], axis=0)
    angt = 2.0 * np.pi * np.outer(k1, np.arange(n2)) / n
    tw = np.stack([np.cos(angt), -np.sin(angt)])
    a2 = 2.0 * np.pi * np.outer(np.arange(n2), np.arange(n2)) / n2
    f2r, f2i = np.cos(a2), -np.sin(a2)
    g2 = np.block([[f2r, f2i], [-f2i, f2r]])
    g2i = np.block([[f2r, -f2i], [f2i, f2r]])
    m1i = np.block([[f1r[:n1h, :], f1i[:n1h, :]], [-f1i[:n1h, :], f1r[:n1h, :]]]) / n
    return {
        "m1": _mxu_const(m1), "m1f": _mxu_const(m1f), "tw": jnp.asarray(tw, F32),
        "g2": _mxu_const(g2), "g2i": _mxu_const(g2i), "m1i": _mxu_const(m1i),
    }


def _dft_consts_single(lc):
    n = 2 * lc
    ang = 2.0 * np.pi * np.outer(np.arange(n), np.arange(n)) / n
    fr, fi = np.cos(ang), -np.sin(ang)
    mf = np.block([[fr[:lc], fi[:lc]], [-fi[:lc], fr[:lc]]])
    mff = np.concatenate([fr, fi], axis=1)
    gr, gi = fr[:, :lc] / n, -fi[:, :lc] / n
    mi = np.block([[gr, gi], [-gi, gr]])
    return {"mf": _mxu_const(mf), "mff": _mxu_const(mff), "mi": _mxu_const(mi)}


def _filter_tables(seq):
    f32 = np.float32
    pos = np.arange(seq, dtype=f32)
    t = pos / f32(max(seq - 1, 1))
    w = f32(2.0 * math.pi) * pos / f32(seq)
    bands = np.linspace(1e-4, HY_BANDS - 1, HY_BANDS, dtype=f32)
    ang = w[:, None] * bands[None, :]
    z = np.concatenate([t[:, None], np.cos(ang), -np.sin(ang)], axis=-1)
    min_decay = abs(math.log(HY_DECAY_TARGET) / HY_SLOW_PCT)
    max_decay = abs(math.log(HY_DECAY_TARGET) / HY_FAST_PCT)
    deltas = np.linspace(min_decay, max_decay, HY_C, dtype=f32)
    window = np.exp(-t[:, None] * deltas[None, :]) + f32(HY_WINDOW_SHIFT)
    idx = np.concatenate([np.arange(seq), np.array([0]), np.arange(seq - 1, 0, -1)])
    et = np.pad(z[idx], ((0, 0), (0, HY_FFN - HY_EMB))).T
    win = window[idx].T
    return {"et": jnp.asarray(et, F32), "win": jnp.asarray(win, F32)}


def _rope_tables(n_lat, n_ctx):
    f32 = np.float32
    n_rows = n_lat // GRID_W
    rows = np.repeat(np.arange(n_rows, dtype=f32), GRID_W)
    cols = np.tile(np.arange(GRID_W, dtype=f32), n_rows)
    quarter = MLA_ROPE // 4
    inv = (f32(1.0) / np.power(f32(ROPE_BASE), np.arange(quarter, dtype=f32) / f32(quarter))).astype(f32)
    ang = np.stack([rows[:, None] * inv, cols[:, None] * inv], axis=1)
    r = np.arange(MLA_ROPE)
    axis, half, i = r // (2 * quarter), (r % (2 * quarter)) // quarter, r % quarter
    cos32 = np.cos(ang)[:, axis, i]
    ssin32 = np.sin(ang)[:, axis, i] * np.where(half == 0, -1.0, 1.0).astype(f32)
    ones = np.ones((n_lat, MLA_NOPE), f32)
    zpad = np.zeros((n_lat, HEAD_PAD - MLA_NOPE - MLA_ROPE), f32)
    z64 = np.zeros((n_lat, MLA_NOPE), f32)
    qs = f32(Q_SCALE)
    lat = {
        "ca": qs * np.concatenate([ones, cos32, zpad], axis=1),
        "sb": qs * np.concatenate([z64, ssin32, zpad], axis=1),
        "ck": np.concatenate([cos32, ssin32, z64], axis=1),
    }
    one_c = np.ones((n_ctx, MLA_NOPE + MLA_ROPE), f32)
    ctx = {
        "ca": qs * np.concatenate([one_c, np.zeros((n_ctx, HEAD_PAD - MLA_NOPE - MLA_ROPE), f32)], axis=1),
        "sb": np.zeros((n_ctx, HEAD_PAD), f32),
        "ck": np.concatenate([np.ones((n_ctx, MLA_ROPE), f32), np.zeros((n_ctx, LANE - MLA_ROPE), f32)], axis=1),
    }
    to_dev = lambda d: {k: jnp.asarray(v, F32) for k, v in d.items()}
    return to_dev(lat), to_dev(ctx)


def _rope_swap():
    quarter = MLA_ROPE // 4
    r = np.arange(MLA_ROPE)
    half = (r % (2 * quarter)) // quarter
    return np.where(half == 0, r + quarter, r - quarter)


def _mix_weights(w_in, mla_q_norm, mla_w_uq, mla_kv_norm, mla_w_ukv, sg_ln_g, sg_ln_b, sg_w, sg_b):
    return jax.vmap(_mix_weights_layer)(w_in, mla_q_norm, mla_w_uq, mla_kv_norm, mla_w_ukv, sg_ln_g, sg_ln_b, sg_w, sg_b)


def _mix_weights_layer(wi, q_norm, w_uq, kv_norm, w_ukv, ln_g, ln_b, w_s, b_s):
    sw = _rope_swap()
    kr = wi[:, OFF_KR:N_IN]
    w_all = jnp.concatenate(
        [wi, kr[:, sw], jnp.zeros((D_MODEL, N_IN_PAD - N_IN - MLA_ROPE), F32)], axis=1).astype(BF)
    dq = MLA_NOPE + MLA_ROPE
    uq = w_uq.reshape(MLA_Q_RANK, MLA_HEADS, dq)
    pad = HEAD_PAD - dq
    wqa = jnp.pad(uq, ((0, 0), (0, 0), (0, pad)))
    rope_sw = uq[:, :, MLA_NOPE:][:, :, sw]
    wqb = jnp.pad(rope_sw, ((0, 0), (0, 0), (MLA_NOPE, pad)))
    wq = jnp.concatenate([wqa.reshape(MLA_Q_RANK, -1), wqb.reshape(MLA_Q_RANK, -1)], axis=1).astype(BF)
    ukv = w_ukv.reshape(MLA_KV_RANK, MLA_HEADS, MLA_NOPE + MLA_V)
    wk = jnp.pad(ukv[:, :, :MLA_NOPE], ((0, 0), (0, 0), (0, HEAD_PAD - MLA_NOPE))).reshape(MLA_KV_RANK, -1)
    wvt = jnp.pad(ukv[:, :, MLA_NOPE:], ((0, 0), (0, 0), (0, V_ROWS - MLA_V))).reshape(MLA_KV_RANK, -1).T
    return {
        "w_all": w_all,
        "gq": q_norm.reshape(1, -1), "wq": wq,
        "gkv": kv_norm.reshape(1, -1), "wk": wk.astype(BF), "wvt": wvt.astype(BF),
        "lng": ln_g.reshape(1, -1), "lnb": ln_b.reshape(1, -1),
        "wsg": w_s.reshape(SG_GROUPS * SG_CHUNK, SG_CHUNK).astype(BF),
        "bsg": jnp.repeat(b_s.T, SG_C // SG_GROUPS, axis=1),
    }


def _value_ones_rows():
    vone = np.zeros((MLA_HEADS, V_ROWS, 1), np.float32)
    vone[:, MLA_V, 0] = 1.0
    return jnp.asarray(vone.reshape(MLA_HEADS * V_ROWS, 1))


def _filter_weights(hy_w1, hy_b1, hy_w2, hy_b2, hy_w3, hy_freq):
    depth = hy_w1.shape[0]
    w1t = jnp.swapaxes(jnp.pad(hy_w1, ((0, 0), (0, HY_FFN - HY_EMB), (0, 0))), 1, 2).astype(BF)
    w3 = hy_w3.reshape(depth, HY_FFN, 2, 2, HY_C)
    w3t = jnp.transpose(w3, (0, 3, 2, 4, 1)).reshape(depth, 2, 2 * HY_C, HY_FFN).astype(BF)
    col = lambda a: a.reshape(depth, HY_FFN, 1)
    return {"w1t": w1t, "b1": col(hy_b1), "f0": col(hy_freq[:, 0]),
            "w2t": jnp.swapaxes(hy_w2, 1, 2).astype(BF), "b2": col(hy_b2), "f1": col(hy_freq[:, 1]), "w3t": w3t}


def kernel(x, c, ctx, c_ctx, ada_w, ada_b, ffn1_wg, ffn1_wu, ffn1_wd, w_in, w_out, hy_conv_w, hy_conv_b, hy_w1, hy_b1, hy_w2, hy_b2, hy_w3, hy_freq, hy_bias, sg_ln_g, sg_ln_b, sg_w, sg_b, mla_q_norm, mla_w_uq, mla_kv_norm, mla_w_ukv, ffn2_wg, ffn2_wu, ffn2_wd, final_norm):
    batch, n_lat, d = x.shape
    n_ctx = ctx.shape[1]
    depth = ada_w.shape[0]
    assert d == D_MODEL and batch % 2 == 0 and batch <= 4
    assert n_lat % (2 * LANE) == 0 and n_ctx % SG_CHUNK == 0
    n1h = n_lat // LANE
    ctx_row = batch

    cc = jnp.concatenate([c, c_ctx[None, :], jnp.zeros((8 - batch - 1, d), F32)], axis=0)
    mods = _modulation(cc, ada_w, ada_b).reshape(depth, 8, N_MOD, d)

    rope_lat, rope_ctx = _rope_tables(n_lat, n_ctx)
    ftab_lat = _filter_tables(n_lat)
    ftab_ctx = _filter_tables(n_ctx)
    dft_lat = _dft_consts_two_stage(2 * n1h)
    dft_ctx = _dft_consts_single(n_ctx)

    xl = x.reshape(batch * n_lat, d)
    xc = ctx.reshape(batch * n_ctx, d)
    tm = 512
    ffn_tm = 1024
    ffn1_bf = [w.astype(BF) for w in (ffn1_wg, ffn1_wu, ffn1_wd)]
    ffn2_bf = [w.astype(BF) for w in (ffn2_wg, ffn2_wu, ffn2_wd)]
    wo = w_out.astype(BF)
    mw = _mix_weights(w_in, mla_q_norm, mla_w_uq, mla_kv_norm, mla_w_ukv, sg_ln_g, sg_ln_b, sg_w, sg_b)
    vone = _value_ones_rows()
    fw = _filter_weights(hy_w1, hy_b1, hy_w2, hy_b2, hy_w3, hy_freq)
    cw_all = jnp.concatenate([hy_conv_w, hy_conv_b[:, None, :]], axis=1)
    cw_all = jnp.transpose(cw_all.reshape(depth, 4, 3, HY_C), (0, 2, 1, 3))
    for l in range(depth):
        last = l == depth - 1
        mod_l = mods[l]

        xl = _ffn(xl, mod_l, 0, l, *ffn1_bf, tm=ffn_tm, seq=n_lat)
        xc = _ffn(xc, mod_l, 0, l, *ffn1_bf, tm=ffn_tm, seq=n_ctx, cond_row=ctx_row)

        p3, ysg, q, k, vt = _mix_in(xl, mod_l, None, n_lat, mw, l, vone, rope_lat, tm=tm)
        p3c, ysgc, qc, kc, vtc = _mix_in(xc, mod_l, ctx_row, n_ctx, mw, l, vone, rope_ctx, tm=n_ctx)

        y_attn = _attention(q, [(kc, vtc, n_ctx), (k, vt, n_lat)], batch=batch, nq_seq=n_lat, tq=256,
                            tiles_per_step=1)

        cw = cw_all[l]
        ktime = _filter_time(ftab_lat, fw, l, pb=1024)
        hb = hy_bias[l] + ktime[:, n_lat].reshape(2, HY_C)
        kf = _filter_spectrum(ktime.reshape(2 * HY_C, 2 * n1h, LANE), dft_lat, cb=32)
        p3t = jnp.transpose(p3.reshape(batch, n_lat, OFF_SG), (0, 2, 1)).reshape(batch, OFF_SG, n1h, LANE)
        y_hy = _hyena_lat(
            p3t,
            jnp.broadcast_to(cw[:, :, :, None, None], (3, 4, HY_C, 1, LANE)),
            jnp.broadcast_to(hb[:, :, None, None], (2, HY_C, 1, LANE)),
            kf.reshape(2, HY_C * 2 * n1h, 2 * LANE), dft_lat, cb=32, rb=512)
        y_hy = jnp.transpose(y_hy.reshape(batch, HY_C, n_lat), (0, 2, 1)).reshape(batch * n_lat, HY_C)

        xl = _ffn(xl, mod_l, 6, l, *ffn2_bf, tm=ffn_tm, seq=n_lat, mixer=(y_hy, ysg, y_attn, wo),
                  final_norm=final_norm if last else None)

        if not last:
            ktime_c = _filter_time(ftab_ctx, fw, l, pb=n_ctx)
            hbc = hy_bias[l] + ktime_c[:, n_ctx].reshape(2, HY_C)
            p3ct = jnp.transpose(p3c.reshape(batch, n_ctx, OFF_SG), (0, 2, 1))
            yc_hy = _hyena_ctx(p3ct, cw[:, :, :, None], hbc[:, :, None], ktime_c, dft_ctx)
            yc_hy = jnp.transpose(yc_hy, (0, 2, 1)).reshape(batch * n_ctx, HY_C)
            yc_attn = _attention(qc, [(kc, vtc, n_ctx)], batch=batch, nq_seq=n_ctx, tq=n_ctx, tiles_per_step=1)
            xc = _ffn(xc, mod_l, 6, l, *ffn2_bf, tm=ffn_tm, seq=n_ctx, cond_row=ctx_row,
                      mixer=(yc_hy, ysgc, yc_attn, wo))
    return xl.reshape(batch, n_lat, d)
```

```python
import functools
import math

import numpy as np
import jax
import jax.numpy as jnp
from jax import lax
from jax.experimental import pallas as pl
from jax.experimental.pallas import tpu as pltpu

F32 = jnp.float32
BF = jnp.bfloat16

D_MODEL = 1024
D_FF = 2816
N_MOD = 9
EPS = 1e-6
GRID_W = 64

HY_C = 256
HY_BANDS = 16
HY_EMB = 1 + 2 * HY_BANDS
HY_FFN = 64
HY_DECAY_TARGET = 1e-2
HY_FAST_PCT = 0.3
HY_SLOW_PCT = 1.5
HY_WINDOW_SHIFT = 0.05

SG_C = 256
SG_GROUPS = 4
SG_CHUNK = 128

MLA_HEADS = 8
MLA_NOPE = 64
MLA_ROPE = 32
MLA_V = 64
MLA_Q_RANK = 384
MLA_KV_RANK = 256
MLA_SCALE = (MLA_NOPE + MLA_ROPE) ** -0.5
Q_SCALE = MLA_SCALE * math.log2(math.e)
ROPE_BASE = 10000.0

OFF_SG = 3 * HY_C
OFF_Q = OFF_SG + 2 * SG_C
OFF_KV = OFF_Q + MLA_Q_RANK
OFF_KR = OFF_KV + MLA_KV_RANK
N_IN = OFF_KR + MLA_ROPE
N_IN_PAD = 2048

LANE = 128
HEAD_PAD = 128
V_ROWS = 80
FFN_CHUNK = 256
VMEM_LIMIT = 56 * 1024 * 1024

DFT_N2 = LANE


def _cparams(n_axes, vmem=VMEM_LIMIT):
    return pltpu.CompilerParams(dimension_semantics=("arbitrary",) * n_axes, vmem_limit_bytes=vmem)


def _mod_spec(d, seq, tm, cond_row):
    if cond_row is None:
        tiles_per_seq = seq // tm
        return pl.BlockSpec((None, N_MOD, d), lambda i: (i // tiles_per_seq, 0, 0))
    return pl.BlockSpec((None, N_MOD, d), lambda i: (cond_row, 0, 0))


def _rms(x):
    return x * lax.rsqrt(jnp.mean(x * x, axis=-1, keepdims=True) + EPS)


def _dot(a, b):
    return jnp.dot(a, b, preferred_element_type=F32)


_NT = (((1,), (1,)), ((), ()))


def _mod_kernel(c_ref, w_ref, b_ref, o_ref):
    c = c_ref[...]
    s = c * jax.nn.sigmoid(c)
    o_ref[0] = _dot(s.astype(BF), w_ref[0].astype(BF)) + b_ref[0]


def _modulation(cc, ada_w, ada_b):
    depth, d, n = ada_w.shape
    tn = 1152
    return pl.pallas_call(
        _mod_kernel,
        grid=(depth, n // tn),
        in_specs=[
            pl.BlockSpec((8, d), lambda l, j: (0, 0)),
            pl.BlockSpec((1, d, tn), lambda l, j: (l, 0, j)),
            pl.BlockSpec((1, 1, tn), lambda l, j: (l, 0, j)),
        ],
        out_specs=pl.BlockSpec((1, 8, tn), lambda l, j: (l, 0, j)),
        out_shape=jax.ShapeDtypeStruct((depth, 8, n), F32),
        compiler_params=_cparams(2),
        name="adaln_mod",
    )(cc, ada_w, ada_b.reshape(depth, 1, n))


def _ffn_kernel(*refs, k0, mixer_out, final):
    refs = list(refs)
    x_ref, mod_ref = refs[:2]
    del refs[:2]
    if mixer_out:
        yh_ref, ys_ref, ya_ref, wo_ref = refs[:4]
        del refs[:4]
    wg_ref, wu_ref, wd_ref = refs[:3]
    del refs[:3]
    if final:
        fn_ref = refs.pop(0)
    o_ref, a_ref = refs
    x = x_ref[...]
    if mixer_out:
        acc = _dot(yh_ref[...].astype(BF), wo_ref[0:HY_C, :])
        acc = acc + _dot(ys_ref[...], wo_ref[HY_C:HY_C + SG_C, :])
        acc = acc + _dot(ya_ref[...], wo_ref[HY_C + SG_C:, :])
        x = x + mod_ref[5:6, :] * acc
    shift = mod_ref[k0:k0 + 1, :]
    scale = mod_ref[k0 + 1:k0 + 2, :]
    gate = mod_ref[k0 + 2:k0 + 3, :]
    hb = (_rms(x) * (1.0 + scale) + shift).astype(BF)
    for j in range(D_FF // FFN_CHUNK):
        cols = slice(j * FFN_CHUNK, (j + 1) * FFN_CHUNK)
        g = _dot(hb, wg_ref[:, cols])
        u = _dot(hb, wu_ref[:, cols])
        a_ref[:, cols] = (g * jax.nn.sigmoid(g) * u).astype(BF)
    y = x + (0.5 * gate) * _dot(a_ref[...], wd_ref[...])
    if final:
        y = _rms(y) * fn_ref[...]
    o_ref[...] = y


def _ffn(x, mod_l, k0, layer, wg, wu, wd, *, tm, seq, cond_row=None, mixer=None, final_norm=None):
    t, d = x.shape
    final = final_norm is not None
    const = lambda i: (0, 0)
    row = lambda i: (i, 0)
    slab = lambda i: (layer, 0, 0)
    in_specs = [pl.BlockSpec((tm, d), row), _mod_spec(d, seq, tm, cond_row)]
    args = [x, mod_l]
    if mixer is not None:
        in_specs += [
            pl.BlockSpec((tm, HY_C), row),
            pl.BlockSpec((tm, SG_C), row),
            pl.BlockSpec((tm, MLA_HEADS * MLA_V), row),
            pl.BlockSpec((None, HY_C + SG_C + MLA_HEADS * MLA_V, d), slab, pipeline_mode=pl.Buffered(1)),
        ]
        args += list(mixer)
    in_specs += [
        pl.BlockSpec((None, d, D_FF), slab, pipeline_mode=pl.Buffered(1)),
        pl.BlockSpec((None, d, D_FF), slab, pipeline_mode=pl.Buffered(1)),
        pl.BlockSpec((None, D_FF, d), slab, pipeline_mode=pl.Buffered(1)),
    ]
    args += [wg, wu, wd]
    if final:
        in_specs.append(pl.BlockSpec((1, d), const))
        args.append(final_norm.reshape(1, d))
    return pl.pallas_call(
        functools.partial(_ffn_kernel, k0=k0, mixer_out=mixer is not None, final=final),
        grid=(t // tm,),
        in_specs=in_specs,
        out_specs=pl.BlockSpec((tm, d), lambda i: (i, 0)),
        out_shape=jax.ShapeDtypeStruct((t, d), F32),
        scratch_shapes=[pltpu.VMEM((tm, D_FF), BF)],
        compiler_params=_cparams(1),
        name="ffn",
    )(*args)


def _gelu(x):
    return 0.5 * x * (1.0 + lax.erf(x * (1.0 / math.sqrt(2.0))))


def _mix_in_kernel(x_ref, mod_ref, wall_ref, ca_ref, sb_ref, ck_ref, gq_ref, wq_ref, gkv_ref, wk_ref, wvt_ref,
                   vone_ref, lng_ref, lnb_ref, wsg_ref, bsg_ref,
                   p3_ref, ysg_ref, q_ref, k_ref, vt_ref, *, tm):
    x = x_ref[...]
    shift = mod_ref[3:4, :]
    scale = mod_ref[4:5, :]
    hb = (_rms(x) * (1.0 + scale) + shift).astype(BF)
    proj = _dot(hb, wall_ref[...])
    p3_ref[...] = proj[:, 0:OFF_SG]

    z = _gelu(proj[:, OFF_SG:OFF_Q])
    u = z[:, 0:SG_C]
    v = z[:, SG_C:2 * SG_C]
    mu = jnp.mean(v, axis=-1, keepdims=True)
    vc = v - mu
    var = jnp.mean(vc * vc, axis=-1, keepdims=True)
    vb = ((vc * lax.rsqrt(var + EPS)) * lng_ref[...] + lnb_ref[...]).astype(BF)
    lane_g = lax.broadcasted_iota(jnp.int32, (SG_CHUNK, SG_C), 1) // (SG_C // SG_GROUPS)
    for j in range(tm // SG_CHUNK):
        rows = slice(j * SG_CHUNK, (j + 1) * SG_CHUNK)
        o = _dot(wsg_ref[...], vb[rows, :])
        s = o[(SG_GROUPS - 1) * SG_CHUNK:, :]
        for g in range(SG_GROUPS - 2, -1, -1):
            s = jnp.where(lane_g == g, o[g * SG_CHUNK:(g + 1) * SG_CHUNK, :], s)
        ysg_ref[rows, :] = (u[rows, :] * (s + bsg_ref[...])).astype(BF)

    cqn = (_rms(proj[:, OFF_Q:OFF_KV]) * gq_ref[...]).astype(BF)
    qq = _dot(cqn, wq_ref[...])
    ca = ca_ref[...]
    sb = sb_ref[...]
    for h in range(MLA_HEADS):
        cols = slice(h * HEAD_PAD, (h + 1) * HEAD_PAD)
        blk = qq[:, cols]
        q_ref[:, cols] = (blk * ca + pltpu.roll(blk, HEAD_PAD - MLA_ROPE, 1) * sb).astype(BF)

    kvn = (_rms(proj[:, OFF_KV:OFF_KR]) * gkv_ref[...]).astype(BF)
    vt_ref[...] = (lax.dot_general(wvt_ref[...], kvn, _NT, preferred_element_type=F32) + vone_ref[...]).astype(BF)
    kv = _dot(kvn, wk_ref[...])
    t = proj[:, OFF_KR:OFF_KR + LANE] * ck_ref[...]
    krr = t + pltpu.roll(t, LANE - MLA_ROPE, 1)
    lane = lax.broadcasted_iota(jnp.int32, (tm, LANE), 1)
    in_rope = (lane >= MLA_NOPE) & (lane < MLA_NOPE + MLA_ROPE)
    krp = jnp.where(in_rope, pltpu.roll(krr, MLA_NOPE, 1), 0.0)
    for h in range(MLA_HEADS):
        cols = slice(h * HEAD_PAD, (h + 1) * HEAD_PAD)
        k_ref[:, cols] = (kv[:, cols] + krp).astype(BF)


def _mix_in(x, mod_l, cond_row, seq, w, layer, vone, tabs, *, tm):
    t, d = x.shape
    tiles_per_seq = seq // tm
    const = lambda i: (0, 0)
    tab = lambda i: (i % tiles_per_seq, 0)
    nq = MLA_HEADS * HEAD_PAD
    row = lambda i: (i, 0)
    wspec = lambda *shape: pl.BlockSpec((None,) + shape, lambda i: (layer, 0, 0))
    return pl.pallas_call(
        functools.partial(_mix_in_kernel, tm=tm),
        grid=(t // tm,),
        in_specs=[
            pl.BlockSpec((tm, d), row),
            _mod_spec(d, seq, tm, cond_row),
            wspec(d, N_IN_PAD),
            pl.BlockSpec((tm, LANE), tab),
            pl.BlockSpec((tm, LANE), tab),
            pl.BlockSpec((tm, LANE), tab),
            wspec(1, MLA_Q_RANK),
            wspec(MLA_Q_RANK, nq),
            wspec(1, MLA_KV_RANK),
            wspec(MLA_KV_RANK, nq),
            wspec(MLA_HEADS * V_ROWS, MLA_KV_RANK),
            pl.BlockSpec((MLA_HEADS * V_ROWS, 1), const),
            wspec(1, SG_C),
            wspec(1, SG_C),
            wspec(SG_GROUPS * SG_CHUNK, SG_CHUNK),
            wspec(SG_CHUNK, SG_C),
        ],
        out_specs=[
            pl.BlockSpec((tm, OFF_SG), row),
            pl.BlockSpec((tm, SG_C), row),
            pl.BlockSpec((tm, nq), row),
            pl.BlockSpec((tm, nq), row),
            pl.BlockSpec((MLA_HEADS * V_ROWS, tm), lambda i: (0, i)),
        ],
        out_shape=[
            jax.ShapeDtypeStruct((t, OFF_SG), F32),
            jax.ShapeDtypeStruct((t, SG_C), BF),
            jax.ShapeDtypeStruct((t, nq), BF),
            jax.ShapeDtypeStruct((t, nq), BF),
            jax.ShapeDtypeStruct((MLA_HEADS * V_ROWS, t), BF),
        ],
        compiler_params=_cparams(1),
        name="mix_in",
    )(x, mod_l, w["w_all"], tabs["ca"], tabs["sb"], tabs["ck"], w["gq"], w["wq"], w["gkv"], w["wk"], w["wvt"], vone,
      w["lng"], w["lnb"], w["wsg"], w["bsg"])


KEY_BLOCK = 128


def _attn_kernel(*refs, n_seg, tq):
    q_ref = refs[0]
    segs = [(refs[1 + 2 * s], refs[2 + 2 * s]) for s in range(n_seg)]
    o_ref, p_buf, m_buf = refs[1 + 2 * n_seg:]
    offs = [0]
    for k_ref, _ in segs:
        offs.append(offs[-1] + k_ref.shape[0])
    blocks = [(s, j, min(KEY_BLOCK, offs[s + 1] - offs[s]))
              for s in range(n_seg) for j in range(0, offs[s + 1] - offs[s], KEY_BLOCK)]
    stages = [(t, h) for t in range(q_ref.shape[0] // tq) for h in range(MLA_HEADS)]

    def scores(i):
        t, h = stages[i]
        cols = slice(h * HEAD_PAD, (h + 1) * HEAD_PAD)
        qt = q_ref[t * tq:(t + 1) * tq, cols].astype(F32).T.astype(BF)
        for b, (s, j, n) in enumerate(blocks):
            st = _dot(segs[s][0][j:j + n, cols], qt)
            mb = jnp.max(st, axis=0, keepdims=True)
            m_buf[i % 2, b:b + 1, :] = mb
            p_buf[i % 2, offs[s] + j:offs[s] + j + n, :] = jnp.exp2(st - mb).astype(BF)

    def values(i, prev):
        t, h = stages[i]
        mall = m_buf[i % 2, 0:len(blocks), :]
        w = jnp.exp2(mall - jnp.max(mall, axis=0, keepdims=True))
        acc = None
        for b, (s, j, n) in enumerate(blocks):
            part = _dot(segs[s][1][h * V_ROWS:(h + 1) * V_ROWS, j:j + n],
                        p_buf[i % 2, offs[s] + j:offs[s] + j + n, :]) * w[b:b + 1, :]
            acc = part if acc is None else acc + part
        o = acc[0:MLA_V, :] / acc[MLA_V:MLA_V + 1, :]
        if h % 2 == 0:
            return o
        pair = h // 2
        o_ref[t * tq:(t + 1) * tq, pair * HEAD_PAD:(pair + 1) * HEAD_PAD] = (
            jnp.concatenate([prev, o], axis=0).T.astype(BF))
        return None

    prev = None
    n = len(stages)
    for step in range(n + 1):
        if step < n:
            scores(step)
        if step >= 1:
            prev = values(step - 1, prev)


def _attention(q, segs, *, batch, nq_seq, tq, tiles_per_step):
    t = q.shape[0]
    rows = tq * tiles_per_step
    nqt = nq_seq // rows
    width = MLA_HEADS * HEAD_PAD
    in_specs = [pl.BlockSpec((rows, width), lambda b, i: (b * nqt + i, 0))]
    args = [q]
    nk_total = n_blocks = 0
    for k, vt, nk in segs:
        in_specs += [pl.BlockSpec((nk, width), lambda b, i: (b, 0)),
                     pl.BlockSpec((MLA_HEADS * V_ROWS, nk), lambda b, i: (0, b))]
        args += [k, vt]
        nk_total += nk
        n_blocks += pl.cdiv(nk, KEY_BLOCK)
    assert all(nk % min(KEY_BLOCK, nk) == 0 for _, _, nk in segs)
    return pl.pallas_call(
        functools.partial(_attn_kernel, n_seg=len(segs), tq=tq),
        grid=(batch, nqt),
        in_specs=in_specs,
        out_specs=pl.BlockSpec((rows, MLA_HEADS * MLA_V), lambda b, i: (b * nqt + i, 0)),
        out_shape=jax.ShapeDtypeStruct((t, MLA_HEADS * MLA_V), BF),
        scratch_shapes=[pltpu.VMEM((2, nk_total, tq), BF),
                        pltpu.VMEM((2, 8 * pl.cdiv(n_blocks, 8), tq), F32)],
        compiler_params=_cparams(2),
        name="attn_lat" if len(segs) > 1 else "attn_ctx",
    )(*args)


def _filt_kernel(et_ref, w1_ref, b1_ref, f0_ref, w2_ref, b2_ref, f1_ref, w3_ref, win_ref, o_ref):
    e = et_ref[...].astype(BF)
    h = jnp.sin(f0_ref[...] * (_dot(w1_ref[...], e) + b1_ref[...]))
    h = jnp.sin(f1_ref[...] * (_dot(w2_ref[...], h.astype(BF)) + b2_ref[...]))
    win = win_ref[...]
    o_ref[...] = _dot(w3_ref[0], h.astype(BF)) * jnp.concatenate([win, win], axis=0)


def _filter_time(tabs, fw, layer, *, pb):
    n2l = tabs["et"].shape[1]
    nb = n2l // pb
    col = lambda i: (0, i)
    rows = 2 * HY_C
    wspec = lambda *shape: pl.BlockSpec((None,) + shape, lambda i: (layer, 0, 0))
    return pl.pallas_call(
        _filt_kernel,
        grid=(nb,),
        in_specs=[
            pl.BlockSpec((HY_FFN, pb), col),
            wspec(HY_FFN, HY_FFN),
            wspec(HY_FFN, 1),
            wspec(HY_FFN, 1),
            wspec(HY_FFN, HY_FFN),
            wspec(HY_FFN, 1),
            wspec(HY_FFN, 1),
            pl.BlockSpec((None, 1, rows, HY_FFN), lambda i: (layer, i // (nb // 2), 0, 0)),
            pl.BlockSpec((HY_C, pb), col),
        ],
        out_specs=pl.BlockSpec((rows, pb), col),
        out_shape=jax.ShapeDtypeStruct((rows, n2l), F32),
        compiler_params=_cparams(1),
        name="hyena_filter",
    )(tabs["et"], fw["w1t"], fw["b1"], fw["f0"], fw["w2t"], fw["b2"], fw["f1"], fw["w3t"], tabs["win"])


DFT_CG = 8


def _dft_stage1(m1, rhs, tw_ref, lhs_ref, c0, n1f):
    r = _dot(m1, rhs)
    tr = tw_ref[0]
    ti = tw_ref[1]
    for c in range(DFT_CG):
        br = r[0:n1f, c * LANE:(c + 1) * LANE]
        bi = r[n1f:2 * n1f, c * LANE:(c + 1) * LANE]
        q0 = (c0 + c) * n1f
        lhs_ref[q0:q0 + n1f, 0:LANE] = (br * tr - bi * ti).astype(BF)
        lhs_ref[q0:q0 + n1f, LANE:2 * LANE] = (br * ti + bi * tr).astype(BF)


def _kf_kernel(k_ref, m1_ref, tw_ref, g2_ref, o_ref, lhs_ref, *, cb, n1f):
    zero_at = n1f // 2
    rowi = lax.broadcasted_iota(jnp.int32, (n1f, DFT_CG * LANE), 0)
    lanei = lax.broadcasted_iota(jnp.int32, (n1f, DFT_CG * LANE), 1)
    pad_tap = (rowi == zero_at) & ((lanei & (LANE - 1)) == 0)
    for c0 in range(0, cb, DFT_CG):
        k = jnp.concatenate([k_ref[c0 + c] for c in range(DFT_CG)], axis=1)
        _dft_stage1(m1_ref[...], jnp.where(pad_tap, 0.0, k).astype(BF), tw_ref, lhs_ref, c0, n1f)
    o_ref[...] = _dot(lhs_ref[...], g2_ref[...])


def _filter_spectrum(ktime3, consts, *, cb):
    rows, n1f, _ = ktime3.shape
    const2 = lambda i: (0, 0)
    return pl.pallas_call(
        functools.partial(_kf_kernel, cb=cb, n1f=n1f),
        grid=(rows // cb,),
        in_specs=[
            pl.BlockSpec((cb, n1f, LANE), lambda i: (i, 0, 0)),
            pl.BlockSpec((2 * n1f, n1f), const2),
            pl.BlockSpec((2, n1f, LANE), lambda i: (0, 0, 0)),
            pl.BlockSpec((2 * LANE, 2 * LANE), const2),
        ],
        out_specs=pl.BlockSpec((cb * n1f, 2 * LANE), lambda i: (i, 0)),
        out_shape=jax.ShapeDtypeStruct((rows * n1f, 2 * LANE), F32),
        scratch_shapes=[pltpu.VMEM((cb * n1f, 2 * LANE), BF)],
        compiler_params=_cparams(1),
        name="hyena_filter_spectrum",
    )(ktime3, consts["m1f"], consts["tw"], consts["g2"])


def _hy_lat_kernel(x1_ref, x2_ref, v_ref, cw_ref, hb_ref, kf_ref, m1_ref, tw_ref, g2_ref, g2i_ref, m1i_ref,
                   o_ref, zx1, zx2, zv, vbuf, lhs_ref, c_ref, *, cb, n1h, rb):
    n1f = 2 * n1h
    r = cb * n1h
    rowi = lax.broadcasted_iota(jnp.int32, (r, LANE), 0)
    lanei = lax.broadcasted_iota(jnp.int32, (r, LANE), 1)
    n1i = rowi & (n1h - 1)
    first = (n1i == 0) & (lanei == 0)
    last = (n1i == n1h - 1) & (lanei == LANE - 1)

    def bro(p):
        return jnp.broadcast_to(p, (cb, n1h, LANE)).reshape(r, LANE)

    def sconv(u3, s):
        u = u3.reshape(r, LANE)
        a = pltpu.roll(u, 1, 1)
        prev = jnp.where(lanei == 0, pltpu.roll(a, 1, 0), a)
        prev = jnp.where(first, 0.0, prev)
        f = pltpu.roll(u, LANE - 1, 1)
        nxt = jnp.where(lanei == LANE - 1, pltpu.roll(f, r - 1, 0), f)
        nxt = jnp.where(last, 0.0, nxt)
        return prev * bro(cw_ref[s, 0]) + u * bro(cw_ref[s, 1]) + nxt * bro(cw_ref[s, 2]) + bro(cw_ref[s, 3])

    for b in range(2):
        zx1[b] = sconv(x1_ref[b], 0)
        zx2[b] = sconv(x2_ref[b], 1)
        zv[b] = sconv(v_ref[b], 2)

    cg = DFT_CG
    gr = cg * n1h

    def conv_order(o, vin, gate, write):
        for c0 in range(0, cb, cg):
            zr = vin[0, c0 * n1h:c0 * n1h + gr, :]
            zi = vin[1, c0 * n1h:c0 * n1h + gr, :]
            rhs = jnp.concatenate([
                jnp.concatenate([zr[c * n1h:(c + 1) * n1h] for c in range(cg)], axis=1),
                jnp.concatenate([zi[c * n1h:(c + 1) * n1h] for c in range(cg)], axis=1)], axis=0).astype(BF)
            _dft_stage1(m1_ref[...], rhs, tw_ref, lhs_ref, c0, n1f)

        for q in range(0, cb * n1f, rb):
            x = _dot(lhs_ref[q:q + rb, :], g2_ref[...])
            kf = kf_ref[o, q:q + rb, :]
            xr, xi = x[:, 0:LANE], x[:, LANE:]
            kr, ki = kf[:, 0:LANE], kf[:, LANE:]
            y = jnp.concatenate([xr * kr - xi * ki, xr * ki + xi * kr], axis=1).astype(BF)
            c_ref[q:q + rb, :] = _dot(y, g2i_ref[...])

        tr = tw_ref[0]
        ti = tw_ref[1]
        for c0 in range(0, cb, cg):
            ars, ais = [], []
            for c in range(cg):
                t = c_ref[(c0 + c) * n1f:(c0 + c + 1) * n1f, :]
                cr, ci = t[:, 0:LANE], t[:, LANE:]
                ars.append(cr * tr + ci * ti)
                ais.append(ci * tr - cr * ti)
            rhs = jnp.concatenate([jnp.concatenate(ars, axis=1), jnp.concatenate(ais, axis=1)], axis=0).astype(BF)
            out = _dot(m1i_ref[...], rhs)
            r0 = c0 * n1h
            bias = jnp.broadcast_to(hb_ref[o, c0:c0 + cg], (cg, n1h, LANE)).reshape(gr, LANE)
            for b in range(2):
                ob = out[b * n1h:(b + 1) * n1h, :]
                y = jnp.concatenate([ob[:, c * LANE:(c + 1) * LANE] for c in range(cg)], axis=0)
                write(b, c0, r0, gate[b, r0:r0 + gr, :] * (y + vin[b, r0:r0 + gr, :] * bias))

    def write_mid(b, c0, r0, val):
        vbuf[b, r0:r0 + gr, :] = val

    def write_out(b, c0, r0, val):
        o_ref[b, c0:c0 + cg] = val.reshape(cg, n1h, LANE).astype(o_ref.dtype)

    conv_order(0, zv, zx1, write_mid)
    conv_order(1, vbuf, zx2, write_out)


def _hyena_lat(p3t, cw, hb, kf, consts, *, cb, rb):
    batch, _, n1h, _ = p3t.shape
    n1f = 2 * n1h
    ncb = HY_C // cb
    r = cb * n1h
    const2 = lambda p, c: (0, 0)
    sec = lambda k: pl.BlockSpec((2, cb, n1h, LANE), lambda p, c, k=k: (p, c + k * ncb, 0, 0))
    return pl.pallas_call(
        functools.partial(_hy_lat_kernel, cb=cb, n1h=n1h, rb=rb),
        grid=(batch // 2, ncb),
        in_specs=[
            sec(0), sec(1), sec(2),
            pl.BlockSpec((3, 4, cb, 1, LANE), lambda p, c: (0, 0, c, 0, 0)),
            pl.BlockSpec((2, cb, 1, LANE), lambda p, c: (0, c, 0, 0)),
            pl.BlockSpec((2, cb * n1f, 2 * LANE), lambda p, c: (0, c, 0)),
            pl.BlockSpec((2 * n1f, n1f), const2),
            pl.BlockSpec((2, n1f, LANE), lambda p, c: (0, 0, 0)),
            pl.BlockSpec((2 * LANE, 2 * LANE), const2),
            pl.BlockSpec((2 * LANE, 2 * LANE), const2),
            pl.BlockSpec((n1f, 2 * n1f), const2),
        ],
        out_specs=pl.BlockSpec((2, cb, n1h, LANE), lambda p, c: (p, c, 0, 0)),
        out_shape=jax.ShapeDtypeStruct((batch, HY_C, n1h, LANE), BF),
        scratch_shapes=[
            pltpu.VMEM((2, r, LANE), F32), pltpu.VMEM((2, r, LANE), F32),
            pltpu.VMEM((2, r, LANE), F32), pltpu.VMEM((2, r, LANE), F32),
            pltpu.VMEM((cb * n1f, 2 * LANE), BF), pltpu.VMEM((cb * n1f, 2 * LANE), F32),
        ],
        compiler_params=_cparams(2),
        name="hyena_lat",
    )(p3t, p3t, p3t, cw, hb, kf, consts["m1"], consts["tw"], consts["g2"], consts["g2i"], consts["m1i"])


def _hy_ctx_kernel(x1_ref, x2_ref, v_ref, cw_ref, hb_ref, kt_ref, mf_ref, mff_ref, mi_ref, o_ref, *, lc):
    lanei = lax.broadcasted_iota(jnp.int32, (HY_C, lc), 1)

    def sconv(u, s):
        prev = jnp.where(lanei == 0, 0.0, pltpu.roll(u, 1, 1))
        nxt = jnp.where(lanei == lc - 1, 0.0, pltpu.roll(u, lc - 1, 1))
        return prev * cw_ref[s, 0] + u * cw_ref[s, 1] + nxt * cw_ref[s, 2] + cw_ref[s, 3]

    zx1 = [sconv(x1_ref[b], 0) for b in range(2)]
    zx2 = [sconv(x2_ref[b], 1) for b in range(2)]
    v = [sconv(v_ref[b], 2) for b in range(2)]

    kt = kt_ref[...]
    kl = lax.broadcasted_iota(jnp.int32, kt.shape, 1)
    kf = _dot(jnp.where(kl == lc, 0.0, kt).astype(BF), mff_ref[...])
    n = 2 * lc
    for o in range(2):
        z = jnp.concatenate([v[0], v[1]], axis=1).astype(BF)
        x = _dot(z, mf_ref[...])
        xr, xi = x[:, 0:n], x[:, n:]
        kr = kf[o * HY_C:(o + 1) * HY_C, 0:n]
        ki = kf[o * HY_C:(o + 1) * HY_C, n:]
        y = _dot(jnp.concatenate([xr * kr - xi * ki, xr * ki + xi * kr], axis=1).astype(BF), mi_ref[...])
        gate = zx1 if o == 0 else zx2
        v = [gate[b] * (y[:, b * lc:(b + 1) * lc] + v[b] * hb_ref[o]) for b in range(2)]
    for b in range(2):
        o_ref[b] = v[b].astype(o_ref.dtype)


def _hyena_ctx(p3t, cw, hb, kt, consts):
    batch, _, lc = p3t.shape
    n = 2 * lc
    sec = lambda k: pl.BlockSpec((2, HY_C, lc), lambda p, k=k: (p, k, 0))
    return pl.pallas_call(
        functools.partial(_hy_ctx_kernel, lc=lc),
        grid=(batch // 2,),
        in_specs=[
            sec(0), sec(1), sec(2),
            pl.BlockSpec((3, 4, HY_C, 1), lambda p: (0, 0, 0, 0)),
            pl.BlockSpec((2, HY_C, 1), lambda p: (0, 0, 0)),
            pl.BlockSpec((2 * HY_C, n), lambda p: (0, 0)),
            pl.BlockSpec((n, 2 * n), lambda p: (0, 0)),
            pl.BlockSpec((n, 2 * n), lambda p: (0, 0)),
            pl.BlockSpec((2 * n, n), lambda p: (0, 0)),
        ],
        out_specs=pl.BlockSpec((2, HY_C, lc), lambda p: (p, 0, 0)),
        out_shape=jax.ShapeDtypeStruct((batch, HY_C, lc), BF),
        compiler_params=_cparams(1),
        name="hyena_ctx",
    )(p3t, p3t, p3t, cw, hb, kt, consts["mf"], consts["mff"], consts["mi"])


def _mxu_const(a):
    return jnp.asarray(a, F32).astype(BF)


def _dft_consts_two_stage(n1f):
    n2 = DFT_N2
    n = n1f * n2
    n1h = n1f // 2
    k1 = np.arange(n1f)
    ang1 = 2.0 * np.pi * np.outer(k1, k1) / n1f
    f1r, f1i = np.cos(ang1), -np.sin(ang1)
    m1 = np.block([[f1r[:, :n1h], -f1i[:, :n1h]], [f1i[:, :n1h], f1r[:, :n1h]]])
    m1f = np.concatenate([f1r, f1i], axis=0)
    angt = 2.0 * np.pi * np.outer(k1, np.arange(n2)) / n
    tw = np.stack([np.cos(angt), -np.sin(angt)])
    a2 = 2.0 * np.pi * np.outer(np.arange(n2), np.arange(n2)) / n2
    f2r, f2i = np.cos(a2), -np.sin(a2)
    g2 = np.block([[f2r, f2i], [-f2i, f2r]])
    g2i = np.block([[f2r, -f2i], [f2i, f2r]])
    m1i = np.block([[f1r[:n1h, :], f1i[:n1h, :]], [-f1i[:n1h, :], f1r[:n1h, :]]]) / n
    return {
        "m1": _mxu_const(m1), "m1f": _mxu_const(m1f), "tw": jnp.asarray(tw, F32),
        "g2": _mxu_const(g2), "g2i": _mxu_const(g2i), "m1i": _mxu_const(m1i),
    }


def _dft_consts_single(lc):
    n = 2 * lc
    ang = 2.0 * np.pi * np.outer(np.arange(n), np.arange(n)) / n
    fr, fi = np.cos(ang), -np.sin(ang)
    mf = np.block([[fr[:lc], fi[:lc]], [-fi[:lc], fr[:lc]]])
    mff = np.concatenate([fr, fi], axis=1)
    gr, gi = fr[:, :lc] / n, -fi[:, :lc] / n
    mi = np.block([[gr, gi], [-gi, gr]])
    return {"mf": _mxu_const(mf), "mff": _mxu_const(mff), "mi": _mxu_const(mi)}


def _filter_tables(seq):
    f32 = np.float32
    pos = np.arange(seq, dtype=f32)
    t = pos / f32(max(seq - 1, 1))
    w = f32(2.0 * math.pi) * pos / f32(seq)
    bands = np.linspace(1e-4, HY_BANDS - 1, HY_BANDS, dtype=f32)
    ang = w[:, None] * bands[None, :]
    z = np.concatenate([t[:, None], np.cos(ang), -np.sin(ang)], axis=-1)
    min_decay = abs(math.log(HY_DECAY_TARGET) / HY_SLOW_PCT)
    max_decay = abs(math.log(HY_DECAY_TARGET) / HY_FAST_PCT)
    deltas = np.linspace(min_decay, max_decay, HY_C, dtype=f32)
    window = np.exp(-t[:, None] * deltas[None, :]) + f32(HY_WINDOW_SHIFT)
    idx = np.concatenate([np.arange(seq), np.array([0]), np.arange(seq - 1, 0, -1)])
    et = np.pad(z[idx], ((0, 0), (0, HY_FFN - HY_EMB))).T
    win = window[idx].T
    return {"et": jnp.asarray(et, F32), "win": jnp.asarray(win, F32)}


def _rope_tables(n_lat, n_ctx):
    f32 = np.float32
    n_rows = n_lat // GRID_W
    rows = np.repeat(np.arange(n_rows, dtype=f32), GRID_W)
    cols = np.tile(np.arange(GRID_W, dtype=f32), n_rows)
    quarter = MLA_ROPE // 4
    inv = (f32(1.0) / np.power(f32(ROPE_BASE), np.arange(quarter, dtype=f32) / f32(quarter))).astype(f32)
    ang = np.stack([rows[:, None] * inv, cols[:, None] * inv], axis=1)
    r = np.arange(MLA_ROPE)
    axis, half, i = r // (2 * quarter), (r % (2 * quarter)) // quarter, r % quarter
    cos32 = np.cos(ang)[:, axis, i]
    ssin32 = np.sin(ang)[:, axis, i] * np.where(half == 0, -1.0, 1.0).astype(f32)
    ones = np.ones((n_lat, MLA_NOPE), f32)
    zpad = np.zeros((n_lat, HEAD_PAD - MLA_NOPE - MLA_ROPE), f32)
    z64 = np.zeros((n_lat, MLA_NOPE), f32)
    qs = f32(Q_SCALE)
    lat = {
        "ca": qs * np.concatenate([ones, cos32, zpad], axis=1),
        "sb": qs * np.concatenate([z64, ssin32, zpad], axis=1),
        "ck": np.concatenate([cos32, ssin32, z64], axis=1),
    }
    one_c = np.ones((n_ctx, MLA_NOPE + MLA_ROPE), f32)
    ctx = {
        "ca": qs * np.concatenate([one_c, np.zeros((n_ctx, HEAD_PAD - MLA_NOPE - MLA_ROPE), f32)], axis=1),
        "sb": np.zeros((n_ctx, HEAD_PAD), f32),
        "ck": np.concatenate([np.ones((n_ctx, MLA_ROPE), f32), np.zeros((n_ctx, LANE - MLA_ROPE), f32)], axis=1),
    }
    to_dev = lambda d: {k: jnp.asarray(v, F32) for k, v in d.items()}
    return to_dev(lat), to_dev(ctx)


def _rope_swap():
    quarter = MLA_ROPE // 4
    r = np.arange(MLA_ROPE)
    half = (r % (2 * quarter)) // quarter
    return np.where(half == 0, r + quarter, r - quarter)


def _mix_weights(w_in, mla_q_norm, mla_w_uq, mla_kv_norm, mla_w_ukv, sg_ln_g, sg_ln_b, sg_w, sg_b):
    return jax.vmap(_mix_weights_layer)(w_in, mla_q_norm, mla_w_uq, mla_kv_norm, mla_w_ukv, sg_ln_g, sg_ln_b, sg_w, sg_b)


def _mix_weights_layer(wi, q_norm, w_uq, kv_norm, w_ukv, ln_g, ln_b, w_s, b_s):
    sw = _rope_swap()
    kr = wi[:, OFF_KR:N_IN]
    w_all = jnp.concatenate(
        [wi, kr[:, sw], jnp.zeros((D_MODEL, N_IN_PAD - N_IN - MLA_ROPE), F32)], axis=1).astype(BF)
    dq = MLA_NOPE + MLA_ROPE
    uq = w_uq.reshape(MLA_Q_RANK, MLA_HEADS, dq)
    assert dq + MLA_ROPE == HEAD_PAD
    wq = jnp.concatenate([uq, uq[:, :, MLA_NOPE:][:, :, sw]], axis=2).reshape(MLA_Q_RANK, -1).astype(BF)
    ukv = w_ukv.reshape(MLA_KV_RANK, MLA_HEADS, MLA_NOPE + MLA_V)
    wk = jnp.pad(ukv[:, :, :MLA_NOPE], ((0, 0), (0, 0), (0, HEAD_PAD - MLA_NOPE))).reshape(MLA_KV_RANK, -1)
    wvt = jnp.pad(ukv[:, :, MLA_NOPE:], ((0, 0), (0, 0), (0, V_ROWS - MLA_V))).reshape(MLA_KV_RANK, -1).T
    return {
        "w_all": w_all,
        "gq": q_norm.reshape(1, -1), "wq": wq,
        "gkv": kv_norm.reshape(1, -1), "wk": wk.astype(BF), "wvt": wvt.astype(BF),
        "lng": ln_g.reshape(1, -1), "lnb": ln_b.reshape(1, -1),
        "wsg": w_s.reshape(SG_GROUPS * SG_CHUNK, SG_CHUNK).astype(BF),
        "bsg": jnp.repeat(b_s.T, SG_C // SG_GROUPS, axis=1),
    }


def _value_ones_rows():
    vone = np.zeros((MLA_HEADS, V_ROWS, 1), np.float32)
    vone[:, MLA_V, 0] = 1.0
    return jnp.asarray(vone.reshape(MLA_HEADS * V_ROWS, 1))


def _filter_weights(hy_w1, hy_b1, hy_w2, hy_b2, hy_w3, hy_freq):
    depth = hy_w1.shape[0]
    w1t = jnp.swapaxes(jnp.pad(hy_w1, ((0, 0), (0, HY_FFN - HY_EMB), (0, 0))), 1, 2).astype(BF)
    w3 = hy_w3.reshape(depth, HY_FFN, 2, 2, HY_C)
    w3t = jnp.transpose(w3, (0, 3, 2, 4, 1)).reshape(depth, 2, 2 * HY_C, HY_FFN).astype(BF)
    col = lambda a: a.reshape(depth, HY_FFN, 1)
    return {"w1t": w1t, "b1": col(hy_b1), "f0": col(hy_freq[:, 0]),
            "w2t": jnp.swapaxes(hy_w2, 1, 2).astype(BF), "b2": col(hy_b2), "f1": col(hy_freq[:, 1]), "w3t": w3t}


def kernel(x, c, ctx, c_ctx, ada_w, ada_b, ffn1_wg, ffn1_wu, ffn1_wd, w_in, w_out, hy_conv_w, hy_conv_b, hy_w1, hy_b1, hy_w2, hy_b2, hy_w3, hy_freq, hy_bias, sg_ln_g, sg_ln_b, sg_w, sg_b, mla_q_norm, mla_w_uq, mla_kv_norm, mla_w_ukv, ffn2_wg, ffn2_wu, ffn2_wd, final_norm):
    batch, n_lat, d = x.shape
    n_ctx = ctx.shape[1]
    depth = ada_w.shape[0]
    assert d == D_MODEL and batch % 2 == 0 and batch <= 4
    assert n_lat % (2 * LANE) == 0 and n_ctx % SG_CHUNK == 0
    n1h = n_lat // LANE
    ctx_row = batch

    cc = jnp.concatenate([c, c_ctx[None, :], jnp.zeros((8 - batch - 1, d), F32)], axis=0)
    mods = _modulation(cc, ada_w, ada_b).reshape(depth, 8, N_MOD, d)

    rope_lat, rope_ctx = _rope_tables(n_lat, n_ctx)
    ftab_lat = _filter_tables(n_lat)
    ftab_ctx = _filter_tables(n_ctx)
    dft_lat = _dft_consts_two_stage(2 * n1h)
    dft_ctx = _dft_consts_single(n_ctx)

    xl = x.reshape(batch * n_lat, d)
    xc = ctx.reshape(batch * n_ctx, d)
    tm = 512
    ffn_tm = 1024
    ffn1_bf = [w.astype(BF) for w in (ffn1_wg, ffn1_wu, ffn1_wd)]
    ffn2_bf = [w.astype(BF) for w in (ffn2_wg, ffn2_wu, ffn2_wd)]
    wo = w_out.astype(BF)
    mw = _mix_weights(w_in, mla_q_norm, mla_w_uq, mla_kv_norm, mla_w_ukv, sg_ln_g, sg_ln_b, sg_w, sg_b)
    vone = _value_ones_rows()
    fw = _filter_weights(hy_w1, hy_b1, hy_w2, hy_b2, hy_w3, hy_freq)
    cw_all = jnp.concatenate([hy_conv_w, hy_conv_b[:, None, :]], axis=1)
    cw_all = jnp.transpose(cw_all.reshape(depth, 4, 3, HY_C), (0, 2, 1, 3))
    for l in range(depth):
        last = l == depth - 1
        mod_l = mods[l]

        xl = _ffn(xl, mod_l, 0, l, *ffn1_bf, tm=ffn_tm, seq=n_lat)
        xc = _ffn(xc, mod_l, 0, l, *ffn1_bf, tm=ffn_tm, seq=n_ctx, cond_row=ctx_row)

        p3, ysg, q, k, vt = _mix_in(xl, mod_l, None, n_lat, mw, l, vone, rope_lat, tm=tm)
        p3c, ysgc, qc, kc, vtc = _mix_in(xc, mod_l, ctx_row, n_ctx, mw, l, vone, rope_ctx, tm=n_ctx)

        y_attn = _attention(q, [(kc, vtc, n_ctx), (k, vt, n_lat)], batch=batch, nq_seq=n_lat, tq=256,
                            tiles_per_step=1)

        cw = cw_all[l]
        ktime = _filter_time(ftab_lat, fw, l, pb=1024)
        hb = hy_bias[l] + ktime[:, n_lat].reshape(2, HY_C)
        kf = _filter_spectrum(ktime.reshape(2 * HY_C, 2 * n1h, LANE), dft_lat, cb=32)
        p3t = jnp.transpose(p3.reshape(batch, n_lat, OFF_SG), (0, 2, 1)).reshape(batch, OFF_SG, n1h, LANE)
        y_hy = _hyena_lat(
            p3t,
            jnp.broadcast_to(cw[:, :, :, None, None], (3, 4, HY_C, 1, LANE)),
            jnp.broadcast_to(hb[:, :, None, None], (2, HY_C, 1, LANE)),
            kf.reshape(2, HY_C * 2 * n1h, 2 * LANE), dft_lat, cb=32, rb=512)
        y_hy = jnp.transpose(y_hy.reshape(batch, HY_C, n_lat), (0, 2, 1)).reshape(batch * n_lat, HY_C)

        xl = _ffn(xl, mod_l, 6, l, *ffn2_bf, tm=ffn_tm, seq=n_lat, mixer=(y_hy, ysg, y_attn, wo),
                  final_norm=final_norm if last else None)

        if not last:
            ktime_c = _filter_time(ftab_ctx, fw, l, pb=n_ctx)
            hbc = hy_bias[l] + ktime_c[:, n_ctx].reshape(2, HY_C)
            p3ct = jnp.transpose(p3c.reshape(batch, n_ctx, OFF_SG), (0, 2, 1))
            yc_hy = _hyena_ctx(p3ct, cw[:, :, :, None], hbc[:, :, None], ktime_c, dft_ctx)
            yc_hy = jnp.transpose(yc_hy, (0, 2, 1)).reshape(batch * n_ctx, HY_C)
            yc_attn = _attention(qc, [(kc, vtc, n_ctx)], batch=batch, nq_seq=n_ctx, tq=n_ctx, tiles_per_step=1)
            xc = _ffn(xc, mod_l, 6, l, *ffn2_bf, tm=ffn_tm, seq=n_ctx, cond_row=ctx_row,
                      mixer=(yc_hy, ysgc, yc_attn, wo))
    return xl.reshape(batch, n_lat, d)
```

```python
import functools
import math

import numpy as np
import jax
import jax.numpy as jnp
from jax import lax
from jax.experimental import pallas as pl
from jax.experimental.pallas import tpu as pltpu

F32 = jnp.float32
BF = jnp.bfloat16

D_MODEL = 1024
D_FF = 2816
N_MOD = 9
EPS = 1e-6
GRID_W = 64

HY_C = 256
HY_BANDS = 16
HY_EMB = 1 + 2 * HY_BANDS
HY_FFN = 64
HY_DECAY_TARGET = 1e-2
HY_FAST_PCT = 0.3
HY_SLOW_PCT = 1.5
HY_WINDOW_SHIFT = 0.05

SG_C = 256
SG_GROUPS = 4
SG_CHUNK = 128

MLA_HEADS = 8
MLA_NOPE = 64
MLA_ROPE = 32
MLA_V = 64
MLA_Q_RANK = 384
MLA_KV_RANK = 256
MLA_SCALE = (MLA_NOPE + MLA_ROPE) ** -0.5
Q_SCALE = MLA_SCALE * math.log2(math.e)
ROPE_BASE = 10000.0

OFF_SG = 3 * HY_C
OFF_Q = OFF_SG + 2 * SG_C
OFF_KV = OFF_Q + MLA_Q_RANK
OFF_KR = OFF_KV + MLA_KV_RANK
N_IN = OFF_KR + MLA_ROPE
N_IN_PAD = 2048

LANE = 128
HEAD_PAD = 128
V_ROWS = 80
FFN_CHUNK = 256
VMEM_LIMIT = 56 * 1024 * 1024

TOKEN_TILE = 512
FFN_TILE = 1024
ATTN_Q_TILE = 256
HYENA_CB = 32
HYENA_RB = 512
FILTER_PB = 1024

DFT_N2 = LANE


def _cparams(n_axes, vmem=VMEM_LIMIT):
    return pltpu.CompilerParams(dimension_semantics=("arbitrary",) * n_axes, vmem_limit_bytes=vmem)


def _mod_spec(d, seq, tm, cond_row):
    if cond_row is None:
        tiles_per_seq = seq // tm
        return pl.BlockSpec((None, N_MOD, d), lambda i: (i // tiles_per_seq, 0, 0))
    return pl.BlockSpec((None, N_MOD, d), lambda i: (cond_row, 0, 0))


def _rms(x):
    return x * lax.rsqrt(jnp.mean(x * x, axis=-1, keepdims=True) + EPS)


def _dot(a, b):
    return jnp.dot(a, b, preferred_element_type=F32)


_NT = (((1,), (1,)), ((), ()))


def _mod_kernel(c_ref, w_ref, b_ref, o_ref):
    c = c_ref[...]
    s = c * jax.nn.sigmoid(c)
    o_ref[0] = _dot(s.astype(BF), w_ref[0].astype(BF)) + b_ref[0]


def _modulation(cc, ada_w, ada_b):
    depth, d, n = ada_w.shape
    tn = 1152
    return pl.pallas_call(
        _mod_kernel,
        grid=(depth, n // tn),
        in_specs=[
            pl.BlockSpec((8, d), lambda l, j: (0, 0)),
            pl.BlockSpec((1, d, tn), lambda l, j: (l, 0, j)),
            pl.BlockSpec((1, 1, tn), lambda l, j: (l, 0, j)),
        ],
        out_specs=pl.BlockSpec((1, 8, tn), lambda l, j: (l, 0, j)),
        out_shape=jax.ShapeDtypeStruct((depth, 8, n), F32),
        compiler_params=_cparams(2),
        name="adaln_mod",
    )(cc, ada_w, ada_b.reshape(depth, 1, n))


def _ffn_kernel(*refs, k0, mixer_out, final):
    refs = list(refs)
    x_ref, mod_ref = refs[:2]
    del refs[:2]
    if mixer_out:
        yh_ref, ys_ref, ya_ref, wo_ref = refs[:4]
        del refs[:4]
    wg_ref, wu_ref, wd_ref = refs[:3]
    del refs[:3]
    if final:
        fn_ref = refs.pop(0)
    o_ref, a_ref = refs
    x = x_ref[...]
    if mixer_out:
        acc = _dot(yh_ref[...].astype(BF), wo_ref[0:HY_C, :])
        acc = acc + _dot(ys_ref[...], wo_ref[HY_C:HY_C + SG_C, :])
        acc = acc + _dot(ya_ref[...], wo_ref[HY_C + SG_C:, :])
        x = x + mod_ref[5:6, :] * acc
    shift = mod_ref[k0:k0 + 1, :]
    scale = mod_ref[k0 + 1:k0 + 2, :]
    gate = mod_ref[k0 + 2:k0 + 3, :]
    hb = (_rms(x) * (1.0 + scale) + shift).astype(BF)
    for j in range(D_FF // FFN_CHUNK):
        cols = slice(j * FFN_CHUNK, (j + 1) * FFN_CHUNK)
        g = _dot(hb, wg_ref[:, cols])
        u = _dot(hb, wu_ref[:, cols])
        a_ref[:, cols] = (g * jax.nn.sigmoid(g) * u).astype(BF)
    y = x + (0.5 * gate) * _dot(a_ref[...], wd_ref[...])
    if final:
        y = _rms(y) * fn_ref[...]
    o_ref[...] = y


def _ffn(x, mod_l, k0, layer, wg, wu, wd, *, tm, seq, cond_row=None, mixer=None, final_norm=None):
    t, d = x.shape
    final = final_norm is not None
    const = lambda i: (0, 0)
    row = lambda i: (i, 0)
    slab = lambda i: (layer, 0, 0)
    in_specs = [pl.BlockSpec((tm, d), row), _mod_spec(d, seq, tm, cond_row)]
    args = [x, mod_l]
    if mixer is not None:
        in_specs += [
            pl.BlockSpec((tm, HY_C), row),
            pl.BlockSpec((tm, SG_C), row),
            pl.BlockSpec((tm, MLA_HEADS * MLA_V), row),
            pl.BlockSpec((None, HY_C + SG_C + MLA_HEADS * MLA_V, d), slab, pipeline_mode=pl.Buffered(1)),
        ]
        args += list(mixer)
    in_specs += [
        pl.BlockSpec((None, d, D_FF), slab, pipeline_mode=pl.Buffered(1)),
        pl.BlockSpec((None, d, D_FF), slab, pipeline_mode=pl.Buffered(1)),
        pl.BlockSpec((None, D_FF, d), slab, pipeline_mode=pl.Buffered(1)),
    ]
    args += [wg, wu, wd]
    if final:
        in_specs.append(pl.BlockSpec((1, d), const))
        args.append(final_norm.reshape(1, d))
    return pl.pallas_call(
        functools.partial(_ffn_kernel, k0=k0, mixer_out=mixer is not None, final=final),
        grid=(t // tm,),
        in_specs=in_specs,
        out_specs=pl.BlockSpec((tm, d), lambda i: (i, 0)),
        out_shape=jax.ShapeDtypeStruct((t, d), F32),
        scratch_shapes=[pltpu.VMEM((tm, D_FF), BF)],
        compiler_params=_cparams(1),
        name="ffn",
    )(*args)


def _gelu(x):
    return 0.5 * x * (1.0 + lax.erf(x * (1.0 / math.sqrt(2.0))))


def _mix_in_kernel(x_ref, mod_ref, wall_ref, ca_ref, sb_ref, ck_ref, gq_ref, wq_ref, gkv_ref, wk_ref, wvt_ref,
                   vone_ref, lng_ref, lnb_ref, wsg_ref, bsg_ref,
                   p3_ref, ysg_ref, q_ref, k_ref, vt_ref, *, tm):
    x = x_ref[...]
    shift = mod_ref[3:4, :]
    scale = mod_ref[4:5, :]
    hb = (_rms(x) * (1.0 + scale) + shift).astype(BF)
    proj = _dot(hb, wall_ref[...])
    p3_ref[...] = proj[:, 0:OFF_SG]

    z = _gelu(proj[:, OFF_SG:OFF_Q])
    u = z[:, 0:SG_C]
    v = z[:, SG_C:2 * SG_C]
    mu = jnp.mean(v, axis=-1, keepdims=True)
    vc = v - mu
    var = jnp.mean(vc * vc, axis=-1, keepdims=True)
    vb = ((vc * lax.rsqrt(var + EPS)) * lng_ref[...] + lnb_ref[...]).astype(BF)
    lane_g = lax.broadcasted_iota(jnp.int32, (SG_CHUNK, SG_C), 1) // (SG_C // SG_GROUPS)
    for j in range(tm // SG_CHUNK):
        rows = slice(j * SG_CHUNK, (j + 1) * SG_CHUNK)
        o = _dot(wsg_ref[...], vb[rows, :])
        s = o[(SG_GROUPS - 1) * SG_CHUNK:, :]
        for g in range(SG_GROUPS - 2, -1, -1):
            s = jnp.where(lane_g == g, o[g * SG_CHUNK:(g + 1) * SG_CHUNK, :], s)
        ysg_ref[rows, :] = (u[rows, :] * (s + bsg_ref[...])).astype(BF)

    cqn = (_rms(proj[:, OFF_Q:OFF_KV]) * gq_ref[...]).astype(BF)
    qq = _dot(cqn, wq_ref[...])
    ca = ca_ref[...]
    sb = sb_ref[...]
    for h in range(MLA_HEADS):
        cols = slice(h * HEAD_PAD, (h + 1) * HEAD_PAD)
        blk = qq[:, cols]
        q_ref[:, cols] = (blk * ca + pltpu.roll(blk, HEAD_PAD - MLA_ROPE, 1) * sb).astype(BF)

    kvn = (_rms(proj[:, OFF_KV:OFF_KR]) * gkv_ref[...]).astype(BF)
    vt_ref[...] = (lax.dot_general(wvt_ref[...], kvn, _NT, preferred_element_type=F32) + vone_ref[...]).astype(BF)
    kv = _dot(kvn, wk_ref[...])
    t = proj[:, OFF_KR:OFF_KR + LANE] * ck_ref[...]
    krr = t + pltpu.roll(t, LANE - MLA_ROPE, 1)
    lane = lax.broadcasted_iota(jnp.int32, (tm, LANE), 1)
    in_rope = (lane >= MLA_NOPE) & (lane < MLA_NOPE + MLA_ROPE)
    krp = jnp.where(in_rope, pltpu.roll(krr, MLA_NOPE, 1), 0.0)
    for h in range(MLA_HEADS):
        cols = slice(h * HEAD_PAD, (h + 1) * HEAD_PAD)
        k_ref[:, cols] = (kv[:, cols] + krp).astype(BF)


def _mix_in(x, mod_l, cond_row, seq, w, layer, vone, tabs, *, tm):
    t, d = x.shape
    tiles_per_seq = seq // tm
    const = lambda i: (0, 0)
    tab = lambda i: (i % tiles_per_seq, 0)
    nq = MLA_HEADS * HEAD_PAD
    row = lambda i: (i, 0)
    wspec = lambda *shape: pl.BlockSpec((None,) + shape, lambda i: (layer, 0, 0))
    return pl.pallas_call(
        functools.partial(_mix_in_kernel, tm=tm),
        grid=(t // tm,),
        in_specs=[
            pl.BlockSpec((tm, d), row),
            _mod_spec(d, seq, tm, cond_row),
            wspec(d, N_IN_PAD),
            pl.BlockSpec((tm, LANE), tab),
            pl.BlockSpec((tm, LANE), tab),
            pl.BlockSpec((tm, LANE), tab),
            wspec(1, MLA_Q_RANK),
            wspec(MLA_Q_RANK, nq),
            wspec(1, MLA_KV_RANK),
            wspec(MLA_KV_RANK, nq),
            wspec(MLA_HEADS * V_ROWS, MLA_KV_RANK),
            pl.BlockSpec((MLA_HEADS * V_ROWS, 1), const),
            wspec(1, SG_C),
            wspec(1, SG_C),
            wspec(SG_GROUPS * SG_CHUNK, SG_CHUNK),
            wspec(SG_CHUNK, SG_C),
        ],
        out_specs=[
            pl.BlockSpec((tm, OFF_SG), row),
            pl.BlockSpec((tm, SG_C), row),
            pl.BlockSpec((tm, nq), row),
            pl.BlockSpec((tm, nq), row),
            pl.BlockSpec((MLA_HEADS * V_ROWS, tm), lambda i: (0, i)),
        ],
        out_shape=[
            jax.ShapeDtypeStruct((t, OFF_SG), F32),
            jax.ShapeDtypeStruct((t, SG_C), BF),
            jax.ShapeDtypeStruct((t, nq), BF),
            jax.ShapeDtypeStruct((t, nq), BF),
            jax.ShapeDtypeStruct((MLA_HEADS * V_ROWS, t), BF),
        ],
        compiler_params=_cparams(1),
        name="mix_in",
    )(x, mod_l, w["w_all"], tabs["ca"], tabs["sb"], tabs["ck"], w["gq"], w["wq"], w["gkv"], w["wk"], w["wvt"], vone,
      w["lng"], w["lnb"], w["wsg"], w["bsg"])


KEY_BLOCK = 128


def _attn_kernel(*refs, n_seg, tq):
    q_ref = refs[0]
    segs = [(refs[1 + 2 * s], refs[2 + 2 * s]) for s in range(n_seg)]
    o_ref, p_buf, m_buf = refs[1 + 2 * n_seg:]
    offs = [0]
    for k_ref, _ in segs:
        offs.append(offs[-1] + k_ref.shape[0])
    blocks = [(s, j, min(KEY_BLOCK, offs[s + 1] - offs[s]))
              for s in range(n_seg) for j in range(0, offs[s + 1] - offs[s], KEY_BLOCK)]
    stages = [(t, h) for t in range(q_ref.shape[0] // tq) for h in range(MLA_HEADS)]

    def scores(i):
        t, h = stages[i]
        cols = slice(h * HEAD_PAD, (h + 1) * HEAD_PAD)
        qt = q_ref[t * tq:(t + 1) * tq, cols].astype(F32).T.astype(BF)
        for b, (s, j, n) in enumerate(blocks):
            st = _dot(segs[s][0][j:j + n, cols], qt)
            mb = jnp.max(st, axis=0, keepdims=True)
            m_buf[i % 2, b:b + 1, :] = mb
            p_buf[i % 2, offs[s] + j:offs[s] + j + n, :] = jnp.exp2(st - mb).astype(BF)

    def values(i, prev):
        t, h = stages[i]
        mall = m_buf[i % 2, 0:len(blocks), :]
        w = jnp.exp2(mall - jnp.max(mall, axis=0, keepdims=True))
        acc = None
        for b, (s, j, n) in enumerate(blocks):
            part = _dot(segs[s][1][h * V_ROWS:(h + 1) * V_ROWS, j:j + n],
                        p_buf[i % 2, offs[s] + j:offs[s] + j + n, :]) * w[b:b + 1, :]
            acc = part if acc is None else acc + part
        o = acc[0:MLA_V, :] / acc[MLA_V:MLA_V + 1, :]
        if h % 2 == 0:
            return o
        pair = h // 2
        o_ref[t * tq:(t + 1) * tq, pair * HEAD_PAD:(pair + 1) * HEAD_PAD] = (
            jnp.concatenate([prev, o], axis=0).T.astype(BF))
        return None

    prev = None
    n = len(stages)
    for step in range(n + 1):
        if step < n:
            scores(step)
        if step >= 1:
            prev = values(step - 1, prev)


def _attention(q, segs, *, batch, nq_seq, tq, tiles_per_step):
    t = q.shape[0]
    rows = tq * tiles_per_step
    nqt = nq_seq // rows
    width = MLA_HEADS * HEAD_PAD
    in_specs = [pl.BlockSpec((rows, width), lambda b, i: (b * nqt + i, 0))]
    args = [q]
    nk_total = n_blocks = 0
    for k, vt, nk in segs:
        in_specs += [pl.BlockSpec((nk, width), lambda b, i: (b, 0)),
                     pl.BlockSpec((MLA_HEADS * V_ROWS, nk), lambda b, i: (0, b))]
        args += [k, vt]
        nk_total += nk
        n_blocks += pl.cdiv(nk, KEY_BLOCK)
    assert all(nk % min(KEY_BLOCK, nk) == 0 for _, _, nk in segs)
    return pl.pallas_call(
        functools.partial(_attn_kernel, n_seg=len(segs), tq=tq),
        grid=(batch, nqt),
        in_specs=in_specs,
        out_specs=pl.BlockSpec((rows, MLA_HEADS * MLA_V), lambda b, i: (b * nqt + i, 0)),
        out_shape=jax.ShapeDtypeStruct((t, MLA_HEADS * MLA_V), BF),
        scratch_shapes=[pltpu.VMEM((2, nk_total, tq), BF),
                        pltpu.VMEM((2, 8 * pl.cdiv(n_blocks, 8), tq), F32)],
        compiler_params=_cparams(2),
        name="attn_lat" if len(segs) > 1 else "attn_ctx",
    )(*args)


def _filt_kernel(et_ref, w1_ref, b1_ref, f0_ref, w2_ref, b2_ref, f1_ref, w3_ref, win_ref, o_ref):
    e = et_ref[...].astype(BF)
    h = jnp.sin(f0_ref[...] * (_dot(w1_ref[...], e) + b1_ref[...]))
    h = jnp.sin(f1_ref[...] * (_dot(w2_ref[...], h.astype(BF)) + b2_ref[...]))
    win = win_ref[...]
    o_ref[...] = _dot(w3_ref[0], h.astype(BF)) * jnp.concatenate([win, win], axis=0)


def _filter_time(tabs, fw, layer, *, pb):
    n2l = tabs["et"].shape[1]
    nb = n2l // pb
    col = lambda i: (0, i)
    rows = 2 * HY_C
    wspec = lambda *shape: pl.BlockSpec((None,) + shape, lambda i: (layer, 0, 0))
    return pl.pallas_call(
        _filt_kernel,
        grid=(nb,),
        in_specs=[
            pl.BlockSpec((HY_FFN, pb), col),
            wspec(HY_FFN, HY_FFN),
            wspec(HY_FFN, 1),
            wspec(HY_FFN, 1),
            wspec(HY_FFN, HY_FFN),
            wspec(HY_FFN, 1),
            wspec(HY_FFN, 1),
            pl.BlockSpec((None, 1, rows, HY_FFN), lambda i: (layer, i // (nb // 2), 0, 0)),
            pl.BlockSpec((HY_C, pb), col),
        ],
        out_specs=pl.BlockSpec((rows, pb), col),
        out_shape=jax.ShapeDtypeStruct((rows, n2l), F32),
        compiler_params=_cparams(1),
        name="hyena_filter",
    )(tabs["et"], fw["w1t"], fw["b1"], fw["f0"], fw["w2t"], fw["b2"], fw["f1"], fw["w3t"], tabs["win"])


DFT_CG = 8


def _dft_stage1(m1, rhs, tw_ref, lhs_ref, c0, n1f):
    r = _dot(m1, rhs)
    tr = tw_ref[0]
    ti = tw_ref[1]
    for c in range(DFT_CG):
        br = r[0:n1f, c * LANE:(c + 1) * LANE]
        bi = r[n1f:2 * n1f, c * LANE:(c + 1) * LANE]
        q0 = (c0 + c) * n1f
        lhs_ref[q0:q0 + n1f, 0:LANE] = (br * tr - bi * ti).astype(BF)
        lhs_ref[q0:q0 + n1f, LANE:2 * LANE] = (br * ti + bi * tr).astype(BF)


def _kf_kernel(k_ref, m1_ref, tw_ref, g2_ref, o_ref, lhs_ref, *, cb, n1f):
    zero_at = n1f // 2
    rowi = lax.broadcasted_iota(jnp.int32, (n1f, DFT_CG * LANE), 0)
    lanei = lax.broadcasted_iota(jnp.int32, (n1f, DFT_CG * LANE), 1)
    pad_tap = (rowi == zero_at) & ((lanei & (LANE - 1)) == 0)
    for c0 in range(0, cb, DFT_CG):
        k = jnp.concatenate([k_ref[c0 + c] for c in range(DFT_CG)], axis=1)
        _dft_stage1(m1_ref[...], jnp.where(pad_tap, 0.0, k).astype(BF), tw_ref, lhs_ref, c0, n1f)
    o_ref[...] = _dot(lhs_ref[...], g2_ref[...])


def _filter_spectrum(ktime3, consts, *, cb):
    rows, n1f, _ = ktime3.shape
    const2 = lambda i: (0, 0)
    return pl.pallas_call(
        functools.partial(_kf_kernel, cb=cb, n1f=n1f),
        grid=(rows // cb,),
        in_specs=[
            pl.BlockSpec((cb, n1f, LANE), lambda i: (i, 0, 0)),
            pl.BlockSpec((2 * n1f, n1f), const2),
            pl.BlockSpec((2, n1f, LANE), lambda i: (0, 0, 0)),
            pl.BlockSpec((2 * LANE, 2 * LANE), const2),
        ],
        out_specs=pl.BlockSpec((cb * n1f, 2 * LANE), lambda i: (i, 0)),
        out_shape=jax.ShapeDtypeStruct((rows * n1f, 2 * LANE), F32),
        scratch_shapes=[pltpu.VMEM((cb * n1f, 2 * LANE), BF)],
        compiler_params=_cparams(1),
        name="hyena_filter_spectrum",
    )(ktime3, consts["m1f"], consts["tw"], consts["g2"])


def _hy_lat_kernel(x1_ref, x2_ref, v_ref, cw_ref, hb_ref, kf_ref, m1_ref, tw_ref, g2_ref, g2i_ref, m1i_ref,
                   o_ref, zx1, zx2, zv, vbuf, lhs_ref, c_ref, *, cb, n1h, rb):
    n1f = 2 * n1h
    r = cb * n1h
    rowi = lax.broadcasted_iota(jnp.int32, (r, LANE), 0)
    lanei = lax.broadcasted_iota(jnp.int32, (r, LANE), 1)
    n1i = rowi & (n1h - 1)
    first = (n1i == 0) & (lanei == 0)
    last = (n1i == n1h - 1) & (lanei == LANE - 1)

    def bro(p):
        return jnp.broadcast_to(p, (cb, n1h, LANE)).reshape(r, LANE)

    def sconv(u3, s):
        u = u3.reshape(r, LANE)
        a = pltpu.roll(u, 1, 1)
        prev = jnp.where(lanei == 0, pltpu.roll(a, 1, 0), a)
        prev = jnp.where(first, 0.0, prev)
        f = pltpu.roll(u, LANE - 1, 1)
        nxt = jnp.where(lanei == LANE - 1, pltpu.roll(f, r - 1, 0), f)
        nxt = jnp.where(last, 0.0, nxt)
        return prev * bro(cw_ref[s, 0]) + u * bro(cw_ref[s, 1]) + nxt * bro(cw_ref[s, 2]) + bro(cw_ref[s, 3])

    for b in range(2):
        zx1[b] = sconv(x1_ref[b], 0)
        zx2[b] = sconv(x2_ref[b], 1)
        zv[b] = sconv(v_ref[b], 2)

    cg = DFT_CG
    gr = cg * n1h

    def conv_order(o, vin, gate, write):
        for c0 in range(0, cb, cg):
            zr = vin[0, c0 * n1h:c0 * n1h + gr, :]
            zi = vin[1, c0 * n1h:c0 * n1h + gr, :]
            rhs = jnp.concatenate([
                jnp.concatenate([zr[c * n1h:(c + 1) * n1h] for c in range(cg)], axis=1),
                jnp.concatenate([zi[c * n1h:(c + 1) * n1h] for c in range(cg)], axis=1)], axis=0).astype(BF)
            _dft_stage1(m1_ref[...], rhs, tw_ref, lhs_ref, c0, n1f)

        for q in range(0, cb * n1f, rb):
            x = _dot(lhs_ref[q:q + rb, :], g2_ref[...])
            kf = kf_ref[o, q:q + rb, :]
            xr, xi = x[:, 0:LANE], x[:, LANE:]
            kr, ki = kf[:, 0:LANE], kf[:, LANE:]
            y = jnp.concatenate([xr * kr - xi * ki, xr * ki + xi * kr], axis=1).astype(BF)
            c_ref[q:q + rb, :] = _dot(y, g2i_ref[...])

        tr = tw_ref[0]
        ti = tw_ref[1]
        for c0 in range(0, cb, cg):
            ars, ais = [], []
            for c in range(cg):
                t = c_ref[(c0 + c) * n1f:(c0 + c + 1) * n1f, :]
                cr, ci = t[:, 0:LANE], t[:, LANE:]
                ars.append(cr * tr + ci * ti)
                ais.append(ci * tr - cr * ti)
            rhs = jnp.concatenate([jnp.concatenate(ars, axis=1), jnp.concatenate(ais, axis=1)], axis=0).astype(BF)
            out = _dot(m1i_ref[...], rhs)
            r0 = c0 * n1h
            bias = jnp.broadcast_to(hb_ref[o, c0:c0 + cg], (cg, n1h, LANE)).reshape(gr, LANE)
            for b in range(2):
                ob = out[b * n1h:(b + 1) * n1h, :]
                y = jnp.concatenate([ob[:, c * LANE:(c + 1) * LANE] for c in range(cg)], axis=0)
                write(b, c0, r0, gate[b, r0:r0 + gr, :] * (y + vin[b, r0:r0 + gr, :] * bias))

    def write_mid(b, c0, r0, val):
        vbuf[b, r0:r0 + gr, :] = val

    def write_out(b, c0, r0, val):
        o_ref[b, c0:c0 + cg] = val.reshape(cg, n1h, LANE).astype(o_ref.dtype)

    conv_order(0, zv, zx1, write_mid)
    conv_order(1, vbuf, zx2, write_out)


def _hyena_lat(p3t, cw, hb, kf, consts, *, cb, rb):
    batch, _, n1h, _ = p3t.shape
    n1f = 2 * n1h
    ncb = HY_C // cb
    r = cb * n1h
    const2 = lambda p, c: (0, 0)
    sec = lambda k: pl.BlockSpec((2, cb, n1h, LANE), lambda p, c, k=k: (p, c + k * ncb, 0, 0))
    return pl.pallas_call(
        functools.partial(_hy_lat_kernel, cb=cb, n1h=n1h, rb=rb),
        grid=(batch // 2, ncb),
        in_specs=[
            sec(0), sec(1), sec(2),
            pl.BlockSpec((3, 4, cb, 1, LANE), lambda p, c: (0, 0, c, 0, 0)),
            pl.BlockSpec((2, cb, 1, LANE), lambda p, c: (0, c, 0, 0)),
            pl.BlockSpec((2, cb * n1f, 2 * LANE), lambda p, c: (0, c, 0)),
            pl.BlockSpec((2 * n1f, n1f), const2),
            pl.BlockSpec((2, n1f, LANE), lambda p, c: (0, 0, 0)),
            pl.BlockSpec((2 * LANE, 2 * LANE), const2),
            pl.BlockSpec((2 * LANE, 2 * LANE), const2),
            pl.BlockSpec((n1f, 2 * n1f), const2),
        ],
        out_specs=pl.BlockSpec((2, cb, n1h, LANE), lambda p, c: (p, c, 0, 0)),
        out_shape=jax.ShapeDtypeStruct((batch, HY_C, n1h, LANE), BF),
        scratch_shapes=[
            pltpu.VMEM((2, r, LANE), F32), pltpu.VMEM((2, r, LANE), F32),
            pltpu.VMEM((2, r, LANE), F32), pltpu.VMEM((2, r, LANE), F32),
            pltpu.VMEM((cb * n1f, 2 * LANE), BF), pltpu.VMEM((cb * n1f, 2 * LANE), F32),
        ],
        compiler_params=_cparams(2),
        name="hyena_lat",
    )(p3t, p3t, p3t, cw, hb, kf, consts["m1"], consts["tw"], consts["g2"], consts["g2i"], consts["m1i"])


def _hy_ctx_kernel(x1_ref, x2_ref, v_ref, cw_ref, hb_ref, kt_ref, mf_ref, mff_ref, mi_ref, o_ref, *, lc):
    lanei = lax.broadcasted_iota(jnp.int32, (HY_C, lc), 1)

    def sconv(u, s):
        prev = jnp.where(lanei == 0, 0.0, pltpu.roll(u, 1, 1))
        nxt = jnp.where(lanei == lc - 1, 0.0, pltpu.roll(u, lc - 1, 1))
        return prev * cw_ref[s, 0] + u * cw_ref[s, 1] + nxt * cw_ref[s, 2] + cw_ref[s, 3]

    zx1 = [sconv(x1_ref[b], 0) for b in range(2)]
    zx2 = [sconv(x2_ref[b], 1) for b in range(2)]
    v = [sconv(v_ref[b], 2) for b in range(2)]

    kt = kt_ref[...]
    kl = lax.broadcasted_iota(jnp.int32, kt.shape, 1)
    kf = _dot(jnp.where(kl == lc, 0.0, kt).astype(BF), mff_ref[...])
    n = 2 * lc
    for o in range(2):
        z = jnp.concatenate([v[0], v[1]], axis=1).astype(BF)
        x = _dot(z, mf_ref[...])
        xr, xi = x[:, 0:n], x[:, n:]
        kr = kf[o * HY_C:(o + 1) * HY_C, 0:n]
        ki = kf[o * HY_C:(o + 1) * HY_C, n:]
        y = _dot(jnp.concatenate([xr * kr - xi * ki, xr * ki + xi * kr], axis=1).astype(BF), mi_ref[...])
        gate = zx1 if o == 0 else zx2
        v = [gate[b] * (y[:, b * lc:(b + 1) * lc] + v[b] * hb_ref[o]) for b in range(2)]
    for b in range(2):
        o_ref[b] = v[b].astype(o_ref.dtype)


def _hyena_ctx(p3t, cw, hb, kt, consts):
    batch, _, lc = p3t.shape
    n = 2 * lc
    sec = lambda k: pl.BlockSpec((2, HY_C, lc), lambda p, k=k: (p, k, 0))
    return pl.pallas_call(
        functools.partial(_hy_ctx_kernel, lc=lc),
        grid=(batch // 2,),
        in_specs=[
            sec(0), sec(1), sec(2),
            pl.BlockSpec((3, 4, HY_C, 1), lambda p: (0, 0, 0, 0)),
            pl.BlockSpec((2, HY_C, 1), lambda p: (0, 0, 0)),
            pl.BlockSpec((2 * HY_C, n), lambda p: (0, 0)),
            pl.BlockSpec((n, 2 * n), lambda p: (0, 0)),
            pl.BlockSpec((n, 2 * n), lambda p: (0, 0)),
            pl.BlockSpec((2 * n, n), lambda p: (0, 0)),
        ],
        out_specs=pl.BlockSpec((2, HY_C, lc), lambda p: (p, 0, 0)),
        out_shape=jax.ShapeDtypeStruct((batch, HY_C, lc), BF),
        compiler_params=_cparams(1),
        name="hyena_ctx",
    )(p3t, p3t, p3t, cw, hb, kt, consts["mf"], consts["mff"], consts["mi"])


def _mxu_const(a):
    return jnp.asarray(a, F32).astype(BF)


def _dft_consts_two_stage(n1f):
    n2 = DFT_N2
    n = n1f * n2
    n1h = n1f // 2
    k1 = np.arange(n1f)
    ang1 = 2.0 * np.pi * np.outer(k1, k1) / n1f
    f1r, f1i = np.cos(ang1), -np.sin(ang1)
    m1 = np.block([[f1r[:, :n1h], -f1i[:, :n1h]], [f1i[:, :n1h], f1r[:, :n1h]]])
    m1f = np.concatenate([f1r, f1i], axis=0)
    angt = 2.0 * np.pi * np.outer(k1, np.arange(n2)) / n
    tw = np.stack([np.cos(angt), -np.sin(angt)])
    a2 = 2.0 * np.pi * np.outer(np.arange(n2), np.arange(n2)) / n2
    f2r, f2i = np.cos(a2), -np.sin(a2)
    g2 = np.block([[f2r, f2i], [-f2i, f2r]])
    g2i = np.block([[f2r, -f2i], [f2i, f2r]])
    m1i = np.block([[f1r[:n1h, :], f1i[:n1h, :]], [-f1i[:n1h, :], f1r[:n1h, :]]]) / n
    return {
        "m1": _mxu_const(m1), "m1f": _mxu_const(m1f), "tw": jnp.asarray(tw, F32),
        "g2": _mxu_const(g2), "g2i": _mxu_const(g2i), "m1i": _mxu_const(m1i),
    }


def _dft_consts_single(lc):
    n = 2 * lc
    ang = 2.0 * np.pi * np.outer(np.arange(n), np.arange(n)) / n
    fr, fi = np.cos(ang), -np.sin(ang)
    mf = np.block([[fr[:lc], fi[:lc]], [-fi[:lc], fr[:lc]]])
    mff = np.concatenate([fr, fi], axis=1)
    gr, gi = fr[:, :lc] / n, -fi[:, :lc] / n
    mi = np.block([[gr, gi], [-gi, gr]])
    return {"mf": _mxu_const(mf), "mff": _mxu_const(mff), "mi": _mxu_const(mi)}


def _filter_tables(seq):
    f32 = np.float32
    pos = np.arange(seq, dtype=f32)
    t = pos / f32(max(seq - 1, 1))
    w = f32(2.0 * math.pi) * pos / f32(seq)
    bands = np.linspace(1e-4, HY_BANDS - 1, HY_BANDS, dtype=f32)
    ang = w[:, None] * bands[None, :]
    z = np.concatenate([t[:, None], np.cos(ang), -np.sin(ang)], axis=-1)
    min_decay = abs(math.log(HY_DECAY_TARGET) / HY_SLOW_PCT)
    max_decay = abs(math.log(HY_DECAY_TARGET) / HY_FAST_PCT)
    deltas = np.linspace(min_decay, max_decay, HY_C, dtype=f32)
    window = np.exp(-t[:, None] * deltas[None, :]) + f32(HY_WINDOW_SHIFT)
    idx = np.concatenate([np.arange(seq), np.array([0]), np.arange(seq - 1, 0, -1)])
    et = np.pad(z[idx], ((0, 0), (0, HY_FFN - HY_EMB))).T
    win = window[idx].T
    return {"et": jnp.asarray(et, F32), "win": jnp.asarray(win, F32)}


def _rope_tables(n_lat, n_ctx):
    f32 = np.float32
    n_rows = n_lat // GRID_W
    rows = np.repeat(np.arange(n_rows, dtype=f32), GRID_W)
    cols = np.tile(np.arange(GRID_W, dtype=f32), n_rows)
    quarter = MLA_ROPE // 4
    inv = (f32(1.0) / np.power(f32(ROPE_BASE), np.arange(quarter, dtype=f32) / f32(quarter))).astype(f32)
    ang = np.stack([rows[:, None] * inv, cols[:, None] * inv], axis=1)
    r = np.arange(MLA_ROPE)
    axis, half, i = r // (2 * quarter), (r % (2 * quarter)) // quarter, r % quarter
    cos32 = np.cos(ang)[:, axis, i]
    ssin32 = np.sin(ang)[:, axis, i] * np.where(half == 0, -1.0, 1.0).astype(f32)
    ones = np.ones((n_lat, MLA_NOPE), f32)
    zpad = np.zeros((n_lat, HEAD_PAD - MLA_NOPE - MLA_ROPE), f32)
    z64 = np.zeros((n_lat, MLA_NOPE), f32)
    qs = f32(Q_SCALE)
    lat = {
        "ca": qs * np.concatenate([ones, cos32, zpad], axis=1),
        "sb": qs * np.concatenate([z64, ssin32, zpad], axis=1),
        "ck": np.concatenate([cos32, ssin32, z64], axis=1),
    }
    one_c = np.ones((n_ctx, MLA_NOPE + MLA_ROPE), f32)
    ctx = {
        "ca": qs * np.concatenate([one_c, np.zeros((n_ctx, HEAD_PAD - MLA_NOPE - MLA_ROPE), f32)], axis=1),
        "sb": np.zeros((n_ctx, HEAD_PAD), f32),
        "ck": np.concatenate([np.ones((n_ctx, MLA_ROPE), f32), np.zeros((n_ctx, LANE - MLA_ROPE), f32)], axis=1),
    }
    to_dev = lambda d: {k: jnp.asarray(v, F32) for k, v in d.items()}
    return to_dev(lat), to_dev(ctx)


def _rope_swap():
    quarter = MLA_ROPE // 4
    r = np.arange(MLA_ROPE)
    half = (r % (2 * quarter)) // quarter
    return np.where(half == 0, r + quarter, r - quarter)


def _mix_weights(w_in, mla_q_norm, mla_w_uq, mla_kv_norm, mla_w_ukv, sg_ln_g, sg_ln_b, sg_w, sg_b):
    return jax.vmap(_mix_weights_layer)(w_in, mla_q_norm, mla_w_uq, mla_kv_norm, mla_w_ukv, sg_ln_g, sg_ln_b, sg_w, sg_b)


def _mix_weights_layer(wi, q_norm, w_uq, kv_norm, w_ukv, ln_g, ln_b, w_s, b_s):
    sw = _rope_swap()
    kr = wi[:, OFF_KR:N_IN]
    w_all = jnp.concatenate(
        [wi, kr[:, sw], jnp.zeros((D_MODEL, N_IN_PAD - N_IN - MLA_ROPE), F32)], axis=1).astype(BF)
    dq = MLA_NOPE + MLA_ROPE
    uq = w_uq.reshape(MLA_Q_RANK, MLA_HEADS, dq)
    assert dq + MLA_ROPE == HEAD_PAD
    wq = jnp.concatenate([uq, uq[:, :, MLA_NOPE:][:, :, sw]], axis=2).reshape(MLA_Q_RANK, -1).astype(BF)
    ukv = w_ukv.reshape(MLA_KV_RANK, MLA_HEADS, MLA_NOPE + MLA_V)
    wk = jnp.pad(ukv[:, :, :MLA_NOPE], ((0, 0), (0, 0), (0, HEAD_PAD - MLA_NOPE))).reshape(MLA_KV_RANK, -1)
    wvt = jnp.pad(ukv[:, :, MLA_NOPE:], ((0, 0), (0, 0), (0, V_ROWS - MLA_V))).reshape(MLA_KV_RANK, -1).T
    return {
        "w_all": w_all,
        "gq": q_norm.reshape(1, -1), "wq": wq,
        "gkv": kv_norm.reshape(1, -1), "wk": wk.astype(BF), "wvt": wvt.astype(BF),
        "lng": ln_g.reshape(1, -1), "lnb": ln_b.reshape(1, -1),
        "wsg": w_s.reshape(SG_GROUPS * SG_CHUNK, SG_CHUNK).astype(BF),
        "bsg": jnp.repeat(b_s.T, SG_C // SG_GROUPS, axis=1),
    }


def _value_ones_rows():
    vone = np.zeros((MLA_HEADS, V_ROWS, 1), np.float32)
    vone[:, MLA_V, 0] = 1.0
    return jnp.asarray(vone.reshape(MLA_HEADS * V_ROWS, 1))


def _filter_weights(hy_w1, hy_b1, hy_w2, hy_b2, hy_w3, hy_freq):
    depth = hy_w1.shape[0]
    w1t = jnp.swapaxes(jnp.pad(hy_w1, ((0, 0), (0, HY_FFN - HY_EMB), (0, 0))), 1, 2).astype(BF)
    w3 = hy_w3.reshape(depth, HY_FFN, 2, 2, HY_C)
    w3t = jnp.transpose(w3, (0, 3, 2, 4, 1)).reshape(depth, 2, 2 * HY_C, HY_FFN).astype(BF)
    col = lambda a: a.reshape(depth, HY_FFN, 1)
    return {"w1t": w1t, "b1": col(hy_b1), "f0": col(hy_freq[:, 0]),
            "w2t": jnp.swapaxes(hy_w2, 1, 2).astype(BF), "b2": col(hy_b2), "f1": col(hy_freq[:, 1]), "w3t": w3t}


def kernel(x, c, ctx, c_ctx, ada_w, ada_b, ffn1_wg, ffn1_wu, ffn1_wd, w_in, w_out, hy_conv_w, hy_conv_b, hy_w1, hy_b1, hy_w2, hy_b2, hy_w3, hy_freq, hy_bias, sg_ln_g, sg_ln_b, sg_w, sg_b, mla_q_norm, mla_w_uq, mla_kv_norm, mla_w_ukv, ffn2_wg, ffn2_wu, ffn2_wd, final_norm):
    batch, n_lat, d = x.shape
    n_ctx = ctx.shape[1]
    depth = ada_w.shape[0]
    assert d == D_MODEL and batch % 2 == 0 and batch <= 4
    assert n_lat % (2 * LANE) == 0 and n_ctx % SG_CHUNK == 0
    n1h = n_lat // LANE
    ctx_row = batch

    cc = jnp.concatenate([c, c_ctx[None, :], jnp.zeros((8 - batch - 1, d), F32)], axis=0)
    mods = _modulation(cc, ada_w, ada_b).reshape(depth, 8, N_MOD, d)

    rope_lat, rope_ctx = _rope_tables(n_lat, n_ctx)
    ftab_lat = _filter_tables(n_lat)
    ftab_ctx = _filter_tables(n_ctx)
    dft_lat = _dft_consts_two_stage(2 * n1h)
    dft_ctx = _dft_consts_single(n_ctx)

    xl = x.reshape(batch * n_lat, d)
    xc = ctx.reshape(batch * n_ctx, d)
    tm = TOKEN_TILE
    ffn_tm = FFN_TILE
    assert n_lat % FFN_TILE == 0 and (batch * n_ctx) % FFN_TILE == 0 and n_lat % TOKEN_TILE == 0
    ffn1_bf = [w.astype(BF) for w in (ffn1_wg, ffn1_wu, ffn1_wd)]
    ffn2_bf = [w.astype(BF) for w in (ffn2_wg, ffn2_wu, ffn2_wd)]
    wo = w_out.astype(BF)
    mw = _mix_weights(w_in, mla_q_norm, mla_w_uq, mla_kv_norm, mla_w_ukv, sg_ln_g, sg_ln_b, sg_w, sg_b)
    vone = _value_ones_rows()
    fw = _filter_weights(hy_w1, hy_b1, hy_w2, hy_b2, hy_w3, hy_freq)
    cw_all = jnp.concatenate([hy_conv_w, hy_conv_b[:, None, :]], axis=1)
    cw_all = jnp.transpose(cw_all.reshape(depth, 4, 3, HY_C), (0, 2, 1, 3))
    for l in range(depth):
        last = l == depth - 1
        mod_l = mods[l]

        xl = _ffn(xl, mod_l, 0, l, *ffn1_bf, tm=ffn_tm, seq=n_lat)
        xc = _ffn(xc, mod_l, 0, l, *ffn1_bf, tm=ffn_tm, seq=n_ctx, cond_row=ctx_row)

        p3, ysg, q, k, vt = _mix_in(xl, mod_l, None, n_lat, mw, l, vone, rope_lat, tm=tm)
        p3c, ysgc, qc, kc, vtc = _mix_in(xc, mod_l, ctx_row, n_ctx, mw, l, vone, rope_ctx, tm=n_ctx)

        y_attn = _attention(q, [(kc, vtc, n_ctx), (k, vt, n_lat)], batch=batch, nq_seq=n_lat, tq=ATTN_Q_TILE,
                            tiles_per_step=1)

        cw = cw_all[l]
        ktime = _filter_time(ftab_lat, fw, l, pb=FILTER_PB)
        hb = hy_bias[l] + ktime[:, n_lat].reshape(2, HY_C)
        kf = _filter_spectrum(ktime.reshape(2 * HY_C, 2 * n1h, LANE), dft_lat, cb=HYENA_CB)
        p3t = jnp.transpose(p3.reshape(batch, n_lat, OFF_SG), (0, 2, 1)).reshape(batch, OFF_SG, n1h, LANE)
        y_hy = _hyena_lat(
            p3t,
            jnp.broadcast_to(cw[:, :, :, None, None], (3, 4, HY_C, 1, LANE)),
            jnp.broadcast_to(hb[:, :, None, None], (2, HY_C, 1, LANE)),
            kf.reshape(2, HY_C * 2 * n1h, 2 * LANE), dft_lat, cb=HYENA_CB, rb=HYENA_RB)
        y_hy = jnp.transpose(y_hy.reshape(batch, HY_C, n_lat), (0, 2, 1)).reshape(batch * n_lat, HY_C)

        xl = _ffn(xl, mod_l, 6, l, *ffn2_bf, tm=ffn_tm, seq=n_lat, mixer=(y_hy, ysg, y_attn, wo),
                  final_norm=final_norm if last else None)

        if not last:
            ktime_c = _filter_time(ftab_ctx, fw, l, pb=n_ctx)
            hbc = hy_bias[l] + ktime_c[:, n_ctx].reshape(2, HY_C)
            p3ct = jnp.transpose(p3c.reshape(batch, n_ctx, OFF_SG), (0, 2, 1))
            yc_hy = _hyena_ctx(p3ct, cw[:, :, :, None], hbc[:, :, None], ktime_c, dft_ctx)
            yc_hy = jnp.transpose(yc_hy, (0, 2, 1)).reshape(batch * n_ctx, HY_C)
            yc_attn = _attention(qc, [(kc, vtc, n_ctx)], batch=batch, nq_seq=n_ctx, tq=n_ctx, tiles_per_step=1)
            xc = _ffn(xc, mod_l, 6, l, *ffn2_bf, tm=ffn_tm, seq=n_ctx, cond_row=ctx_row,
                      mixer=(yc_hy, ysgc, yc_attn, wo))
    return xl.reshape(batch, n_lat, d)
```

```python
import functools
import math

import numpy as np
import jax
import jax.numpy as jnp
from jax import lax
from jax.experimental import pallas as pl
from jax.experimental.pallas import tpu as pltpu

F32 = jnp.float32
BF = jnp.bfloat16

D_MODEL = 1024
D_FF = 2816
N_MOD = 9
EPS = 1e-6
GRID_W = 64

HY_C = 256
HY_BANDS = 16
HY_EMB = 1 + 2 * HY_BANDS
HY_FFN = 64
HY_DECAY_TARGET = 1e-2
HY_FAST_PCT = 0.3
HY_SLOW_PCT = 1.5
HY_WINDOW_SHIFT = 0.05

SG_C = 256
SG_GROUPS = 4
SG_CHUNK = 128

MLA_HEADS = 8
MLA_NOPE = 64
MLA_ROPE = 32
MLA_V = 64
MLA_Q_RANK = 384
MLA_KV_RANK = 256
MLA_SCALE = (MLA_NOPE + MLA_ROPE) ** -0.5
Q_SCALE = MLA_SCALE * math.log2(math.e)
ROPE_BASE = 10000.0

OFF_SG = 3 * HY_C
OFF_Q = OFF_SG + 2 * SG_C
OFF_KV = OFF_Q + MLA_Q_RANK
OFF_KR = OFF_KV + MLA_KV_RANK
N_IN = OFF_KR + MLA_ROPE
N_IN_PAD = 2048

LANE = 128
HEAD_PAD = 128
V_ROWS = 80
FFN_CHUNK = 256
VMEM_LIMIT = 56 * 1024 * 1024

TOKEN_TILE = 1024
FFN_TILE = 1024
ATTN_Q_TILE = 256
HYENA_CB = 32
HYENA_RB = 512
FILTER_PB = 1024

DFT_N2 = LANE


def _cparams(n_axes, vmem=VMEM_LIMIT):
    return pltpu.CompilerParams(dimension_semantics=("arbitrary",) * n_axes, vmem_limit_bytes=vmem)


def _mod_spec(d, seq, tm, cond_row):
    if cond_row is None:
        tiles_per_seq = seq // tm
        return pl.BlockSpec((None, N_MOD, d), lambda i: (i // tiles_per_seq, 0, 0))
    return pl.BlockSpec((None, N_MOD, d), lambda i: (cond_row, 0, 0))


def _rms(x):
    return x * lax.rsqrt(jnp.mean(x * x, axis=-1, keepdims=True) + EPS)


def _dot(a, b):
    return jnp.dot(a, b, preferred_element_type=F32)


_NT = (((1,), (1,)), ((), ()))


def _mod_kernel(c_ref, w_ref, b_ref, o_ref):
    c = c_ref[...]
    s = c * jax.nn.sigmoid(c)
    o_ref[0] = _dot(s.astype(BF), w_ref[0].astype(BF)) + b_ref[0]


def _modulation(cc, ada_w, ada_b):
    depth, d, n = ada_w.shape
    tn = 1152
    return pl.pallas_call(
        _mod_kernel,
        grid=(depth, n // tn),
        in_specs=[
            pl.BlockSpec((8, d), lambda l, j: (0, 0)),
            pl.BlockSpec((1, d, tn), lambda l, j: (l, 0, j)),
            pl.BlockSpec((1, 1, tn), lambda l, j: (l, 0, j)),
        ],
        out_specs=pl.BlockSpec((1, 8, tn), lambda l, j: (l, 0, j)),
        out_shape=jax.ShapeDtypeStruct((depth, 8, n), F32),
        compiler_params=_cparams(2),
        name="adaln_mod",
    )(cc, ada_w, ada_b.reshape(depth, 1, n))


def _ffn_kernel(*refs, k0, mixer_out, final):
    refs = list(refs)
    x_ref, mod_ref = refs[:2]
    del refs[:2]
    if mixer_out:
        yh_ref, ys_ref, ya_ref, wo_ref = refs[:4]
        del refs[:4]
    wg_ref, wu_ref, wd_ref = refs[:3]
    del refs[:3]
    if final:
        fn_ref = refs.pop(0)
    o_ref, a_ref = refs
    x = x_ref[...]
    if mixer_out:
        acc = _dot(yh_ref[...].astype(BF), wo_ref[0:HY_C, :])
        acc = acc + _dot(ys_ref[...], wo_ref[HY_C:HY_C + SG_C, :])
        acc = acc + _dot(ya_ref[...], wo_ref[HY_C + SG_C:, :])
        x = x + mod_ref[5:6, :] * acc
    shift = mod_ref[k0:k0 + 1, :]
    scale = mod_ref[k0 + 1:k0 + 2, :]
    gate = mod_ref[k0 + 2:k0 + 3, :]
    hb = (_rms(x) * (1.0 + scale) + shift).astype(BF)
    for j in range(D_FF // FFN_CHUNK):
        cols = slice(j * FFN_CHUNK, (j + 1) * FFN_CHUNK)
        g = _dot(hb, wg_ref[:, cols])
        u = _dot(hb, wu_ref[:, cols])
        a_ref[:, cols] = (g * jax.nn.sigmoid(g) * u).astype(BF)
    y = x + (0.5 * gate) * _dot(a_ref[...], wd_ref[...])
    if final:
        y = _rms(y) * fn_ref[...]
    o_ref[...] = y


def _ffn(x, mod_l, k0, layer, wg, wu, wd, *, tm, seq, cond_row=None, mixer=None, final_norm=None):
    t, d = x.shape
    final = final_norm is not None
    const = lambda i: (0, 0)
    row = lambda i: (i, 0)
    slab = lambda i: (layer, 0, 0)
    in_specs = [pl.BlockSpec((tm, d), row), _mod_spec(d, seq, tm, cond_row)]
    args = [x, mod_l]
    if mixer is not None:
        in_specs += [
            pl.BlockSpec((tm, HY_C), row),
            pl.BlockSpec((tm, SG_C), row),
            pl.BlockSpec((tm, MLA_HEADS * MLA_V), row),
            pl.BlockSpec((None, HY_C + SG_C + MLA_HEADS * MLA_V, d), slab, pipeline_mode=pl.Buffered(1)),
        ]
        args += list(mixer)
    in_specs += [
        pl.BlockSpec((None, d, D_FF), slab, pipeline_mode=pl.Buffered(1)),
        pl.BlockSpec((None, d, D_FF), slab, pipeline_mode=pl.Buffered(1)),
        pl.BlockSpec((None, D_FF, d), slab, pipeline_mode=pl.Buffered(1)),
    ]
    args += [wg, wu, wd]
    if final:
        in_specs.append(pl.BlockSpec((1, d), const))
        args.append(final_norm.reshape(1, d))
    return pl.pallas_call(
        functools.partial(_ffn_kernel, k0=k0, mixer_out=mixer is not None, final=final),
        grid=(t // tm,),
        in_specs=in_specs,
        out_specs=pl.BlockSpec((tm, d), lambda i: (i, 0)),
        out_shape=jax.ShapeDtypeStruct((t, d), F32),
        scratch_shapes=[pltpu.VMEM((tm, D_FF), BF)],
        compiler_params=_cparams(1),
        name="ffn",
    )(*args)


def _gelu(x):
    return 0.5 * x * (1.0 + lax.erf(x * (1.0 / math.sqrt(2.0))))


def _mix_in_kernel(x_ref, mod_ref, wall_ref, ca_ref, sb_ref, ck_ref, gq_ref, wq_ref, gkv_ref, wk_ref, wvt_ref,
                   vone_ref, lng_ref, lnb_ref, wsg_ref, bsg_ref,
                   p3_ref, ysg_ref, q_ref, k_ref, vt_ref, *, tm):
    x = x_ref[...]
    shift = mod_ref[3:4, :]
    scale = mod_ref[4:5, :]
    hb = (_rms(x) * (1.0 + scale) + shift).astype(BF)
    proj = _dot(hb, wall_ref[...])
    p3_ref[...] = proj[:, 0:OFF_SG]

    z = _gelu(proj[:, OFF_SG:OFF_Q])
    u = z[:, 0:SG_C]
    v = z[:, SG_C:2 * SG_C]
    mu = jnp.mean(v, axis=-1, keepdims=True)
    vc = v - mu
    var = jnp.mean(vc * vc, axis=-1, keepdims=True)
    vb = ((vc * lax.rsqrt(var + EPS)) * lng_ref[...] + lnb_ref[...]).astype(BF)
    lane_g = lax.broadcasted_iota(jnp.int32, (SG_CHUNK, SG_C), 1) // (SG_C // SG_GROUPS)
    for j in range(tm // SG_CHUNK):
        rows = slice(j * SG_CHUNK, (j + 1) * SG_CHUNK)
        o = _dot(wsg_ref[...], vb[rows, :])
        s = o[(SG_GROUPS - 1) * SG_CHUNK:, :]
        for g in range(SG_GROUPS - 2, -1, -1):
            s = jnp.where(lane_g == g, o[g * SG_CHUNK:(g + 1) * SG_CHUNK, :], s)
        ysg_ref[rows, :] = (u[rows, :] * (s + bsg_ref[...])).astype(BF)

    cqn = (_rms(proj[:, OFF_Q:OFF_KV]) * gq_ref[...]).astype(BF)
    qq = _dot(cqn, wq_ref[...])
    ca = ca_ref[...]
    sb = sb_ref[...]
    for h in range(MLA_HEADS):
        cols = slice(h * HEAD_PAD, (h + 1) * HEAD_PAD)
        blk = qq[:, cols]
        q_ref[:, cols] = (blk * ca + pltpu.roll(blk, HEAD_PAD - MLA_ROPE, 1) * sb).astype(BF)

    kvn = (_rms(proj[:, OFF_KV:OFF_KR]) * gkv_ref[...]).astype(BF)
    vt_ref[...] = (lax.dot_general(wvt_ref[...], kvn, _NT, preferred_element_type=F32) + vone_ref[...]).astype(BF)
    kv = _dot(kvn, wk_ref[...])
    t = proj[:, OFF_KR:OFF_KR + LANE] * ck_ref[...]
    krr = t + pltpu.roll(t, LANE - MLA_ROPE, 1)
    lane = lax.broadcasted_iota(jnp.int32, (tm, LANE), 1)
    in_rope = (lane >= MLA_NOPE) & (lane < MLA_NOPE + MLA_ROPE)
    krp = jnp.where(in_rope, pltpu.roll(krr, MLA_NOPE, 1), 0.0)
    for h in range(MLA_HEADS):
        cols = slice(h * HEAD_PAD, (h + 1) * HEAD_PAD)
        k_ref[:, cols] = (kv[:, cols] + krp).astype(BF)


def _mix_in(x, mod_l, cond_row, seq, w, layer, vone, tabs, *, tm):
    t, d = x.shape
    tiles_per_seq = seq // tm
    const = lambda i: (0, 0)
    tab = lambda i: (i % tiles_per_seq, 0)
    nq = MLA_HEADS * HEAD_PAD
    row = lambda i: (i, 0)
    wspec = lambda *shape: pl.BlockSpec((None,) + shape, lambda i: (layer, 0, 0))
    return pl.pallas_call(
        functools.partial(_mix_in_kernel, tm=tm),
        grid=(t // tm,),
        in_specs=[
            pl.BlockSpec((tm, d), row),
            _mod_spec(d, seq, tm, cond_row),
            wspec(d, N_IN_PAD),
            pl.BlockSpec((tm, LANE), tab),
            pl.BlockSpec((tm, LANE), tab),
            pl.BlockSpec((tm, LANE), tab),
            wspec(1, MLA_Q_RANK),
            wspec(MLA_Q_RANK, nq),
            wspec(1, MLA_KV_RANK),
            wspec(MLA_KV_RANK, nq),
            wspec(MLA_HEADS * V_ROWS, MLA_KV_RANK),
            pl.BlockSpec((MLA_HEADS * V_ROWS, 1), const),
            wspec(1, SG_C),
            wspec(1, SG_C),
            wspec(SG_GROUPS * SG_CHUNK, SG_CHUNK),
            wspec(SG_CHUNK, SG_C),
        ],
        out_specs=[
            pl.BlockSpec((tm, OFF_SG), row),
            pl.BlockSpec((tm, SG_C), row),
            pl.BlockSpec((tm, nq), row),
            pl.BlockSpec((tm, nq), row),
            pl.BlockSpec((MLA_HEADS * V_ROWS, tm), lambda i: (0, i)),
        ],
        out_shape=[
            jax.ShapeDtypeStruct((t, OFF_SG), F32),
            jax.ShapeDtypeStruct((t, SG_C), BF),
            jax.ShapeDtypeStruct((t, nq), BF),
            jax.ShapeDtypeStruct((t, nq), BF),
            jax.ShapeDtypeStruct((MLA_HEADS * V_ROWS, t), BF),
        ],
        compiler_params=_cparams(1),
        name="mix_in",
    )(x, mod_l, w["w_all"], tabs["ca"], tabs["sb"], tabs["ck"], w["gq"], w["wq"], w["gkv"], w["wk"], w["wvt"], vone,
      w["lng"], w["lnb"], w["wsg"], w["bsg"])


KEY_BLOCK = 128


def _attn_kernel(*refs, n_seg, tq):
    q_ref = refs[0]
    segs = [(refs[1 + 2 * s], refs[2 + 2 * s]) for s in range(n_seg)]
    o_ref, p_buf, m_buf = refs[1 + 2 * n_seg:]
    offs = [0]
    for k_ref, _ in segs:
        offs.append(offs[-1] + k_ref.shape[0])
    blocks = [(s, j, min(KEY_BLOCK, offs[s + 1] - offs[s]))
              for s in range(n_seg) for j in range(0, offs[s + 1] - offs[s], KEY_BLOCK)]
    stages = [(t, h) for t in range(q_ref.shape[0] // tq) for h in range(MLA_HEADS)]

    def scores(i):
        t, h = stages[i]
        cols = slice(h * HEAD_PAD, (h + 1) * HEAD_PAD)
        qt = q_ref[t * tq:(t + 1) * tq, cols].astype(F32).T.astype(BF)
        for b, (s, j, n) in enumerate(blocks):
            st = _dot(segs[s][0][j:j + n, cols], qt)
            mb = jnp.max(st, axis=0, keepdims=True)
            m_buf[i % 2, b:b + 1, :] = mb
            p_buf[i % 2, offs[s] + j:offs[s] + j + n, :] = jnp.exp2(st - mb).astype(BF)

    def values(i, prev):
        t, h = stages[i]
        mall = m_buf[i % 2, 0:len(blocks), :]
        w = jnp.exp2(mall - jnp.max(mall, axis=0, keepdims=True))
        acc = None
        for b, (s, j, n) in enumerate(blocks):
            part = _dot(segs[s][1][h * V_ROWS:(h + 1) * V_ROWS, j:j + n],
                        p_buf[i % 2, offs[s] + j:offs[s] + j + n, :]) * w[b:b + 1, :]
            acc = part if acc is None else acc + part
        o = acc[0:MLA_V, :] / acc[MLA_V:MLA_V + 1, :]
        if h % 2 == 0:
            return o
        pair = h // 2
        o_ref[t * tq:(t + 1) * tq, pair * HEAD_PAD:(pair + 1) * HEAD_PAD] = (
            jnp.concatenate([prev, o], axis=0).T.astype(BF))
        return None

    prev = None
    n = len(stages)
    for step in range(n + 1):
        if step < n:
            scores(step)
        if step >= 1:
            prev = values(step - 1, prev)


def _attention(q, segs, *, batch, nq_seq, tq, tiles_per_step):
    t = q.shape[0]
    rows = tq * tiles_per_step
    nqt = nq_seq // rows
    width = MLA_HEADS * HEAD_PAD
    in_specs = [pl.BlockSpec((rows, width), lambda b, i: (b * nqt + i, 0))]
    args = [q]
    nk_total = n_blocks = 0
    for k, vt, nk in segs:
        in_specs += [pl.BlockSpec((nk, width), lambda b, i: (b, 0)),
                     pl.BlockSpec((MLA_HEADS * V_ROWS, nk), lambda b, i: (0, b))]
        args += [k, vt]
        nk_total += nk
        n_blocks += pl.cdiv(nk, KEY_BLOCK)
    assert all(nk % min(KEY_BLOCK, nk) == 0 for _, _, nk in segs)
    return pl.pallas_call(
        functools.partial(_attn_kernel, n_seg=len(segs), tq=tq),
        grid=(batch, nqt),
        in_specs=in_specs,
        out_specs=pl.BlockSpec((rows, MLA_HEADS * MLA_V), lambda b, i: (b * nqt + i, 0)),
        out_shape=jax.ShapeDtypeStruct((t, MLA_HEADS * MLA_V), BF),
        scratch_shapes=[pltpu.VMEM((2, nk_total, tq), BF),
                        pltpu.VMEM((2, 8 * pl.cdiv(n_blocks, 8), tq), F32)],
        compiler_params=_cparams(2),
        name="attn_lat" if len(segs) > 1 else "attn_ctx",
    )(*args)


def _filt_kernel(et_ref, w1_ref, b1_ref, f0_ref, w2_ref, b2_ref, f1_ref, w3_ref, win_ref, o_ref):
    e = et_ref[...].astype(BF)
    h = jnp.sin(f0_ref[...] * (_dot(w1_ref[...], e) + b1_ref[...]))
    h = jnp.sin(f1_ref[...] * (_dot(w2_ref[...], h.astype(BF)) + b2_ref[...]))
    win = win_ref[...]
    o_ref[...] = _dot(w3_ref[0], h.astype(BF)) * jnp.concatenate([win, win], axis=0)


def _filter_time(tabs, fw, layer, *, pb):
    n2l = tabs["et"].shape[1]
    nb = n2l // pb
    col = lambda i: (0, i)
    rows = 2 * HY_C
    wspec = lambda *shape: pl.BlockSpec((None,) + shape, lambda i: (layer, 0, 0))
    return pl.pallas_call(
        _filt_kernel,
        grid=(nb,),
        in_specs=[
            pl.BlockSpec((HY_FFN, pb), col),
            wspec(HY_FFN, HY_FFN),
            wspec(HY_FFN, 1),
            wspec(HY_FFN, 1),
            wspec(HY_FFN, HY_FFN),
            wspec(HY_FFN, 1),
            wspec(HY_FFN, 1),
            pl.BlockSpec((None, 1, rows, HY_FFN), lambda i: (layer, i // (nb // 2), 0, 0)),
            pl.BlockSpec((HY_C, pb), col),
        ],
        out_specs=pl.BlockSpec((rows, pb), col),
        out_shape=jax.ShapeDtypeStruct((rows, n2l), F32),
        compiler_params=_cparams(1),
        name="hyena_filter",
    )(tabs["et"], fw["w1t"], fw["b1"], fw["f0"], fw["w2t"], fw["b2"], fw["f1"], fw["w3t"], tabs["win"])


DFT_CG = 8


def _dft_stage1(m1, rhs, tw_ref, lhs_ref, c0, n1f):
    r = _dot(m1, rhs)
    tr = tw_ref[0]
    ti = tw_ref[1]
    for c in range(DFT_CG):
        br = r[0:n1f, c * LANE:(c + 1) * LANE]
        bi = r[n1f:2 * n1f, c * LANE:(c + 1) * LANE]
        q0 = (c0 + c) * n1f
        lhs_ref[q0:q0 + n1f, 0:LANE] = (br * tr - bi * ti).astype(BF)
        lhs_ref[q0:q0 + n1f, LANE:2 * LANE] = (br * ti + bi * tr).astype(BF)


def _kf_kernel(k_ref, m1_ref, tw_ref, g2_ref, o_ref, lhs_ref, *, cb, n1f):
    zero_at = n1f // 2
    rowi = lax.broadcasted_iota(jnp.int32, (n1f, DFT_CG * LANE), 0)
    lanei = lax.broadcasted_iota(jnp.int32, (n1f, DFT_CG * LANE), 1)
    pad_tap = (rowi == zero_at) & ((lanei & (LANE - 1)) == 0)
    for c0 in range(0, cb, DFT_CG):
        k = jnp.concatenate([k_ref[c0 + c] for c in range(DFT_CG)], axis=1)
        _dft_stage1(m1_ref[...], jnp.where(pad_tap, 0.0, k).astype(BF), tw_ref, lhs_ref, c0, n1f)
    o_ref[...] = _dot(lhs_ref[...], g2_ref[...])


def _filter_spectrum(ktime3, consts, *, cb):
    rows, n1f, _ = ktime3.shape
    const2 = lambda i: (0, 0)
    return pl.pallas_call(
        functools.partial(_kf_kernel, cb=cb, n1f=n1f),
        grid=(rows // cb,),
        in_specs=[
            pl.BlockSpec((cb, n1f, LANE), lambda i: (i, 0, 0)),
            pl.BlockSpec((2 * n1f, n1f), const2),
            pl.BlockSpec((2, n1f, LANE), lambda i: (0, 0, 0)),
            pl.BlockSpec((2 * LANE, 2 * LANE), const2),
        ],
        out_specs=pl.BlockSpec((cb * n1f, 2 * LANE), lambda i: (i, 0)),
        out_shape=jax.ShapeDtypeStruct((rows * n1f, 2 * LANE), F32),
        scratch_shapes=[pltpu.VMEM((cb * n1f, 2 * LANE), BF)],
        compiler_params=_cparams(1),
        name="hyena_filter_spectrum",
    )(ktime3, consts["m1f"], consts["tw"], consts["g2"])


def _hy_lat_kernel(x1_ref, x2_ref, v_ref, cw_ref, hb_ref, kf_ref, m1_ref, tw_ref, g2_ref, g2i_ref, m1i_ref,
                   o_ref, zx1, zx2, zv, vbuf, lhs_ref, c_ref, *, cb, n1h, rb):
    n1f = 2 * n1h
    r = cb * n1h
    rowi = lax.broadcasted_iota(jnp.int32, (r, LANE), 0)
    lanei = lax.broadcasted_iota(jnp.int32, (r, LANE), 1)
    n1i = rowi & (n1h - 1)
    first = (n1i == 0) & (lanei == 0)
    last = (n1i == n1h - 1) & (lanei == LANE - 1)

    def bro(p):
        return jnp.broadcast_to(p, (cb, n1h, LANE)).reshape(r, LANE)

    def sconv(u3, s):
        u = u3.reshape(r, LANE)
        a = pltpu.roll(u, 1, 1)
        prev = jnp.where(lanei == 0, pltpu.roll(a, 1, 0), a)
        prev = jnp.where(first, 0.0, prev)
        f = pltpu.roll(u, LANE - 1, 1)
        nxt = jnp.where(lanei == LANE - 1, pltpu.roll(f, r - 1, 0), f)
        nxt = jnp.where(last, 0.0, nxt)
        return prev * bro(cw_ref[s, 0]) + u * bro(cw_ref[s, 1]) + nxt * bro(cw_ref[s, 2]) + bro(cw_ref[s, 3])

    for b in range(2):
        zx1[b] = sconv(x1_ref[b], 0)
        zx2[b] = sconv(x2_ref[b], 1)
        zv[b] = sconv(v_ref[b], 2)

    cg = DFT_CG
    gr = cg * n1h

    def conv_order(o, vin, gate, write):
        for c0 in range(0, cb, cg):
            zr = vin[0, c0 * n1h:c0 * n1h + gr, :]
            zi = vin[1, c0 * n1h:c0 * n1h + gr, :]
            rhs = jnp.concatenate([
                jnp.concatenate([zr[c * n1h:(c + 1) * n1h] for c in range(cg)], axis=1),
                jnp.concatenate([zi[c * n1h:(c + 1) * n1h] for c in range(cg)], axis=1)], axis=0).astype(BF)
            _dft_stage1(m1_ref[...], rhs, tw_ref, lhs_ref, c0, n1f)

        for q in range(0, cb * n1f, rb):
            x = _dot(lhs_ref[q:q + rb, :], g2_ref[...])
            kf = kf_ref[o, q:q + rb, :]
            xr, xi = x[:, 0:LANE], x[:, LANE:]
            kr, ki = kf[:, 0:LANE], kf[:, LANE:]
            y = jnp.concatenate([xr * kr - xi * ki, xr * ki + xi * kr], axis=1).astype(BF)
            c_ref[q:q + rb, :] = _dot(y, g2i_ref[...])

        tr = tw_ref[0]
        ti = tw_ref[1]
        for c0 in range(0, cb, cg):
            ars, ais = [], []
            for c in range(cg):
                t = c_ref[(c0 + c) * n1f:(c0 + c + 1) * n1f, :]
                cr, ci = t[:, 0:LANE], t[:, LANE:]
                ars.append(cr * tr + ci * ti)
                ais.append(ci * tr - cr * ti)
            rhs = jnp.concatenate([jnp.concatenate(ars, axis=1), jnp.concatenate(ais, axis=1)], axis=0).astype(BF)
            out = _dot(m1i_ref[...], rhs)
            r0 = c0 * n1h
            bias = jnp.broadcast_to(hb_ref[o, c0:c0 + cg], (cg, n1h, LANE)).reshape(gr, LANE)
            for b in range(2):
                ob = out[b * n1h:(b + 1) * n1h, :]
                y = jnp.concatenate([ob[:, c * LANE:(c + 1) * LANE] for c in range(cg)], axis=0)
                write(b, c0, r0, gate[b, r0:r0 + gr, :] * (y + vin[b, r0:r0 + gr, :] * bias))

    def write_mid(b, c0, r0, val):
        vbuf[b, r0:r0 + gr, :] = val

    def write_out(b, c0, r0, val):
        o_ref[b, c0:c0 + cg] = val.reshape(cg, n1h, LANE).astype(o_ref.dtype)

    conv_order(0, zv, zx1, write_mid)
    conv_order(1, vbuf, zx2, write_out)


def _hyena_lat(p3t, cw, hb, kf, consts, *, cb, rb):
    batch, _, n1h, _ = p3t.shape
    n1f = 2 * n1h
    ncb = HY_C // cb
    r = cb * n1h
    const2 = lambda p, c: (0, 0)
    sec = lambda k: pl.BlockSpec((2, cb, n1h, LANE), lambda p, c, k=k: (p, c + k * ncb, 0, 0))
    return pl.pallas_call(
        functools.partial(_hy_lat_kernel, cb=cb, n1h=n1h, rb=rb),
        grid=(batch // 2, ncb),
        in_specs=[
            sec(0), sec(1), sec(2),
            pl.BlockSpec((3, 4, cb, 1, LANE), lambda p, c: (0, 0, c, 0, 0)),
            pl.BlockSpec((2, cb, 1, LANE), lambda p, c: (0, c, 0, 0)),
            pl.BlockSpec((2, cb * n1f, 2 * LANE), lambda p, c: (0, c, 0)),
            pl.BlockSpec((2 * n1f, n1f), const2),
            pl.BlockSpec((2, n1f, LANE), lambda p, c: (0, 0, 0)),
            pl.BlockSpec((2 * LANE, 2 * LANE), const2),
            pl.BlockSpec((2 * LANE, 2 * LANE), const2),
            pl.BlockSpec((n1f, 2 * n1f), const2),
        ],
        out_specs=pl.BlockSpec((2, cb, n1h, LANE), lambda p, c: (p, c, 0, 0)),
        out_shape=jax.ShapeDtypeStruct((batch, HY_C, n1h, LANE), BF),
        scratch_shapes=[
            pltpu.VMEM((2, r, LANE), F32), pltpu.VMEM((2, r, LANE), F32),
            pltpu.VMEM((2, r, LANE), F32), pltpu.VMEM((2, r, LANE), F32),
            pltpu.VMEM((cb * n1f, 2 * LANE), BF), pltpu.VMEM((cb * n1f, 2 * LANE), F32),
        ],
        compiler_params=_cparams(2),
        name="hyena_lat",
    )(p3t, p3t, p3t, cw, hb, kf, consts["m1"], consts["tw"], consts["g2"], consts["g2i"], consts["m1i"])


def _hy_ctx_kernel(x1_ref, x2_ref, v_ref, cw_ref, hb_ref, kt_ref, mf_ref, mff_ref, mi_ref, o_ref, *, lc):
    lanei = lax.broadcasted_iota(jnp.int32, (HY_C, lc), 1)

    def sconv(u, s):
        prev = jnp.where(lanei == 0, 0.0, pltpu.roll(u, 1, 1))
        nxt = jnp.where(lanei == lc - 1, 0.0, pltpu.roll(u, lc - 1, 1))
        return prev * cw_ref[s, 0] + u * cw_ref[s, 1] + nxt * cw_ref[s, 2] + cw_ref[s, 3]

    zx1 = [sconv(x1_ref[b], 0) for b in range(2)]
    zx2 = [sconv(x2_ref[b], 1) for b in range(2)]
    v = [sconv(v_ref[b], 2) for b in range(2)]

    kt = kt_ref[...]
    kl = lax.broadcasted_iota(jnp.int32, kt.shape, 1)
    kf = _dot(jnp.where(kl == lc, 0.0, kt).astype(BF), mff_ref[...])
    n = 2 * lc
    for o in range(2):
        z = jnp.concatenate([v[0], v[1]], axis=1).astype(BF)
        x = _dot(z, mf_ref[...])
        xr, xi = x[:, 0:n], x[:, n:]
        kr = kf[o * HY_C:(o + 1) * HY_C, 0:n]
        ki = kf[o * HY_C:(o + 1) * HY_C, n:]
        y = _dot(jnp.concatenate([xr * kr - xi * ki, xr * ki + xi * kr], axis=1).astype(BF), mi_ref[...])
        gate = zx1 if o == 0 else zx2
        v = [gate[b] * (y[:, b * lc:(b + 1) * lc] + v[b] * hb_ref[o]) for b in range(2)]
    for b in range(2):
        o_ref[b] = v[b].astype(o_ref.dtype)


def _hyena_ctx(p3t, cw, hb, kt, consts):
    batch, _, lc = p3t.shape
    n = 2 * lc
    sec = lambda k: pl.BlockSpec((2, HY_C, lc), lambda p, k=k: (p, k, 0))
    return pl.pallas_call(
        functools.partial(_hy_ctx_kernel, lc=lc),
        grid=(batch // 2,),
        in_specs=[
            sec(0), sec(1), sec(2),
            pl.BlockSpec((3, 4, HY_C, 1), lambda p: (0, 0, 0, 0)),
            pl.BlockSpec((2, HY_C, 1), lambda p: (0, 0, 0)),
            pl.BlockSpec((2 * HY_C, n), lambda p: (0, 0)),
            pl.BlockSpec((n, 2 * n), lambda p: (0, 0)),
            pl.BlockSpec((n, 2 * n), lambda p: (0, 0)),
            pl.BlockSpec((2 * n, n), lambda p: (0, 0)),
        ],
        out_specs=pl.BlockSpec((2, HY_C, lc), lambda p: (p, 0, 0)),
        out_shape=jax.ShapeDtypeStruct((batch, HY_C, lc), BF),
        compiler_params=_cparams(1),
        name="hyena_ctx",
    )(p3t, p3t, p3t, cw, hb, kt, consts["mf"], consts["mff"], consts["mi"])


def _mxu_const(a):
    return jnp.asarray(a, F32).astype(BF)


def _dft_consts_two_stage(n1f):
    n2 = DFT_N2
    n = n1f * n2
    n1h = n1f // 2
    k1 = np.arange(n1f)
    ang1 = 2.0 * np.pi * np.outer(k1, k1) / n1f
    f1r, f1i = np.cos(ang1), -np.sin(ang1)
    m1 = np.block([[f1r[:, :n1h], -f1i[:, :n1h]], [f1i[:, :n1h], f1r[:, :n1h]]])
    m1f = np.concatenate([f1r, f1i], axis=0)
    angt = 2.0 * np.pi * np.outer(k1, np.arange(n2)) / n
    tw = np.stack([np.cos(angt), -np.sin(angt)])
    a2 = 2.0 * np.pi * np.outer(np.arange(n2), np.arange(n2)) / n2
    f2r, f2i = np.cos(a2), -np.sin(a2)
    g2 = np.block([[f2r, f2i], [-f2i, f2r]])
    g2i = np.block([[f2r, -f2i], [f2i, f2r]])
    m1i = np.block([[f1r[:n1h, :], f1i[:n1h, :]], [-f1i[:n1h, :], f1r[:n1h, :]]]) / n
    return {
        "m1": _mxu_const(m1), "m1f": _mxu_const(m1f), "tw": jnp.asarray(tw, F32),
        "g2": _mxu_const(g2), "g2i": _mxu_const(g2i), "m1i": _mxu_const(m1i),
    }


def _dft_consts_single(lc):
    n = 2 * lc
    ang = 2.0 * np.pi * np.outer(np.arange(n), np.arange(n)) / n
    fr, fi = np.cos(ang), -np.sin(ang)
    mf = np.block([[fr[:lc], fi[:lc]], [-fi[:lc], fr[:lc]]])
    mff = np.concatenate([fr, fi], axis=1)
    gr, gi = fr[:, :lc] / n, -fi[:, :lc] / n
    mi = np.block([[gr, gi], [-gi, gr]])
    return {"mf": _mxu_const(mf), "mff": _mxu_const(mff), "mi": _mxu_const(mi)}


def _filter_tables(seq):
    f32 = np.float32
    pos = np.arange(seq, dtype=f32)
    t = pos / f32(max(seq - 1, 1))
    w = f32(2.0 * math.pi) * pos / f32(seq)
    bands = np.linspace(1e-4, HY_BANDS - 1, HY_BANDS, dtype=f32)
    ang = w[:, None] * bands[None, :]
    z = np.concatenate([t[:, None], np.cos(ang), -np.sin(ang)], axis=-1)
    min_decay = abs(math.log(HY_DECAY_TARGET) / HY_SLOW_PCT)
    max_decay = abs(math.log(HY_DECAY_TARGET) / HY_FAST_PCT)
    deltas = np.linspace(min_decay, max_decay, HY_C, dtype=f32)
    window = np.exp(-t[:, None] * deltas[None, :]) + f32(HY_WINDOW_SHIFT)
    idx = np.concatenate([np.arange(seq), np.array([0]), np.arange(seq - 1, 0, -1)])
    et = np.pad(z[idx], ((0, 0), (0, HY_FFN - HY_EMB))).T
    win = window[idx].T
    return {"et": jnp.asarray(et, F32), "win": jnp.asarray(win, F32)}


def _rope_tables(n_lat, n_ctx):
    f32 = np.float32
    n_rows = n_lat // GRID_W
    rows = np.repeat(np.arange(n_rows, dtype=f32), GRID_W)
    cols = np.tile(np.arange(GRID_W, dtype=f32), n_rows)
    quarter = MLA_ROPE // 4
    inv = (f32(1.0) / np.power(f32(ROPE_BASE), np.arange(quarter, dtype=f32) / f32(quarter))).astype(f32)
    ang = np.stack([rows[:, None] * inv, cols[:, None] * inv], axis=1)
    r = np.arange(MLA_ROPE)
    axis, half, i = r // (2 * quarter), (r % (2 * quarter)) // quarter, r % quarter
    cos32 = np.cos(ang)[:, axis, i]
    ssin32 = np.sin(ang)[:, axis, i] * np.where(half == 0, -1.0, 1.0).astype(f32)
    ones = np.ones((n_lat, MLA_NOPE), f32)
    zpad = np.zeros((n_lat, HEAD_PAD - MLA_NOPE - MLA_ROPE), f32)
    z64 = np.zeros((n_lat, MLA_NOPE), f32)
    qs = f32(Q_SCALE)
    lat = {
        "ca": qs * np.concatenate([ones, cos32, zpad], axis=1),
        "sb": qs * np.concatenate([z64, ssin32, zpad], axis=1),
        "ck": np.concatenate([cos32, ssin32, z64], axis=1),
    }
    one_c = np.ones((n_ctx, MLA_NOPE + MLA_ROPE), f32)
    ctx = {
        "ca": qs * np.concatenate([one_c, np.zeros((n_ctx, HEAD_PAD - MLA_NOPE - MLA_ROPE), f32)], axis=1),
        "sb": np.zeros((n_ctx, HEAD_PAD), f32),
        "ck": np.concatenate([np.ones((n_ctx, MLA_ROPE), f32), np.zeros((n_ctx, LANE - MLA_ROPE), f32)], axis=1),
    }
    to_dev = lambda d: {k: jnp.asarray(v, F32) for k, v in d.items()}
    return to_dev(lat), to_dev(ctx)


def _rope_swap():
    quarter = MLA_ROPE // 4
    r = np.arange(MLA_ROPE)
    half = (r % (2 * quarter)) // quarter
    return np.where(half == 0, r + quarter, r - quarter)


def _mix_weights(w_in, mla_q_norm, mla_w_uq, mla_kv_norm, mla_w_ukv, sg_ln_g, sg_ln_b, sg_w, sg_b):
    return jax.vmap(_mix_weights_layer)(w_in, mla_q_norm, mla_w_uq, mla_kv_norm, mla_w_ukv, sg_ln_g, sg_ln_b, sg_w, sg_b)


def _mix_weights_layer(wi, q_norm, w_uq, kv_norm, w_ukv, ln_g, ln_b, w_s, b_s):
    sw = _rope_swap()
    kr = wi[:, OFF_KR:N_IN]
    w_all = jnp.concatenate(
        [wi, kr[:, sw], jnp.zeros((D_MODEL, N_IN_PAD - N_IN - MLA_ROPE), F32)], axis=1).astype(BF)
    dq = MLA_NOPE + MLA_ROPE
    uq = w_uq.reshape(MLA_Q_RANK, MLA_HEADS, dq)
    assert dq + MLA_ROPE == HEAD_PAD
    wq = jnp.concatenate([uq, uq[:, :, MLA_NOPE:][:, :, sw]], axis=2).reshape(MLA_Q_RANK, -1).astype(BF)
    ukv = w_ukv.reshape(MLA_KV_RANK, MLA_HEADS, MLA_NOPE + MLA_V)
    wk = jnp.pad(ukv[:, :, :MLA_NOPE], ((0, 0), (0, 0), (0, HEAD_PAD - MLA_NOPE))).reshape(MLA_KV_RANK, -1)
    wvt = jnp.pad(ukv[:, :, MLA_NOPE:], ((0, 0), (0, 0), (0, V_ROWS - MLA_V))).reshape(MLA_KV_RANK, -1).T
    return {
        "w_all": w_all,
        "gq": q_norm.reshape(1, -1), "wq": wq,
        "gkv": kv_norm.reshape(1, -1), "wk": wk.astype(BF), "wvt": wvt.astype(BF),
        "lng": ln_g.reshape(1, -1), "lnb": ln_b.reshape(1, -1),
        "wsg": w_s.reshape(SG_GROUPS * SG_CHUNK, SG_CHUNK).astype(BF),
        "bsg": jnp.repeat(b_s.T, SG_C // SG_GROUPS, axis=1),
    }


def _value_ones_rows():
    vone = np.zeros((MLA_HEADS, V_ROWS, 1), np.float32)
    vone[:, MLA_V, 0] = 1.0
    return jnp.asarray(vone.reshape(MLA_HEADS * V_ROWS, 1))


def _filter_weights(hy_w1, hy_b1, hy_w2, hy_b2, hy_w3, hy_freq):
    depth = hy_w1.shape[0]
    w1t = jnp.swapaxes(jnp.pad(hy_w1, ((0, 0), (0, HY_FFN - HY_EMB), (0, 0))), 1, 2).astype(BF)
    w3 = hy_w3.reshape(depth, HY_FFN, 2, 2, HY_C)
    w3t = jnp.transpose(w3, (0, 3, 2, 4, 1)).reshape(depth, 2, 2 * HY_C, HY_FFN).astype(BF)
    col = lambda a: a.reshape(depth, HY_FFN, 1)
    return {"w1t": w1t, "b1": col(hy_b1), "f0": col(hy_freq[:, 0]),
            "w2t": jnp.swapaxes(hy_w2, 1, 2).astype(BF), "b2": col(hy_b2), "f1": col(hy_freq[:, 1]), "w3t": w3t}


def kernel(x, c, ctx, c_ctx, ada_w, ada_b, ffn1_wg, ffn1_wu, ffn1_wd, w_in, w_out, hy_conv_w, hy_conv_b, hy_w1, hy_b1, hy_w2, hy_b2, hy_w3, hy_freq, hy_bias, sg_ln_g, sg_ln_b, sg_w, sg_b, mla_q_norm, mla_w_uq, mla_kv_norm, mla_w_ukv, ffn2_wg, ffn2_wu, ffn2_wd, final_norm):
    batch, n_lat, d = x.shape
    n_ctx = ctx.shape[1]
    depth = ada_w.shape[0]
    assert d == D_MODEL and batch % 2 == 0 and batch <= 4
    assert n_lat % (2 * LANE) == 0 and n_ctx % SG_CHUNK == 0
    n1h = n_lat // LANE
    ctx_row = batch

    cc = jnp.concatenate([c, c_ctx[None, :], jnp.zeros((8 - batch - 1, d), F32)], axis=0)
    mods = _modulation(cc, ada_w, ada_b).reshape(depth, 8, N_MOD, d)

    rope_lat, rope_ctx = _rope_tables(n_lat, n_ctx)
    ftab_lat = _filter_tables(n_lat)
    ftab_ctx = _filter_tables(n_ctx)
    dft_lat = _dft_consts_two_stage(2 * n1h)
    dft_ctx = _dft_consts_single(n_ctx)

    xl = x.reshape(batch * n_lat, d)
    xc = ctx.reshape(batch * n_ctx, d)
    tm = TOKEN_TILE
    ffn_tm = FFN_TILE
    assert n_lat % FFN_TILE == 0 and (batch * n_ctx) % FFN_TILE == 0 and n_lat % TOKEN_TILE == 0
    ffn1_bf = [w.astype(BF) for w in (ffn1_wg, ffn1_wu, ffn1_wd)]
    ffn2_bf = [w.astype(BF) for w in (ffn2_wg, ffn2_wu, ffn2_wd)]
    wo = w_out.astype(BF)
    mw = _mix_weights(w_in, mla_q_norm, mla_w_uq, mla_kv_norm, mla_w_ukv, sg_ln_g, sg_ln_b, sg_w, sg_b)
    vone = _value_ones_rows()
    fw = _filter_weights(hy_w1, hy_b1, hy_w2, hy_b2, hy_w3, hy_freq)
    cw_all = jnp.concatenate([hy_conv_w, hy_conv_b[:, None, :]], axis=1)
    cw_all = jnp.transpose(cw_all.reshape(depth, 4, 3, HY_C), (0, 2, 1, 3))
    for l in range(depth):
        last = l == depth - 1
        mod_l = mods[l]

        xl = _ffn(xl, mod_l, 0, l, *ffn1_bf, tm=ffn_tm, seq=n_lat)
        xc = _ffn(xc, mod_l, 0, l, *ffn1_bf, tm=ffn_tm, seq=n_ctx, cond_row=ctx_row)

        p3, ysg, q, k, vt = _mix_in(xl, mod_l, None, n_lat, mw, l, vone, rope_lat, tm=tm)
        p3c, ysgc, qc, kc, vtc = _mix_in(xc, mod_l, ctx_row, n_ctx, mw, l, vone, rope_ctx, tm=n_ctx)

        y_attn = _attention(q, [(kc, vtc, n_ctx), (k, vt, n_lat)], batch=batch, nq_seq=n_lat, tq=ATTN_Q_TILE,
                            tiles_per_step=1)

        cw = cw_all[l]
        ktime = _filter_time(ftab_lat, fw, l, pb=FILTER_PB)
        hb = hy_bias[l] + ktime[:, n_lat].reshape(2, HY_C)
        kf = _filter_spectrum(ktime.reshape(2 * HY_C, 2 * n1h, LANE), dft_lat, cb=HYENA_CB)
        p3t = jnp.transpose(p3.reshape(batch, n_lat, OFF_SG), (0, 2, 1)).reshape(batch, OFF_SG, n1h, LANE)
        y_hy = _hyena_lat(
            p3t,
            jnp.broadcast_to(cw[:, :, :, None, None], (3, 4, HY_C, 1, LANE)),
            jnp.broadcast_to(hb[:, :, None, None], (2, HY_C, 1, LANE)),
            kf.reshape(2, HY_C * 2 * n1h, 2 * LANE), dft_lat, cb=HYENA_CB, rb=HYENA_RB)
        y_hy = jnp.transpose(y_hy.reshape(batch, HY_C, n_lat), (0, 2, 1)).reshape(batch * n_lat, HY_C)

        xl = _ffn(xl, mod_l, 6, l, *ffn2_bf, tm=ffn_tm, seq=n_lat, mixer=(y_hy, ysg, y_attn, wo),
                  final_norm=final_norm if last else None)

        if not last:
            ktime_c = _filter_time(ftab_ctx, fw, l, pb=n_ctx)
            hbc = hy_bias[l] + ktime_c[:, n_ctx].reshape(2, HY_C)
            p3ct = jnp.transpose(p3c.reshape(batch, n_ctx, OFF_SG), (0, 2, 1))
            yc_hy = _hyena_ctx(p3ct, cw[:, :, :, None], hbc[:, :, None], ktime_c, dft_ctx)
            yc_hy = jnp.transpose(yc_hy, (0, 2, 1)).reshape(batch * n_ctx, HY_C)
            yc_attn = _attention(qc, [(kc, vtc, n_ctx)], batch=batch, nq_seq=n_ctx, tq=n_ctx, tiles_per_step=1)
            xc = _ffn(xc, mod_l, 6, l, *ffn2_bf, tm=ffn_tm, seq=n_ctx, cond_row=ctx_row,
                      mixer=(yc_hy, ysgc, yc_attn, wo))
    return xl.reshape(batch, n_lat, d)
```
